```python
import math
import jax, jax.numpy as jnp
from jax import lax
import numpy as np

D_MODEL = 2048
BATCH = 8
SEQ = 2048
DEPTH = 4

GRID_W = 64
CTX_LEN = 256
N_MIXERS = 2
N_FOURIER_LAYERS = (DEPTH + 1) // 2
N_SSD_LAYERS = DEPTH // 2
FOURIER_GROUPS = 8
SSD_EXPAND = 2
SSD_D_INNER = SSD_EXPAND * D_MODEL
SSD_HEAD_DIM = 64
SSD_HEADS = SSD_D_INNER // SSD_HEAD_DIM
SSD_GROUPS = 8
SSD_STATE = 128
SSD_CONV = 3
SSD_CHUNK = 128
SSD_GN = SSD_GROUPS * SSD_STATE
SSD_STATE_COLS = SSD_D_INNER + SSD_GN + 2 * SSD_HEADS
SSD_IN_DIM = SSD_STATE_COLS + SSD_GN + SSD_D_INNER
SSD_CONV_DIM = SSD_D_INNER + 2 * SSD_GN
D_FF = 5632
FFN_CONV = 3
EPS = 1e-6
MOD_SCALE = 0.5
DT_MIN = 1e-3
DT_MAX = 1e-1
A_MIN = 1.0
A_MAX = 16.0

kernel_name = "hybrid_fourier_ssd_dit_trunk"


def rmsnorm(x, g):
    xf = x.astype(jnp.float32)
    y = xf * lax.rsqrt(jnp.mean(xf * xf, axis=-1, keepdims=True) + EPS)
    return (y * g.astype(jnp.float32)).astype(x.dtype)


def modulate(h, shift, scale):
    return h * (1 + scale) + shift


def dwconv1d(u, w, b):
    K = w.shape[0]
    pad = K // 2
    L = u.shape[1]
    up = jnp.pad(u, ((0, 0), (pad, pad), (0, 0)))
    out = up[:, 0:L] * w[0]
    for k in range(1, K):
        out = out + up[:, k:k + L] * w[k]
    return out + b


def dwconv2d_grid(u, w, b):
    Bsz, L, C = u.shape
    rows = L // GRID_W
    kh, kw = w.shape[0], w.shape[1]
    g = jnp.pad(u.reshape(Bsz, rows, GRID_W, C), ((0, 0), (kh // 2, kh // 2), (kw // 2, kw // 2), (0, 0)))
    out = b
    for i in range(kh):
        for j in range(kw):
            out = out + g[:, i:i + rows, j:j + GRID_W] * w[i, j]
    return out.reshape(Bsz, L, C)


def fourier_mix(h, w):
    Bsz, L, D = h.shape
    hg = h.astype(jnp.float32).reshape(Bsz, L, FOURIER_GROUPS, D // FOURIER_GROUPS)
    f = jnp.fft.fft2(hg, axes=(1, 3), norm="ortho").real
    return f.reshape(Bsz, L, D).astype(h.dtype) @ w


def conv_ffn(h, w_up, w_down, conv_fn):
    gate, val = jnp.split(h @ w_up, 2, axis=-1)
    return (jax.nn.silu(conv_fn(gate)) * val) @ w_down


def ssd_chunked(xs, dt, a, bm, cm, h0, with_output):
    f32 = jnp.float32
    Bsz, L, H, P = xs.shape
    G, N = bm.shape[-2], bm.shape[-1]
    R = H // G
    nc = L // SSD_CHUNK
    dtc = dt.astype(f32).reshape(Bsz, nc, SSD_CHUNK, G, R)
    xdt = xs.astype(f32).reshape(Bsz, nc, SSD_CHUNK, G, R, P) * dtc[..., None]
    acs = jnp.cumsum(dtc * a.astype(f32).reshape(G, R), axis=2)
    bmc = bm.astype(f32).reshape(Bsz, nc, SSD_CHUNK, G, N)
    decay_to_end = jnp.exp(acs[:, :, -1:] - acs)
    states = jnp.einsum('bcsgn,bcsgr,bcsgrp->bcgrpn', bmc, decay_to_end, xdt)

    def step(h, inp):
        s, d = inp
        return h * jnp.exp(d)[..., None, None] + s, h

    h_init = h0.astype(f32).reshape(Bsz, G, R, P, N)
    final, h_in = lax.scan(step, h_init, (jnp.moveaxis(states, 1, 0), jnp.moveaxis(acs[:, :, -1], 1, 0)))
    final = final.reshape(Bsz, H, P, N)
    if not with_output:
        return None, final
    h_in = jnp.moveaxis(h_in, 0, 1)
    cmc = cm.astype(f32).reshape(Bsz, nc, SSD_CHUNK, G, N)
    acs_t = jnp.moveaxis(acs, 2, -1)
    seg = acs_t[..., :, None] - acs_t[..., None, :]
    lower = jnp.tril(jnp.ones((SSD_CHUNK, SSD_CHUNK), dtype=bool))
    decay = jnp.exp(jnp.where(lower, seg, -jnp.inf))
    scores = jnp.einsum('bclgn,bcsgn->bcgls', cmc, bmc)
    y_diag = jnp.einsum('bcgls,bcgrls,bcsgrp->bclgrp', scores, decay, xdt)
    y_off = jnp.einsum('bclgn,bcgrpn,bclgr->bclgrp', cmc, h_in, jnp.exp(acs))
    return (y_diag + y_off).reshape(Bsz, L, H, P), final


def ssd_branch_inputs(h, w_in, conv_w, conv_b, full):
    Bsz, L, _ = h.shape
    xb_dim = SSD_D_INNER + SSD_GN
    p = h @ (w_in if full else w_in[:, :SSD_STATE_COLS])
    xb = jax.nn.silu(dwconv1d(p[..., :xb_dim], conv_w[:, :xb_dim], conv_b[:xb_dim]))
    xs = xb[..., :SSD_D_INNER].reshape(Bsz, L, SSD_HEADS, SSD_HEAD_DIM)
    bm = xb[..., SSD_D_INNER:].reshape(Bsz, L, SSD_GROUPS, SSD_STATE)
    dt_raw = p[..., xb_dim:SSD_STATE_COLS].reshape(Bsz, L, 2, SSD_HEADS)
    if not full:
        return xs, bm, dt_raw, None, None
    cpre = p[..., SSD_STATE_COLS:SSD_STATE_COLS + SSD_GN]
    cm = jax.nn.silu(dwconv1d(cpre, conv_w[:, xb_dim:], conv_b[xb_dim:])).reshape(Bsz, L, SSD_GROUPS, SSD_STATE)
    z = p[..., SSD_STATE_COLS + SSD_GN:]
    return xs, bm, dt_raw, cm, z


def bidir_ssd(xs, bm, cm, dt_raw, dt_bias, a_log, d_skip, h0_fwd, h0_bwd, with_output):
    f32 = jnp.float32
    dt = jax.nn.softplus(dt_raw.astype(f32) + dt_bias.astype(f32))
    a = -jnp.exp(a_log.astype(f32))

    def rev(t):
        return None if t is None else jnp.flip(t, axis=1)

    y_f, h_f = ssd_chunked(xs, dt[:, :, 0], a[0], bm, cm, h0_fwd, with_output)
    y_b, h_b = ssd_chunked(rev(xs), rev(dt[:, :, 1]), a[1], rev(bm), rev(cm), h0_bwd, with_output)
    if not with_output:
        return None, h_f, h_b
    skip = (d_skip[0] + d_skip[1]).astype(f32)[:, None] * xs.astype(f32)
    return y_f + rev(y_b) + skip, h_f, h_b


def ssd_gated_out(y, z, norm_g, w_out):
    Bsz, L = y.shape[0], y.shape[1]
    g = y.reshape(Bsz, L, SSD_GROUPS, SSD_D_INNER // SSD_GROUPS) * jax.nn.silu(
        z.astype(jnp.float32)).reshape(Bsz, L, SSD_GROUPS, SSD_D_INNER // SSD_GROUPS)
    g = g * lax.rsqrt(jnp.mean(g * g, axis=-1, keepdims=True) + EPS)
    g = g.reshape(Bsz, L, SSD_D_INNER) * norm_g.astype(jnp.float32)
    return g.astype(z.dtype) @ w_out


def ssd_mixer(a_lat, a_ctx, w_in, conv_w, conv_b, dt_bias, a_log, d_skip, norm_g, w_out, ctx_out):
    Bsz = a_lat.shape[0]
    zeros = jnp.zeros((Bsz, SSD_HEADS, SSD_HEAD_DIM, SSD_STATE), jnp.float32)
    cx, cb, cdt, cc, cz = ssd_branch_inputs(a_ctx, w_in, conv_w, conv_b, ctx_out)
    y_ctx, h_f, h_b = bidir_ssd(cx, cb, cc, cdt, dt_bias, a_log, d_skip, zeros, zeros, ctx_out)
    lx, lb, ldt, lc, lz = ssd_branch_inputs(a_lat, w_in, conv_w, conv_b, True)
    y_lat, _, _ = bidir_ssd(lx, lb, lc, ldt, dt_bias, a_log, d_skip, h_f, h_b, True)
    out_lat = ssd_gated_out(y_lat, lz, norm_g, w_out)
    out_ctx = ssd_gated_out(y_ctx, cz, norm_g, w_out) if ctx_out else None
    return out_lat, out_ctx


def setup_inputs(seed: int = 0) -> dict:
    key = jax.random.key(seed)
    ks = jax.random.split(key, 24)
    f32 = jnp.float32
    n_a, n_b = N_FOURIER_LAYERS, N_SSD_LAYERS

    def dense(k, shape, fan_in, scale=1.0):
        return jax.random.normal(k, shape, f32) * (scale * fan_in ** -0.5)

    def gain(k, shape):
        return 1.0 + 0.05 * jax.random.normal(k, shape, f32)

    def small(k, shape):
        return 0.02 * jax.random.normal(k, shape, f32)

    dt = jnp.exp(jax.random.uniform(ks[13], (n_b, 2, SSD_HEADS), f32, math.log(DT_MIN), math.log(DT_MAX)))
    dt_bias = dt + jnp.log(-jnp.expm1(-dt))
    a_log = jnp.log(jax.random.uniform(ks[14], (n_b, 2, SSD_HEADS), f32, A_MIN, A_MAX))
    return {
        "x": jax.random.normal(ks[0], (BATCH, SEQ, D_MODEL), f32),
        "c": jax.random.normal(ks[1], (BATCH, D_MODEL), f32),
        "ctx": jax.random.normal(ks[2], (BATCH, CTX_LEN, D_MODEL), f32),
        "c_ctx": jax.random.normal(ks[3], (D_MODEL,), f32),
        "w_mod": dense(ks[4], (DEPTH, D_MODEL, 6 * D_MODEL), D_MODEL, MOD_SCALE),
        "b_mod": small(ks[5], (DEPTH, 6 * D_MODEL)),
        "norm_mix_g": gain(ks[6], (DEPTH, D_MODEL)),
        "norm_ffn_g": gain(ks[7], (DEPTH, D_MODEL)),
        "four_w": dense(ks[8], (n_a, D_MODEL, D_MODEL), D_MODEL),
        "ssd_w_in": dense(ks[9], (n_b, D_MODEL, SSD_IN_DIM), D_MODEL),
        "ssd_conv_w": dense(ks[10], (n_b, SSD_CONV, SSD_CONV_DIM), SSD_CONV),
        "ssd_conv_b": small(ks[11], (n_b, SSD_CONV_DIM)),
        "ssd_dt_bias": dt_bias,
        "ssd_a_log": a_log,
        "ssd_d": 1.0 + 0.1 * jax.random.normal(ks[12], (n_b, 2, SSD_HEADS), f32),
        "ssd_norm_g": gain(ks[15], (n_b, SSD_D_INNER)),
        "ssd_w_out": dense(ks[16], (n_b, SSD_D_INNER, D_MODEL), SSD_D_INNER),
        "ffn_w_up": dense(ks[17], (DEPTH, D_MODEL, 2 * D_FF), D_MODEL),
        "ffn_conv_w": dense(ks[18], (DEPTH, FFN_CONV, FFN_CONV, D_FF), FFN_CONV * FFN_CONV),
        "ffn_conv_b": small(ks[19], (DEPTH, D_FF)),
        "ffn_w_down": dense(ks[20], (DEPTH, D_FF, D_MODEL), D_FF),
        "final_g": gain(ks[21], (D_MODEL,)),
    }


def reference(x, c, ctx, c_ctx, w_mod, b_mod, norm_mix_g, norm_ffn_g, four_w, ssd_w_in, ssd_conv_w,
              ssd_conv_b, ssd_dt_bias, ssd_a_log, ssd_d, ssd_norm_g, ssd_w_out, ffn_w_up, ffn_conv_w,
              ffn_conv_b, ffn_w_down, final_g):
    s_lat = jax.nn.silu(c)
    s_ctx = jax.nn.silu(c_ctx)
    for i in range(DEPTH):
        last = i == DEPTH - 1
        is_ssd = (i % N_MIXERS) == 1
        j = i // N_MIXERS
        sh1, sc1, g1, sh2, sc2, g2 = jnp.split((s_lat @ w_mod[i] + b_mod[i])[:, None, :], 6, axis=-1)
        a_lat = modulate(rmsnorm(x, norm_mix_g[i]), sh1, sc1)
        if (not last) or is_ssd:
            n_cols = 2 * D_MODEL if last else 6 * D_MODEL
            m_ctx = jnp.split(s_ctx @ w_mod[i][:, :n_cols] + b_mod[i][:n_cols], n_cols // D_MODEL)
            a_ctx = modulate(rmsnorm(ctx, norm_mix_g[i]), m_ctx[0], m_ctx[1])
        if is_ssd:
            y_lat, y_ctx = ssd_mixer(a_lat, a_ctx, ssd_w_in[j], ssd_conv_w[j], ssd_conv_b[j], ssd_dt_bias[j],
                                     ssd_a_log[j], ssd_d[j], ssd_norm_g[j], ssd_w_out[j], not last)
        else:
            y_lat = fourier_mix(a_lat, four_w[j])
            y_ctx = None if last else fourier_mix(a_ctx, four_w[j])
        x = x + g1 * y_lat
        b_lat = modulate(rmsnorm(x, norm_ffn_g[i]), sh2, sc2)
        x = x + g2 * conv_ffn(b_lat, ffn_w_up[i], ffn_w_down[i],
                              lambda u: dwconv2d_grid(u, ffn_conv_w[i], ffn_conv_b[i]))
        if not last:
            ctx = ctx + m_ctx[2] * y_ctx
            b_ctx = modulate(rmsnorm(ctx, norm_ffn_g[i]), m_ctx[3], m_ctx[4])
            ctx = ctx + m_ctx[5] * conv_ffn(b_ctx, ffn_w_up[i], ffn_w_down[i],
                                            lambda u: dwconv1d(u, ffn_conv_w[i][FFN_CONV // 2], ffn_conv_b[i]))
    return rmsnorm(x, final_g)
```

```python
import functools
import math

import jax
import jax.numpy as jnp
from jax import lax
from jax.experimental import pallas as pl
from jax.experimental.pallas import tpu as pltpu

F32 = jnp.float32
BF16 = jnp.bfloat16

EPS = 1e-6
GRID_W = 64
FOURIER_GROUPS = 8
SSD_HEAD_DIM = 64
SSD_GROUPS = 8
SSD_STATE = 128
SSD_CHUNK = 128

LANES = 128
SUBLANES = 8
VMEM_LIMIT_BYTES = 56 * 1024 * 1024
ROW_TILE = 2048
EPILOGUE_ROWS = 256

_HIGHEST = lax.Precision.HIGHEST


def _cparams(*sem):
    return pltpu.CompilerParams(dimension_semantics=sem, vmem_limit_bytes=VMEM_LIMIT_BYTES)


def _norm_rows(x, g, shift, scale, modulate):
    ms = jnp.mean(x * x, axis=-1, keepdims=True)
    y = x * lax.rsqrt(ms + EPS) * g
    if modulate:
        y = y * (1.0 + scale) + shift
    return y


def _silu(v):
    return v * jax.nn.sigmoid(v)


def _mod_kernel(s_ref, w_ref, b_ref, o_ref):
    s = _silu(s_ref[...]).astype(BF16)
    w = w_ref[0].astype(BF16)
    o_ref[0] = jnp.dot(s, w, preferred_element_type=F32) + b_ref[0]


def _modulation(cond, w_mod, b_mod):
    depth, d, n = w_mod.shape
    rows = cond.shape[0]
    tn = 1024
    return pl.pallas_call(
        _mod_kernel,
        grid=(depth, n // tn),
        in_specs=[
            pl.BlockSpec((rows, d), lambda i, j: (0, 0)),
            pl.BlockSpec((1, d, tn), lambda i, j: (i, 0, j)),
            pl.BlockSpec((1, 1, tn), lambda i, j: (i, 0, j)),
        ],
        out_specs=pl.BlockSpec((1, rows, tn), lambda i, j: (i, 0, j)),
        out_shape=jax.ShapeDtypeStruct((depth, rows, n), F32),
        compiler_params=_cparams("parallel", "parallel"),
        name="modulation",
    )(cond, w_mod, b_mod.reshape(depth, 1, n))


def _normmod_kernel(x_ref, g_ref, sh_ref, sc_ref, o_ref):
    o_ref[...] = _norm_rows(x_ref[...], g_ref[...], sh_ref[0], sc_ref[0], True).astype(o_ref.dtype)


def _normmod(x, g, shift, scale, rows_per_mod):
    m, d = x.shape
    tm = 512
    mod_spec = pl.BlockSpec((1, 1, d), lambda i: ((i * tm) // rows_per_mod, 0, 0))
    return pl.pallas_call(
        _normmod_kernel,
        grid=(m // tm,),
        in_specs=[
            pl.BlockSpec((tm, d), lambda i: (i, 0)),
            pl.BlockSpec((1, d), lambda i: (0, 0)),
            mod_spec,
            mod_spec,
        ],
        out_specs=pl.BlockSpec((tm, d), lambda i: (i, 0)),
        out_shape=jax.ShapeDtypeStruct((m, d), BF16),
        compiler_params=_cparams("parallel"),
        name="normmod",
    )(x, g.reshape(1, d), shift, scale)


def _mm_kernel(a_ref, w_ref, o_ref):
    o_ref[...] = jnp.dot(a_ref[...], w_ref[...], preferred_element_type=F32).astype(o_ref.dtype)


def _matmul(a, w, out_dtype, tm=1024, tn=1024):
    m, k = a.shape
    n = w.shape[1]
    tm, tn = min(tm, m), min(tn, n)
    return pl.pallas_call(
        _mm_kernel,
        grid=(m // tm, n // tn),
        in_specs=[
            pl.BlockSpec((tm, k), lambda i, j: (i, 0)),
            pl.BlockSpec((k, tn), lambda i, j: (0, j)),
        ],
        out_specs=pl.BlockSpec((tm, tn), lambda i, j: (i, j)),
        out_shape=jax.ShapeDtypeStruct((m, n), out_dtype),
        compiler_params=_cparams("parallel", "parallel"),
        name="matmul",
    )(a, w)


def _mm_res_kernel(a_ref, w_ref, x_ref, gate_ref, g_ref, sh_ref, sc_ref, xo_ref, ao_ref, *, nk, modulate):
    k = pl.program_id(1)
    part = jnp.dot(a_ref[...], w_ref[...], preferred_element_type=F32)

    @pl.when(k == 0)
    def _():
        xo_ref[...] = part

    @pl.when(k > 0)
    def _():
        xo_ref[...] += part

    @pl.when(k == nk - 1)
    def _():
        gate = gate_ref[0]
        g = g_ref[...]
        shift = sh_ref[0]
        scale = sc_ref[0]
        for r in range(xo_ref.shape[0] // EPILOGUE_ROWS):
            rows = pl.ds(r * EPILOGUE_ROWS, EPILOGUE_ROWS)
            xn = x_ref[rows, :] + gate * xo_ref[rows, :]
            xo_ref[rows, :] = xn
            ao_ref[rows, :] = _norm_rows(xn, g, shift, scale, modulate).astype(ao_ref.dtype)


def _matmul_residual(a, w, x, gate, g_next, shift, scale, rows_per_mod, modulate=True, a_dtype=BF16):
    m, k = a.shape
    d = w.shape[1]
    tm, tk = 1024, 512
    nk = k // tk
    mod_spec = pl.BlockSpec((1, 1, d), lambda i, j: ((i * tm) // rows_per_mod, 0, 0))
    return pl.pallas_call(
        functools.partial(_mm_res_kernel, nk=nk, modulate=modulate),
        grid=(m // tm, nk),
        in_specs=[
            pl.BlockSpec((tm, tk), lambda i, j: (i, j)),
            pl.BlockSpec((tk, d), lambda i, j: (j, 0)),
            pl.BlockSpec((tm, d), lambda i, j: (i, 0), pipeline_mode=pl.Buffered(1)),
            mod_spec,
            pl.BlockSpec((1, d), lambda i, j: (0, 0)),
            mod_spec,
            mod_spec,
        ],
        out_specs=[
            pl.BlockSpec((tm, d), lambda i, j: (i, 0)),
            pl.BlockSpec((tm, d), lambda i, j: (i, 0)),
        ],
        out_shape=[
            jax.ShapeDtypeStruct((m, d), F32),
            jax.ShapeDtypeStruct((m, d), a_dtype),
        ],
        compiler_params=_cparams("parallel", "arbitrary"),
        name="matmul_residual",
    )(a, w, x, gate, g_next.reshape(1, d), shift, scale)


def _proj_conv_kernel(*refs, rows, width, vertical, gated):
    refs = list(refs)
    a_ref, wg_ref = refs[:2]
    wv_ref = refs.pop(2) if gated else None
    cw_ref, cb_ref, o_ref, u_scr = refs[2:6]
    scratch = refs[6:]
    ul_scr, ur_scr = scratch[:2] if vertical else (None, None)
    v_scr = scratch[-1] if gated else None
    tn = o_ref.shape[1]
    pad = (u_scr.shape[0] - rows) // 2
    rc = EPILOGUE_ROWS
    a = a_ref[...]
    u_scr[pl.ds(pad, rows), :] = jnp.dot(a, wg_ref[...], preferred_element_type=F32)
    zeros = jnp.zeros((pad, tn), F32)
    u_scr[pl.ds(0, pad), :] = zeros
    u_scr[pl.ds(pad + rows, pad), :] = zeros
    if gated:
        v_scr[...] = jnp.dot(a, wv_ref[...], preferred_element_type=F32)

    def shifted(r):
        col = (lax.broadcasted_iota(jnp.int32, (rc, tn), 0) + r * rc) & (width - 1)
        left = jnp.where(col != 0, u_scr[pl.ds(pad + r * rc - 1, rc), :], 0.0)
        right = jnp.where(col != width - 1, u_scr[pl.ds(pad + r * rc + 1, rc), :], 0.0)
        return left, right

    cw = cw_ref[...]
    bias = cb_ref[...]
    if vertical:
        ul_scr[pl.ds(0, pad), :] = zeros
        ul_scr[pl.ds(pad + rows, pad), :] = zeros
        ur_scr[pl.ds(0, pad), :] = zeros
        ur_scr[pl.ds(pad + rows, pad), :] = zeros
        for r in range(rows // rc):
            left, right = shifted(r)
            ul_scr[pl.ds(pad + r * rc, rc), :] = left
            ur_scr[pl.ds(pad + r * rc, rc), :] = right
    for r in range(rows // rc):
        base = pad + r * rc
        if vertical:
            acc = bias
            for di in range(3):
                off = base + (di - 1) * width
                acc = acc + ul_scr[pl.ds(off, rc), :] * cw[3 * di:3 * di + 1, :]
                acc = acc + u_scr[pl.ds(off, rc), :] * cw[3 * di + 1:3 * di + 2, :]
                acc = acc + ur_scr[pl.ds(off, rc), :] * cw[3 * di + 2:3 * di + 3, :]
        else:
            left, right = shifted(r)
            acc = left * cw[0:1, :] + u_scr[pl.ds(base, rc), :] * cw[1:2, :] + right * cw[2:3, :] + bias
        y = _silu(acc)
        if gated:
            y = y * v_scr[pl.ds(r * rc, rc), :]
        o_ref[pl.ds(r * rc, rc), :] = y.astype(o_ref.dtype)


def _proj_conv(a, w, conv_w, conv_b, *, width, vertical, n_out, val_offset=None, tn):
    m, k = a.shape
    rows = ROW_TILE
    gated = val_offset is not None
    assert width & (width - 1) == 0 and rows % width == 0 and m % rows == 0
    pad = (width + SUBLANES) if vertical else SUBLANES
    taps = conv_w.shape[0]
    in_specs = [
        pl.BlockSpec((rows, k), lambda i, j: (i, 0)),
        pl.BlockSpec((k, tn), lambda i, j: (0, j)),
    ]
    operands = [a, w]
    scratch = [pltpu.VMEM((rows + 2 * pad, tn), F32)]
    if gated:
        voff = val_offset // tn
        in_specs.append(pl.BlockSpec((k, tn), lambda i, j: (0, j + voff)))
        operands.append(w)
    in_specs += [
        pl.BlockSpec((taps, tn), lambda i, j: (0, j)),
        pl.BlockSpec((1, tn), lambda i, j: (0, j)),
    ]
    operands += [conv_w, conv_b.reshape(1, n_out)]
    if vertical:
        scratch += [pltpu.VMEM((rows + 2 * pad, tn), F32), pltpu.VMEM((rows + 2 * pad, tn), F32)]
    if gated:
        scratch.append(pltpu.VMEM((rows, tn), F32))
    return pl.pallas_call(
        functools.partial(_proj_conv_kernel, rows=rows, width=width, vertical=vertical, gated=gated),
        grid=(m // rows, n_out // tn),
        in_specs=in_specs,
        out_specs=pl.BlockSpec((rows, tn), lambda i, j: (i, j)),
        out_shape=jax.ShapeDtypeStruct((m, n_out), BF16),
        scratch_shapes=scratch,
        compiler_params=_cparams("parallel", "parallel"),
        name="proj_conv",
    )(*operands)


def _dft_tables(n):
    k = jnp.arange(n, dtype=jnp.int32)
    idx = (k[:, None] * k[None, :]) % n
    ang = idx.astype(F32) * F32(2.0 * math.pi / n)
    s = F32(1.0 / math.sqrt(n))
    return jnp.cos(ang) * s, jnp.sin(ang) * s


def _chan_dft_kernel(a_ref, wc_ref, o_ref, *, groups):
    dg = wc_ref.shape[0]
    wc = wc_ref[...]
    for g in range(groups):
        cols = pl.ds(g * dg, dg)
        res = jnp.dot(a_ref[:, cols], wc, preferred_element_type=F32)
        o_ref[0, :, cols] = res[:, :dg].astype(o_ref.dtype)
        o_ref[1, :, cols] = res[:, dg:].astype(o_ref.dtype)


def _chan_dft(a, wc, nb, ln):
    d = a.shape[1]
    dg = wc.shape[0]
    tm = min(512, ln)
    mt = ln // tm
    return pl.pallas_call(
        functools.partial(_chan_dft_kernel, groups=d // dg),
        grid=(nb, mt),
        in_specs=[
            pl.BlockSpec((tm, d), lambda b, i: (b * mt + i, 0)),
            pl.BlockSpec((dg, 2 * dg), lambda b, i: (0, 0)),
        ],
        out_specs=pl.BlockSpec((None, 2, tm, d), lambda b, i: (b, 0, i, 0)),
        out_shape=jax.ShapeDtypeStruct((nb, 2, ln, d), BF16),
        compiler_params=_cparams("parallel", "parallel"),
        name="chan_dft",
    )(a, wc)


def _pos_dft(cs, y):
    nb, k2, d = y.shape
    ln = cs.shape[0]
    tm, tn = min(1024, ln), 1024
    return pl.pallas_call(
        _mm_kernel,
        grid=(nb, d // tn, ln // tm),
        in_specs=[
            pl.BlockSpec((tm, k2), lambda b, j, i: (i, 0)),
            pl.BlockSpec((None, k2, tn), lambda b, j, i: (b, 0, j)),
        ],
        out_specs=pl.BlockSpec((None, tm, tn), lambda b, j, i: (b, i, j)),
        out_shape=jax.ShapeDtypeStruct((nb, ln, d), BF16),
        compiler_params=_cparams("parallel", "parallel", "parallel"),
        name="pos_dft",
    )(cs, y)


def _fourier(a, nb, ln, wc, cs):
    d = a.shape[1]
    y = _chan_dft(a, wc, nb, ln)
    f = _pos_dft(cs, y.reshape(nb, 2 * ln, d))
    return f.reshape(nb * ln, d)


def _ssd_dt_kernel(a_ref, w_ref, wt_ref, bias_ref, biast_ref, alog_ref, alogt_ref,
                   dt_ref, acs_ref, acst_ref, *, heads):
    ln = a_ref.shape[0]
    q = SSD_CHUNK
    a = a_ref[...]
    dt = jax.nn.softplus(jnp.dot(a, w_ref[...], preferred_element_type=F32) + bias_ref[...])
    dt_ref[...] = dt
    dta = dt * (-jnp.exp(alog_ref[...]))
    dtt = jax.nn.softplus(
        lax.dot_general(wt_ref[...], a, (((1,), (1,)), ((), ())), preferred_element_type=F32)
        + biast_ref[...])
    dtat = dtt * (-jnp.exp(alogt_ref[...]))
    ri = lax.broadcasted_iota(jnp.int32, (q, q), 0)
    ci = lax.broadcasted_iota(jnp.int32, (q, q), 1)
    lower = (ri >= ci).astype(F32)
    upper = (ri <= ci).astype(F32)
    fwd_col = lax.broadcasted_iota(jnp.int32, (q, 2 * heads), 1) < heads
    fwd_row = lax.broadcasted_iota(jnp.int32, (2 * heads, q), 0) < heads
    for c in range(ln // q):
        rows = pl.ds(c * q, q)
        xc = dta[c * q:(c + 1) * q, :]
        pre = jnp.dot(lower, xc, precision=_HIGHEST, preferred_element_type=F32)
        suf = jnp.dot(upper, xc, precision=_HIGHEST, preferred_element_type=F32)
        acs_ref[rows, :] = jnp.where(fwd_col, pre, suf)
        xt = dtat[:, c * q:(c + 1) * q]
        pre_t = jnp.dot(xt, upper, precision=_HIGHEST, preferred_element_type=F32)
        suf_t = jnp.dot(xt, lower, precision=_HIGHEST, preferred_element_type=F32)
        acst_ref[c] = jnp.where(fwd_row, pre_t, suf_t)


def _ssd_dt(a, w_dt, dt_bias, a_log, nb, ln):
    d = a.shape[1]
    h2 = w_dt.shape[1]
    nc = ln // SSD_CHUNK
    bias = dt_bias.reshape(1, h2)
    alog = a_log.reshape(1, h2)
    vec = pl.BlockSpec((1, h2), lambda b: (0, 0))
    vect = pl.BlockSpec((h2, 1), lambda b: (0, 0))
    return pl.pallas_call(
        functools.partial(_ssd_dt_kernel, heads=h2 // 2),
        grid=(nb,),
        in_specs=[
            pl.BlockSpec((ln, d), lambda b: (b, 0)),
            pl.BlockSpec((d, h2), lambda b: (0, 0)),
            pl.BlockSpec((h2, d), lambda b: (0, 0)),
            vec, vect, vec, vect,
        ],
        out_specs=[
            pl.BlockSpec((None, ln, h2), lambda b: (b, 0, 0)),
            pl.BlockSpec((None, ln, h2), lambda b: (b, 0, 0)),
            pl.BlockSpec((None, nc, h2, SSD_CHUNK), lambda b: (b, 0, 0, 0)),
        ],
        out_shape=[
            jax.ShapeDtypeStruct((nb, ln, h2), F32),
            jax.ShapeDtypeStruct((nb, ln, h2), F32),
            jax.ShapeDtypeStruct((nb, nc, h2, SSD_CHUNK), F32),
        ],
        compiler_params=_cparams("parallel"),
        name="ssd_dt",
    )(a, w_dt, w_dt.T, bias, bias.reshape(h2, 1), alog, alog.reshape(h2, 1))


def _ssd_scan_kernel(xs_ref, b_ref, c_ref, z_ref, dt_ref, acs_ref, acst_ref, h0_ref, d0_ref, d1_ref,
                     ng_ref, o_ref, hfin_ref, yf_scr, h_scr, *, nc, heads_per_group):
    q = SSD_CHUNK
    r_heads = heads_per_group
    p = SSD_HEAD_DIM
    gw = r_heads * p
    expand = (lax.broadcasted_iota(jnp.int32, (2 * r_heads, 2 * gw), 1) // p
              == lax.broadcasted_iota(jnp.int32, (2 * r_heads, 2 * gw), 0)).astype(F32)
    li = lax.broadcasted_iota(jnp.int32, (q, q), 0)
    si = lax.broadcasted_iota(jnp.int32, (q, q), 1)
    lane = lax.broadcasted_iota(jnp.int32, (q, gw), 1)
    even_head = (lane % (2 * p)) < p
    dskip = d0_ref[...] + d1_ref[...]
    norm_g = ng_ref[...]

    def chunk(c, direction):
        rows = pl.ds(pl.multiple_of(c * q, q), q)
        xs = xs_ref[rows, :].astype(F32)
        bc = b_ref[rows, :]
        cc = c_ref[rows, :]
        acs_cols = acs_ref[rows, :]
        acs_rows = acst_ref[c]
        acs_e = jnp.dot(acs_cols, expand, precision=_HIGHEST, preferred_element_type=F32)[:, direction * gw:(direction + 1) * gw]
        dt_e = jnp.dot(dt_ref[rows, :], expand, precision=_HIGHEST, preferred_element_type=F32)[:, direction * gw:(direction + 1) * gw]
        xdt = xs * dt_e
        last = acs_e[q - 1:q, :] if direction == 0 else acs_e[0:1, :]
        causal = (li >= si) if direction == 0 else (li <= si)
        scores = lax.dot_general(cc, bc, (((1,), (1,)), ((), ())), preferred_element_type=F32)
        h = h_scr[...]
        y = jnp.dot(cc, h.astype(BF16), preferred_element_type=F32) * jnp.exp(acs_e)
        xdt_even = jnp.where(even_head, xdt, 0.0).astype(BF16)
        xdt_odd = jnp.where(even_head, 0.0, xdt).astype(BF16)
        pieces = []
        for pair in range(r_heads // 2):
            acc = None
            for sub, rhs in ((0, xdt_even), (1, xdt_odd)):
                col = direction * r_heads + 2 * pair + sub
                seg = acs_cols[:, col:col + 1] - acs_rows[col:col + 1, :]
                decay = jnp.exp(jnp.where(causal, seg, -jnp.inf))
                m = (scores * decay).astype(BF16)
                term = jnp.dot(m, rhs[:, pair * 2 * p:(pair + 1) * 2 * p], preferred_element_type=F32)
                acc = term if acc is None else acc + term
            pieces.append(acc)
        y = y + jnp.concatenate(pieces, axis=1)
        w = (xdt * jnp.exp(last - acs_e)).astype(BF16)
        h_scr[...] = h * jnp.exp(last) + lax.dot_general(
            bc, w, (((0,), (0,)), ((), ())), preferred_element_type=F32)
        if direction == 0:
            yf_scr[rows, :] = y
        else:
            tot = yf_scr[rows, :] + y + dskip * xs
            gz = tot * _silu(z_ref[rows, :].astype(F32))
            gz = gz * lax.rsqrt(jnp.mean(gz * gz, axis=-1, keepdims=True) + EPS)
            o_ref[rows, :] = (gz * norm_g).astype(o_ref.dtype)

    h_scr[...] = h0_ref[0]

    def fwd_body(c, carry):
        chunk(c, 0)
        return carry

    lax.fori_loop(0, nc, fwd_body, 0)
    hfin_ref[0] = h_scr[...]
    h_scr[...] = h0_ref[1]

    def bwd_body(i, carry):
        chunk(nc - 1 - i, 1)
        return carry

    lax.fori_loop(0, nc, bwd_body, 0)
    hfin_ref[1] = h_scr[...]


def _ssd_scan(xbc, z, dt4, acs4, acst4, h0, d0e, d1e, norm_g, nb, ln):
    d_inner = z.shape[1]
    groups = SSD_GROUPS
    gw = d_inner // groups
    n = SSD_STATE
    r2 = dt4.shape[-1]
    nc = ln // SSD_CHUNK
    b_blk = d_inner // n
    c_blk = b_blk + groups
    seq_g = pl.BlockSpec((None, None, ln, r2), lambda b, g: (b, g, 0, 0))
    state = pl.BlockSpec((None, None, 2, n, gw), lambda b, g: (b, g, 0, 0, 0))
    vec = pl.BlockSpec((1, gw), lambda b, g: (0, g))
    return pl.pallas_call(
        functools.partial(_ssd_scan_kernel, nc=nc, heads_per_group=r2 // 2),
        grid=(nb, groups),
        in_specs=[
            pl.BlockSpec((ln, gw), lambda b, g: (b, g)),
            pl.BlockSpec((ln, n), lambda b, g: (b, b_blk + g)),
            pl.BlockSpec((ln, n), lambda b, g: (b, c_blk + g)),
            pl.BlockSpec((ln, gw), lambda b, g: (b, g)),
            seq_g,
            seq_g,
            pl.BlockSpec((None, None, nc, r2, SSD_CHUNK), lambda b, g: (b, g, 0, 0, 0)),
            state,
            vec, vec, vec,
        ],
        out_specs=[
            pl.BlockSpec((ln, gw), lambda b, g: (b, g)),
            state,
        ],
        out_shape=[
            jax.ShapeDtypeStruct((nb * ln, d_inner), BF16),
            jax.ShapeDtypeStruct(h0.shape, F32),
        ],
        scratch_shapes=[pltpu.VMEM((ln, gw), F32), pltpu.VMEM((n, gw), F32)],
        compiler_params=_cparams("parallel", "parallel"),
        name="ssd_scan",
    )(xbc, xbc, xbc, z, dt4, acs4, acst4, h0, d0e, d1e, norm_g)


def _group_cols(t, groups):
    nb, ln, _, h = t.shape
    r = h // groups
    return t.reshape(nb, ln, 2, groups, r).transpose(0, 3, 1, 2, 4).reshape(nb, groups, ln, 2 * r)


def _ssd_mixer(a, nb, ln, weights, h0):
    w_xbc, w_z, w_dt, conv_w, conv_b, dt_bias, a_log, d0e, d1e, norm_g = weights
    groups = SSD_GROUPS
    heads = dt_bias.shape[1]
    xbc = _proj_conv(a, w_xbc, conv_w, conv_b, width=ln, vertical=False, n_out=w_xbc.shape[1], tn=512)
    z = _matmul(a, w_z, BF16)
    dt, acs, acst = _ssd_dt(a, w_dt, dt_bias, a_log, nb, ln)
    nc = ln // SSD_CHUNK
    dt4 = _group_cols(dt.reshape(nb, ln, 2, heads), groups)
    acs4 = _group_cols(acs.reshape(nb, ln, 2, heads), groups)
    r = heads // groups
    acst4 = acst.reshape(nb, nc, 2, groups, r, SSD_CHUNK).transpose(0, 3, 1, 2, 4, 5).reshape(
        nb, groups, nc, 2 * r, SSD_CHUNK)
    return _ssd_scan(xbc, z, dt4, acs4, acst4, h0, d0e, d1e, norm_g, nb, ln)


def kernel(x, c, ctx, c_ctx, w_mod, b_mod, norm_mix_g, norm_ffn_g, four_w, ssd_w_in, ssd_conv_w, ssd_conv_b,
           ssd_dt_bias, ssd_a_log, ssd_d, ssd_norm_g, ssd_w_out, ffn_w_up, ffn_conv_w, ffn_conv_b, ffn_w_down,
           final_g):
    nb, ln, d = x.shape
    lc = ctx.shape[1]
    depth = w_mod.shape[0]
    d_ff = ffn_w_down.shape[1]
    d_inner = ssd_w_out.shape[1]
    heads = ssd_dt_bias.shape[2]
    gn = SSD_GROUPS * SSD_STATE
    t_lat, t_ctx = nb * ln, nb * lc

    mod_rows = -(-(nb + 1) // SUBLANES) * SUBLANES
    cond = jnp.zeros((mod_rows, d), F32).at[:nb].set(c).at[nb].set(c_ctx)
    mods = _modulation(cond, w_mod, b_mod)

    def lat_mod(i, j):
        return mods[i, :nb, j * d:(j + 1) * d].reshape(nb, 1, d)

    def ctx_mod(i, j):
        return mods[i, nb:nb + 1, j * d:(j + 1) * d].reshape(1, 1, d)

    four_bf = four_w.astype(BF16)
    w_up_bf = ffn_w_up.astype(BF16)
    w_down_bf = ffn_w_down.astype(BF16)
    w_in_bf = ssd_w_in.astype(BF16)
    w_out_bf = ssd_w_out.astype(BF16)
    conv9 = ffn_conv_w.reshape(depth, 9, d_ff)

    dg = d // FOURIER_GROUPS
    cos_c, sin_c = _dft_tables(dg)
    wc = jnp.concatenate([cos_c, sin_c], axis=1).astype(BF16)
    cos_l, sin_l = _dft_tables(ln)
    cs_lat = jnp.concatenate([cos_l, -sin_l], axis=1).astype(BF16)
    cos_x, sin_x = _dft_tables(lc)
    cs_ctx = jnp.concatenate([cos_x, -sin_x], axis=1).astype(BF16)

    xl = x.reshape(t_lat, d)
    xc = ctx.reshape(t_ctx, d)
    a_lat = _normmod(xl, norm_mix_g[0], lat_mod(0, 0), lat_mod(0, 1), ln)
    a_ctx = _normmod(xc, norm_mix_g[0], ctx_mod(0, 0), ctx_mod(0, 1), t_ctx)
    out = None
    for i in range(depth):
        last = i == depth - 1
        is_ssd = i % 2 == 1
        j = i // 2
        if is_ssd:
            w_in = w_in_bf[j]
            xb = d_inner + gn
            state_cols = xb + 2 * heads
            w_xbc = jnp.concatenate([w_in[:, :xb], w_in[:, state_cols:state_cols + gn]], axis=1)
            w_dt = w_in[:, xb:state_cols]
            w_z = w_in[:, state_cols + gn:]
            d0e = jnp.repeat(ssd_d[j, 0], SSD_HEAD_DIM).reshape(1, d_inner)
            d1e = jnp.repeat(ssd_d[j, 1], SSD_HEAD_DIM).reshape(1, d_inner)
            weights = (w_xbc, w_z, w_dt, ssd_conv_w[j], ssd_conv_b[j], ssd_dt_bias[j], ssd_a_log[j], d0e, d1e,
                       ssd_norm_g[j].reshape(1, d_inner))
            zeros = jnp.zeros((nb, SSD_GROUPS, 2, SSD_STATE, d_inner // SSD_GROUPS), F32)
            mix_ctx, h_ctx = _ssd_mixer(a_ctx, nb, lc, weights, zeros)
            mix_lat, _ = _ssd_mixer(a_lat, nb, ln, weights, h_ctx)
            w_mix = w_out_bf[j]
        else:
            mix_lat = _fourier(a_lat, nb, ln, wc, cs_lat)
            mix_ctx = None if last else _fourier(a_ctx, nb, lc, wc, cs_ctx)
            w_mix = four_bf[j]
        xl, b_lat = _matmul_residual(mix_lat, w_mix, xl, lat_mod(i, 2), norm_ffn_g[i], lat_mod(i, 3),
                                     lat_mod(i, 4), ln)
        act = _proj_conv(b_lat, w_up_bf[i], conv9[i], ffn_conv_b[i], width=GRID_W, vertical=True,
                         n_out=d_ff, val_offset=d_ff, tn=256)
        if last:
            zero_mod = jnp.zeros((nb, 1, d), F32)
            _, out = _matmul_residual(act, w_down_bf[i], xl, lat_mod(i, 5), final_g, zero_mod, zero_mod, ln,
                                      modulate=False, a_dtype=F32)
        else:
            xl, a_lat = _matmul_residual(act, w_down_bf[i], xl, lat_mod(i, 5), norm_mix_g[i + 1],
                                         lat_mod(i + 1, 0), lat_mod(i + 1, 1), ln)
            xc, b_ctx = _matmul_residual(mix_ctx, w_mix, xc, ctx_mod(i, 2), norm_ffn_g[i], ctx_mod(i, 3),
                                         ctx_mod(i, 4), t_ctx)
            act_c = _proj_conv(b_ctx, w_up_bf[i], ffn_conv_w[i, 1], ffn_conv_b[i], width=lc, vertical=False,
                               n_out=d_ff, val_offset=d_ff, tn=256)
            xc, a_ctx = _matmul_residual(act_c, w_down_bf[i], xc, ctx_mod(i, 5), norm_mix_g[i + 1],
                                         ctx_mod(i + 1, 0), ctx_mod(i + 1, 1), t_ctx)
    return out.reshape(nb, ln, d)
```

```python
import functools
import math

import jax
import jax.numpy as jnp
from jax import lax
from jax.experimental import pallas as pl
from jax.experimental.pallas import tpu as pltpu

F32 = jnp.float32
BF16 = jnp.bfloat16

EPS = 1e-6
GRID_W = 64
FOURIER_GROUPS = 8
SSD_HEAD_DIM = 64
SSD_GROUPS = 8
SSD_STATE = 128
SSD_CHUNK = 128

LANES = 128
SUBLANES = 8
VMEM_LIMIT_BYTES = 56 * 1024 * 1024
ROW_TILE = 2048
EPILOGUE_ROWS = 256

_HIGHEST = lax.Precision.HIGHEST


def _cparams(*sem, flags=None):
    return pltpu.CompilerParams(dimension_semantics=sem, vmem_limit_bytes=VMEM_LIMIT_BYTES, flags=flags)


def _norm_rows(x, g, shift, scale, modulate):
    ms = jnp.mean(x * x, axis=-1, keepdims=True)
    y = x * lax.rsqrt(ms + EPS) * g
    if modulate:
        y = y * (1.0 + scale) + shift
    return y


def _silu(v):
    return v * jax.nn.sigmoid(v)


def _mod_kernel(s_ref, w_ref, b_ref, o_ref):
    s = _silu(s_ref[...]).astype(BF16)
    w = w_ref[0].astype(BF16)
    o_ref[0] = jnp.dot(s, w, preferred_element_type=F32) + b_ref[0]


def _modulation(cond, w_mod, b_mod):
    depth, d, n = w_mod.shape
    rows = cond.shape[0]
    tn = 1024
    return pl.pallas_call(
        _mod_kernel,
        grid=(depth, n // tn),
        in_specs=[
            pl.BlockSpec((rows, d), lambda i, j: (0, 0)),
            pl.BlockSpec((1, d, tn), lambda i, j: (i, 0, j)),
            pl.BlockSpec((1, 1, tn), lambda i, j: (i, 0, j)),
        ],
        out_specs=pl.BlockSpec((1, rows, tn), lambda i, j: (i, 0, j)),
        out_shape=jax.ShapeDtypeStruct((depth, rows, n), F32),
        compiler_params=_cparams("parallel", "parallel"),
        name="modulation",
    )(cond, w_mod, b_mod.reshape(depth, 1, n))


def _normmod_kernel(x_ref, g_ref, sh_ref, sc_ref, o_ref):
    o_ref[...] = _norm_rows(x_ref[...], g_ref[...], sh_ref[0], sc_ref[0], True).astype(o_ref.dtype)


def _normmod(x, g, shift, scale, rows_per_mod):
    m, d = x.shape
    tm = 512
    mod_spec = pl.BlockSpec((1, 1, d), lambda i: ((i * tm) // rows_per_mod, 0, 0))
    return pl.pallas_call(
        _normmod_kernel,
        grid=(m // tm,),
        in_specs=[
            pl.BlockSpec((tm, d), lambda i: (i, 0)),
            pl.BlockSpec((1, d), lambda i: (0, 0)),
            mod_spec,
            mod_spec,
        ],
        out_specs=pl.BlockSpec((tm, d), lambda i: (i, 0)),
        out_shape=jax.ShapeDtypeStruct((m, d), BF16),
        compiler_params=_cparams("parallel"),
        name="normmod",
    )(x, g.reshape(1, d), shift, scale)


def _mm_kernel(a_ref, w_ref, o_ref):
    o_ref[...] = jnp.dot(a_ref[...], w_ref[...], preferred_element_type=F32).astype(o_ref.dtype)


def _matmul(a, w, out_dtype, tm=1024, tn=1024):
    m, k = a.shape
    n = w.shape[1]
    tm, tn = min(tm, m), min(tn, n)
    return pl.pallas_call(
        _mm_kernel,
        grid=(m // tm, n // tn),
        in_specs=[
            pl.BlockSpec((tm, k), lambda i, j: (i, 0)),
            pl.BlockSpec((k, tn), lambda i, j: (0, j)),
        ],
        out_specs=pl.BlockSpec((tm, tn), lambda i, j: (i, j)),
        out_shape=jax.ShapeDtypeStruct((m, n), out_dtype),
        compiler_params=_cparams("parallel", "parallel"),
        name="matmul",
    )(a, w)


def _mm_res_kernel(a_ref, w_ref, x_ref, gate_ref, g_ref, sh_ref, sc_ref, xo_ref, ao_ref, xn_scr, ssq_scr,
                   *, nn, modulate):
    j = pl.program_id(1)
    tn = w_ref.shape[1]
    xn = x_ref[...] + gate_ref[0] * jnp.dot(a_ref[...], w_ref[...], preferred_element_type=F32)
    xo_ref[...] = xn
    xn_scr[j] = xn
    ssq = jnp.sum(xn * xn, axis=-1, keepdims=True)

    @pl.when(j == 0)
    def _():
        ssq_scr[...] = ssq

    @pl.when(j > 0)
    def _():
        ssq_scr[...] += ssq

    @pl.when(j == nn - 1)
    def _():
        inv = lax.rsqrt(ssq_scr[...] * (1.0 / (nn * tn)) + EPS)
        for c in range(nn):
            cols = slice(c * tn, (c + 1) * tn)
            y = xn_scr[c] * inv * g_ref[:, cols]
            if modulate:
                y = y * (1.0 + sc_ref[0][:, cols]) + sh_ref[0][:, cols]
            ao_ref[:, cols] = y.astype(ao_ref.dtype)


def _matmul_residual(a, w, x, gate, g_next, shift, scale, rows_per_mod, modulate=True, a_dtype=BF16):
    m, k = a.shape
    d = w.shape[1]
    tm = min(512, m)
    tn = 1024 if k <= 4096 else 512
    nn = d // tn

    def mod_row(i, j):
        return ((i * tm) // rows_per_mod, 0, 0)

    mod_spec = pl.BlockSpec((1, 1, d), mod_row)
    return pl.pallas_call(
        functools.partial(_mm_res_kernel, nn=nn, modulate=modulate),
        grid=(m // tm, nn),
        in_specs=[
            pl.BlockSpec((tm, k), lambda i, j: (i, 0)),
            pl.BlockSpec((k, tn), lambda i, j: (0, j)),
            pl.BlockSpec((tm, tn), lambda i, j: (i, j)),
            pl.BlockSpec((1, 1, tn), lambda i, j: ((i * tm) // rows_per_mod, 0, j)),
            pl.BlockSpec((1, d), lambda i, j: (0, 0)),
            mod_spec,
            mod_spec,
        ],
        out_specs=[
            pl.BlockSpec((tm, tn), lambda i, j: (i, j)),
            pl.BlockSpec((tm, d), lambda i, j: (i, 0)),
        ],
        out_shape=[
            jax.ShapeDtypeStruct((m, d), F32),
            jax.ShapeDtypeStruct((m, d), a_dtype),
        ],
        scratch_shapes=[pltpu.VMEM((nn, tm, tn), F32), pltpu.VMEM((tm, 1), F32)],
        compiler_params=_cparams("parallel", "arbitrary"),
        name="matmul_residual",
    )(a, w, x, gate, g_next.reshape(1, d), shift, scale)


def _proj_conv_kernel(*refs, rows, width, vertical, gated):
    refs = list(refs)
    a_ref, wg_ref = refs[:2]
    wv_ref = refs.pop(2) if gated else None
    cw_ref, cb_ref, o_ref, u_scr = refs[2:6]
    scratch = refs[6:]
    ul_scr, ur_scr = scratch[:2] if vertical else (None, None)
    v_scr = scratch[-1] if gated else None
    tn = o_ref.shape[1]
    pad = (u_scr.shape[0] - rows) // 2
    rc = min(EPILOGUE_ROWS, rows)
    a = a_ref[...]
    u_scr[pl.ds(pad, rows), :] = jnp.dot(a, wg_ref[...], preferred_element_type=F32)
    zeros = jnp.zeros((pad, tn), F32)
    u_scr[pl.ds(0, pad), :] = zeros
    u_scr[pl.ds(pad + rows, pad), :] = zeros
    if gated:
        v_scr[...] = jnp.dot(a, wv_ref[...], preferred_element_type=F32)

    def shifted(r):
        col = (lax.broadcasted_iota(jnp.int32, (rc, tn), 0) + r * rc) & (width - 1)
        left = jnp.where(col != 0, u_scr[pl.ds(pad + r * rc - 1, rc), :], 0.0)
        right = jnp.where(col != width - 1, u_scr[pl.ds(pad + r * rc + 1, rc), :], 0.0)
        return left, right

    cw = cw_ref[...]
    bias = cb_ref[...]
    if vertical:
        ul_scr[pl.ds(0, pad), :] = zeros
        ul_scr[pl.ds(pad + rows, pad), :] = zeros
        ur_scr[pl.ds(0, pad), :] = zeros
        ur_scr[pl.ds(pad + rows, pad), :] = zeros
        for r in range(rows // rc):
            left, right = shifted(r)
            ul_scr[pl.ds(pad + r * rc, rc), :] = left
            ur_scr[pl.ds(pad + r * rc, rc), :] = right
    for r in range(rows // rc):
        base = pad + r * rc
        if vertical:
            acc = bias
            for di in range(3):
                off = base + (di - 1) * width
                acc = acc + ul_scr[pl.ds(off, rc), :] * cw[3 * di:3 * di + 1, :]
                acc = acc + u_scr[pl.ds(off, rc), :] * cw[3 * di + 1:3 * di + 2, :]
                acc = acc + ur_scr[pl.ds(off, rc), :] * cw[3 * di + 2:3 * di + 3, :]
        else:
            left, right = shifted(r)
            acc = left * cw[0:1, :] + u_scr[pl.ds(base, rc), :] * cw[1:2, :] + right * cw[2:3, :] + bias
        y = _silu(acc)
        if gated:
            y = y * v_scr[pl.ds(r * rc, rc), :]
        o_ref[pl.ds(r * rc, rc), :] = y.astype(o_ref.dtype)


def _proj_conv(a, w, conv_w, conv_b, *, width, vertical, n_out, val_offset=None, tn):
    m, k = a.shape
    rows = min(ROW_TILE, m)
    gated = val_offset is not None
    assert width & (width - 1) == 0 and rows % width == 0 and m % rows == 0
    pad = (width + SUBLANES) if vertical else SUBLANES
    taps = conv_w.shape[0]
    in_specs = [
        pl.BlockSpec((rows, k), lambda i, j: (i, 0)),
        pl.BlockSpec((k, tn), lambda i, j: (0, j)),
    ]
    operands = [a, w]
    padded = pltpu.VMEM((rows + 2 * pad, tn), F32)
    scratch = [padded] + ([padded, padded] if vertical else []) + ([pltpu.VMEM((rows, tn), F32)] if gated else [])
    if gated:
        voff = val_offset // tn
        in_specs.append(pl.BlockSpec((k, tn), lambda i, j: (0, j + voff)))
        operands.append(w)
    in_specs += [
        pl.BlockSpec((taps, tn), lambda i, j: (0, j)),
        pl.BlockSpec((1, tn), lambda i, j: (0, j)),
    ]
    operands += [conv_w, conv_b.reshape(1, n_out)]
    return pl.pallas_call(
        functools.partial(_proj_conv_kernel, rows=rows, width=width, vertical=vertical, gated=gated),
        grid=(m // rows, n_out // tn),
        in_specs=in_specs,
        out_specs=pl.BlockSpec((rows, tn), lambda i, j: (i, j)),
        out_shape=jax.ShapeDtypeStruct((m, n_out), BF16),
        scratch_shapes=scratch,
        compiler_params=_cparams("parallel", "parallel"),
        name="proj_conv",
    )(*operands)


def _dft_tables(n):
    k = jnp.arange(n, dtype=jnp.int32)
    idx = (k[:, None] * k[None, :]) % n
    ang = idx.astype(F32) * F32(2.0 * math.pi / n)
    s = F32(1.0 / math.sqrt(n))
    return jnp.cos(ang) * s, jnp.sin(ang) * s


def _chan_dft_kernel(a_ref, wc_ref, o_ref, *, groups):
    dg = wc_ref.shape[0]
    wc = wc_ref[...]
    for g in range(groups):
        cols = pl.ds(g * dg, dg)
        res = jnp.dot(a_ref[:, cols], wc, preferred_element_type=F32)
        o_ref[0, :, cols] = res[:, :dg].astype(o_ref.dtype)
        o_ref[1, :, cols] = res[:, dg:].astype(o_ref.dtype)


def _chan_dft(a, wc, nb, ln):
    d = a.shape[1]
    dg = wc.shape[0]
    tm = min(512, ln)
    mt = ln // tm
    return pl.pallas_call(
        functools.partial(_chan_dft_kernel, groups=d // dg),
        grid=(nb, mt),
        in_specs=[
            pl.BlockSpec((tm, d), lambda b, i: (b * mt + i, 0)),
            pl.BlockSpec((dg, 2 * dg), lambda b, i: (0, 0)),
        ],
        out_specs=pl.BlockSpec((None, 2, tm, d), lambda b, i: (b, 0, i, 0)),
        out_shape=jax.ShapeDtypeStruct((nb, 2, ln, d), BF16),
        compiler_params=_cparams("parallel", "parallel"),
        name="chan_dft",
    )(a, wc)


def _pos_dft(cs, y):
    nb, k2, d = y.shape
    ln = cs.shape[0]
    tm, tn = min(1024, ln), 1024
    return pl.pallas_call(
        _mm_kernel,
        grid=(nb, d // tn, ln // tm),
        in_specs=[
            pl.BlockSpec((tm, k2), lambda b, j, i: (i, 0)),
            pl.BlockSpec((None, k2, tn), lambda b, j, i: (b, 0, j)),
        ],
        out_specs=pl.BlockSpec((None, tm, tn), lambda b, j, i: (b, i, j)),
        out_shape=jax.ShapeDtypeStruct((nb, ln, d), BF16),
        compiler_params=_cparams("parallel", "parallel", "parallel"),
        name="pos_dft",
    )(cs, y)


def _fourier(a, nb, ln, wc, cs):
    d = a.shape[1]
    y = _chan_dft(a, wc, nb, ln)
    f = _pos_dft(cs, y.reshape(nb, 2 * ln, d))
    return f.reshape(nb * ln, d)


def _ssd_dt_kernel(a_ref, w_ref, wt_ref, bias_ref, biast_ref, alog_ref, alogt_ref,
                   dt_ref, acs_ref, dtt_ref, acst_ref, *, heads):
    ln = a_ref.shape[0]
    q = SSD_CHUNK
    a = a_ref[...]
    dt = jax.nn.softplus(jnp.dot(a, w_ref[...], preferred_element_type=F32) + bias_ref[...])
    dt_ref[...] = dt
    dta = dt * (-jnp.exp(alog_ref[...]))
    dtt = jax.nn.softplus(
        lax.dot_general(wt_ref[...], a, (((1,), (1,)), ((), ())), preferred_element_type=F32)
        + biast_ref[...])
    dtat = dtt * (-jnp.exp(alogt_ref[...]))
    ri = lax.broadcasted_iota(jnp.int32, (q, q), 0)
    ci = lax.broadcasted_iota(jnp.int32, (q, q), 1)
    lower = (ri >= ci).astype(F32)
    upper = (ri <= ci).astype(F32)
    fwd_col = lax.broadcasted_iota(jnp.int32, (q, 2 * heads), 1) < heads
    fwd_row = lax.broadcasted_iota(jnp.int32, (2 * heads, q), 0) < heads
    for c in range(ln // q):
        rows = pl.ds(c * q, q)
        xc = dta[c * q:(c + 1) * q, :]
        pre = jnp.dot(lower, xc, precision=_HIGHEST, preferred_element_type=F32)
        suf = jnp.dot(upper, xc, precision=_HIGHEST, preferred_element_type=F32)
        acs_ref[rows, :] = jnp.where(fwd_col, pre, suf)
        xt = dtat[:, c * q:(c + 1) * q]
        pre_t = jnp.dot(xt, upper, precision=_HIGHEST, preferred_element_type=F32)
        suf_t = jnp.dot(xt, lower, precision=_HIGHEST, preferred_element_type=F32)
        acst_ref[c] = jnp.where(fwd_row, pre_t, suf_t)
        dtt_ref[c] = dtt[:, c * q:(c + 1) * q]


def _ssd_dt(a, w_dt, dt_bias, a_log, nb, ln):
    d = a.shape[1]
    h2 = w_dt.shape[1]
    nc = ln // SSD_CHUNK
    bias = dt_bias.reshape(1, h2)
    alog = a_log.reshape(1, h2)
    vec = pl.BlockSpec((1, h2), lambda b: (0, 0))
    vect = pl.BlockSpec((h2, 1), lambda b: (0, 0))
    return pl.pallas_call(
        functools.partial(_ssd_dt_kernel, heads=h2 // 2),
        grid=(nb,),
        in_specs=[
            pl.BlockSpec((ln, d), lambda b: (b, 0)),
            pl.BlockSpec((d, h2), lambda b: (0, 0)),
            pl.BlockSpec((h2, d), lambda b: (0, 0)),
            vec, vect, vec, vect,
        ],
        out_specs=[
            pl.BlockSpec((None, ln, h2), lambda b: (b, 0, 0)),
            pl.BlockSpec((None, ln, h2), lambda b: (b, 0, 0)),
            pl.BlockSpec((None, nc, h2, SSD_CHUNK), lambda b: (b, 0, 0, 0)),
            pl.BlockSpec((None, nc, h2, SSD_CHUNK), lambda b: (b, 0, 0, 0)),
        ],
        out_shape=[
            jax.ShapeDtypeStruct((nb, ln, h2), F32),
            jax.ShapeDtypeStruct((nb, ln, h2), F32),
            jax.ShapeDtypeStruct((nb, nc, h2, SSD_CHUNK), F32),
            jax.ShapeDtypeStruct((nb, nc, h2, SSD_CHUNK), F32),
        ],
        compiler_params=_cparams("parallel"),
        name="ssd_dt",
    )(a, w_dt, w_dt.T, bias, bias.reshape(h2, 1), alog, alog.reshape(h2, 1))


def _ssd_scan_kernel(xs_ref, b_ref, c_ref, z_ref, a2_ref, d2_ref, acst_ref, dtt_ref, h0_ref, d0_ref, d1_ref,
                     ng_ref, o_ref, hfin_ref, yf_scr, h_scr, *, nc, heads_per_group):
    q = SSD_CHUNK
    r_heads = heads_per_group
    p = SSD_HEAD_DIM
    gw = r_heads * p
    expand = (lax.broadcasted_iota(jnp.int32, (2 * r_heads, 2 * gw), 1) // p
              == lax.broadcasted_iota(jnp.int32, (2 * r_heads, 2 * gw), 0)).astype(BF16)
    li = lax.broadcasted_iota(jnp.int32, (q, q), 0)
    si = lax.broadcasted_iota(jnp.int32, (q, q), 1)
    lane = lax.broadcasted_iota(jnp.int32, (q, gw), 1)
    even_head = (lane % (2 * p)) < p
    first_half = lax.broadcasted_iota(jnp.int32, (q, 2 * r_heads), 1) < r_heads
    dskip = d0_ref[...] + d1_ref[...]
    norm_g = ng_ref[...]

    def chunk(c, direction):
        rows = pl.ds(pl.multiple_of(c * q, q), q)
        xs_bf = xs_ref[rows, :]
        xs = xs_bf.astype(F32)
        bc = b_ref[rows, :]
        cc = c_ref[rows, :]
        a2 = a2_ref[direction, rows, :]
        d2 = d2_ref[direction, rows, :]
        acs_rows = acst_ref[c]
        dt_rows = dtt_ref[c]
        last = a2[q - 1:q, :] if direction == 0 else a2[0:1, :]
        fac = jnp.exp(jnp.where(first_half, a2, last - a2)) * d2
        fac_hi = fac.astype(BF16)
        fac_lo = (fac - fac_hi.astype(F32)).astype(BF16)
        fac_e = (jnp.dot(fac_hi, expand, preferred_element_type=F32)
                 + jnp.dot(fac_lo, expand, preferred_element_type=F32))
        ea_e = fac_e[:, :gw]
        wf_e = fac_e[:, gw:]
        h_decay = ea_e[q - 1:q, :] if direction == 0 else ea_e[0:1, :]
        causal = (li >= si) if direction == 0 else (li <= si)
        scores = lax.dot_general(cc, bc, (((1,), (1,)), ((), ())), preferred_element_type=F32)
        h = h_scr[...]
        y = jnp.dot(cc, h.astype(BF16), preferred_element_type=F32) * ea_e
        xs_even = jnp.where(even_head, xs_bf, jnp.zeros_like(xs_bf))
        xs_odd = jnp.where(even_head, jnp.zeros_like(xs_bf), xs_bf)
        pieces = []
        for pair in range(r_heads // 2):
            ms = []
            for sub in range(2):
                r = 2 * pair + sub
                col = direction * r_heads + r
                seg = a2[:, r:r + 1] - acs_rows[col:col + 1, :]
                decay = jnp.exp(jnp.where(causal, seg, -jnp.inf))
                ms.append((scores * decay * dt_rows[col:col + 1, :]).astype(BF16))
            cols = slice(pair * 2 * p, (pair + 1) * 2 * p)
            pieces.append(jnp.dot(jnp.concatenate(ms, axis=1),
                                  jnp.concatenate([xs_even[:, cols], xs_odd[:, cols]], axis=0),
                                  preferred_element_type=F32))
        y = y + jnp.concatenate(pieces, axis=1)
        w = (xs * wf_e).astype(BF16)
        h_scr[...] = h * h_decay + lax.dot_general(
            bc, w, (((0,), (0,)), ((), ())), preferred_element_type=F32)
        if direction == 0:
            yf_scr[rows, :] = y
        else:
            tot = yf_scr[rows, :] + y + dskip * xs
            gz = tot * _silu(z_ref[rows, :].astype(F32))
            gz = gz * lax.rsqrt(jnp.mean(gz * gz, axis=-1, keepdims=True) + EPS)
            o_ref[rows, :] = (gz * norm_g).astype(o_ref.dtype)

    h_scr[...] = h0_ref[0]

    def fwd_body(c, carry):
        chunk(c, 0)
        return carry

    lax.fori_loop(0, nc, fwd_body, 0)
    hfin_ref[0] = h_scr[...]
    h_scr[...] = h0_ref[1]

    def bwd_body(i, carry):
        chunk(nc - 1 - i, 1)
        return carry

    lax.fori_loop(0, nc, bwd_body, 0)
    hfin_ref[1] = h_scr[...]


def _ssd_scan(xbc, z, a2, d2, acst4, dtt4, h0, d0e, d1e, norm_g, nb, ln):
    d_inner = z.shape[1]
    groups = SSD_GROUPS
    gw = d_inner // groups
    n = SSD_STATE
    r2 = a2.shape[-1]
    nc = ln // SSD_CHUNK
    b_blk = d_inner // n
    c_blk = b_blk + groups
    seq_g = pl.BlockSpec((None, None, 2, ln, r2), lambda b, g: (b, g, 0, 0, 0))
    chunk_g = pl.BlockSpec((None, None, nc, r2, SSD_CHUNK), lambda b, g: (b, g, 0, 0, 0))
    state = pl.BlockSpec((None, None, 2, n, gw), lambda b, g: (b, g, 0, 0, 0))
    vec = pl.BlockSpec((1, gw), lambda b, g: (0, g))
    return pl.pallas_call(
        functools.partial(_ssd_scan_kernel, nc=nc, heads_per_group=r2 // 2),
        grid=(nb, groups),
        in_specs=[
            pl.BlockSpec((ln, gw), lambda b, g: (b, g)),
            pl.BlockSpec((ln, n), lambda b, g: (b, b_blk + g)),
            pl.BlockSpec((ln, n), lambda b, g: (b, c_blk + g)),
            pl.BlockSpec((ln, gw), lambda b, g: (b, g)),
            seq_g,
            seq_g,
            chunk_g,
            chunk_g,
            state,
            vec, vec, vec,
        ],
        out_specs=[
            pl.BlockSpec((ln, gw), lambda b, g: (b, g)),
            state,
        ],
        out_shape=[
            jax.ShapeDtypeStruct((nb * ln, d_inner), BF16),
            jax.ShapeDtypeStruct(h0.shape, F32),
        ],
        scratch_shapes=[pltpu.VMEM((ln, gw), F32), pltpu.VMEM((n, gw), F32)],
        compiler_params=_cparams("parallel", "parallel"),
        name="ssd_scan",
    )(xbc, xbc, xbc, z, a2, d2, acst4, dtt4, h0, d0e, d1e, norm_g)


def _group_token_major(t, groups):
    nb, ln, h2 = t.shape
    r = h2 // (2 * groups)
    return t.reshape(nb, ln, 2, groups, r).transpose(0, 3, 2, 1, 4)


def _group_head_major(t, groups):
    nb, nc, h2, q = t.shape
    r = h2 // (2 * groups)
    return t.reshape(nb, nc, 2, groups, r, q).transpose(0, 3, 1, 2, 4, 5).reshape(nb, groups, nc, 2 * r, q)


def _ssd_mixer(a, nb, ln, weights, h0):
    w_xbc, w_z, w_dt, conv_w, conv_b, dt_bias, a_log, d0e, d1e, norm_g = weights
    groups = SSD_GROUPS
    xbc = _proj_conv(a, w_xbc, conv_w, conv_b, width=ln, vertical=False, n_out=w_xbc.shape[1], tn=512)
    z = _matmul(a, w_z, BF16)
    dt, acs, dtt, acst = _ssd_dt(a, w_dt, dt_bias, a_log, nb, ln)
    acs5 = _group_token_major(acs, groups)
    dt5 = _group_token_major(dt, groups)
    a2 = jnp.concatenate([acs5, acs5], axis=-1)
    d2 = jnp.concatenate([jnp.ones_like(dt5), dt5], axis=-1)
    return _ssd_scan(xbc, z, a2, d2, _group_head_major(acst, groups), _group_head_major(dtt, groups),
                     h0, d0e, d1e, norm_g, nb, ln)


def kernel(x, c, ctx, c_ctx, w_mod, b_mod, norm_mix_g, norm_ffn_g, four_w, ssd_w_in, ssd_conv_w, ssd_conv_b,
           ssd_dt_bias, ssd_a_log, ssd_d, ssd_norm_g, ssd_w_out, ffn_w_up, ffn_conv_w, ffn_conv_b, ffn_w_down,
           final_g):
    nb, ln, d = x.shape
    lc = ctx.shape[1]
    depth = w_mod.shape[0]
    d_ff = ffn_w_down.shape[1]
    d_inner = ssd_w_out.shape[1]
    heads = ssd_dt_bias.shape[2]
    gn = SSD_GROUPS * SSD_STATE
    t_lat, t_ctx = nb * ln, nb * lc

    mod_rows = -(-(nb + 1) // SUBLANES) * SUBLANES
    cond = jnp.zeros((mod_rows, d), F32).at[:nb].set(c).at[nb].set(c_ctx)
    mods = _modulation(cond, w_mod, b_mod)

    def lat_mod(i, j):
        return mods[i, :nb, j * d:(j + 1) * d].reshape(nb, 1, d)

    def ctx_mod(i, j):
        return mods[i, nb:nb + 1, j * d:(j + 1) * d].reshape(1, 1, d)

    four_bf = four_w.astype(BF16)
    w_up_bf = ffn_w_up.astype(BF16)
    w_down_bf = ffn_w_down.astype(BF16)
    w_in_bf = ssd_w_in.astype(BF16)
    w_out_bf = ssd_w_out.astype(BF16)
    conv9 = ffn_conv_w.reshape(depth, 9, d_ff)

    dg = d // FOURIER_GROUPS
    cos_c, sin_c = _dft_tables(dg)
    wc = jnp.concatenate([cos_c, sin_c], axis=1).astype(BF16)
    cos_l, sin_l = _dft_tables(ln)
    cs_lat = jnp.concatenate([cos_l, -sin_l], axis=1).astype(BF16)
    cos_x, sin_x = _dft_tables(lc)
    cs_ctx = jnp.concatenate([cos_x, -sin_x], axis=1).astype(BF16)

    xl = x.reshape(t_lat, d)
    xc = ctx.reshape(t_ctx, d)
    a_lat = _normmod(xl, norm_mix_g[0], lat_mod(0, 0), lat_mod(0, 1), ln)
    a_ctx = _normmod(xc, norm_mix_g[0], ctx_mod(0, 0), ctx_mod(0, 1), t_ctx)
    out = None
    for i in range(depth):
        last = i == depth - 1
        is_ssd = i % 2 == 1
        j = i // 2
        if is_ssd:
            w_in = w_in_bf[j]
            xb = d_inner + gn
            state_cols = xb + 2 * heads
            w_xbc = jnp.concatenate([w_in[:, :xb], w_in[:, state_cols:state_cols + gn]], axis=1)
            w_dt = w_in[:, xb:state_cols]
            w_z = w_in[:, state_cols + gn:]
            d0e = jnp.repeat(ssd_d[j, 0], SSD_HEAD_DIM).reshape(1, d_inner)
            d1e = jnp.repeat(ssd_d[j, 1], SSD_HEAD_DIM).reshape(1, d_inner)
            weights = (w_xbc, w_z, w_dt, ssd_conv_w[j], ssd_conv_b[j], ssd_dt_bias[j], ssd_a_log[j], d0e, d1e,
                       ssd_norm_g[j].reshape(1, d_inner))
            zeros = jnp.zeros((nb, SSD_GROUPS, 2, SSD_STATE, d_inner // SSD_GROUPS), F32)
            mix_ctx, h_ctx = _ssd_mixer(a_ctx, nb, lc, weights, zeros)
            mix_lat, _ = _ssd_mixer(a_lat, nb, ln, weights, h_ctx)
            w_mix = w_out_bf[j]
        else:
            mix_lat = _fourier(a_lat, nb, ln, wc, cs_lat)
            mix_ctx = None if last else _fourier(a_ctx, nb, lc, wc, cs_ctx)
            w_mix = four_bf[j]
        xl, b_lat = _matmul_residual(mix_lat, w_mix, xl, lat_mod(i, 2), norm_ffn_g[i], lat_mod(i, 3),
                                     lat_mod(i, 4), ln)
        act = _proj_conv(b_lat, w_up_bf[i], conv9[i], ffn_conv_b[i], width=GRID_W, vertical=True,
                         n_out=d_ff, val_offset=d_ff, tn=256)
        if last:
            zero_mod = jnp.zeros((nb, 1, d), F32)
            _, out = _matmul_residual(act, w_down_bf[i], xl, lat_mod(i, 5), final_g, zero_mod, zero_mod, ln,
                                      modulate=False, a_dtype=F32)
        else:
            xl, a_lat = _matmul_residual(act, w_down_bf[i], xl, lat_mod(i, 5), norm_mix_g[i + 1],
                                         lat_mod(i + 1, 0), lat_mod(i + 1, 1), ln)
            xc, b_ctx = _matmul_residual(mix_ctx, w_mix, xc, ctx_mod(i, 2), norm_ffn_g[i], ctx_mod(i, 3),
                                         ctx_mod(i, 4), t_ctx)
            act_c = _proj_conv(b_ctx, w_up_bf[i], ffn_conv_w[i, 1], ffn_conv_b[i], width=lc, vertical=False,
                               n_out=d_ff, val_offset=d_ff, tn=256)
            xc, a_ctx = _matmul_residual(act_c, w_down_bf[i], xc, ctx_mod(i, 5), norm_mix_g[i + 1],
                                         ctx_mod(i + 1, 0), ctx_mod(i + 1, 1), t_ctx)
    return out.reshape(nb, ln, d)
```

```python
import functools
import math

import jax
import jax.numpy as jnp
from jax import lax
from jax.experimental import pallas as pl
from jax.experimental.pallas import tpu as pltpu

F32 = jnp.float32
BF16 = jnp.bfloat16

EPS = 1e-6
GRID_W = 64
FOURIER_GROUPS = 8
SSD_HEAD_DIM = 64
SSD_GROUPS = 8
SSD_STATE = 128
SSD_CHUNK = 128
SCAN_UNROLL = 4

LANES = 128
SUBLANES = 8
VMEM_LIMIT_BYTES = 56 * 1024 * 1024
ROW_TILE = 2048
EPILOGUE_ROWS = 256

_HIGHEST = lax.Precision.HIGHEST


def _cparams(*sem, flags=None):
    return pltpu.CompilerParams(dimension_semantics=sem, vmem_limit_bytes=VMEM_LIMIT_BYTES, flags=flags)


def _norm_rows(x, g, shift, scale, modulate):
    ms = jnp.mean(x * x, axis=-1, keepdims=True)
    y = x * lax.rsqrt(ms + EPS) * g
    if modulate:
        y = y * (1.0 + scale) + shift
    return y


def _silu(v):
    return v * jax.nn.sigmoid(v)


def _mod_kernel(s_ref, w_ref, b_ref, o_ref):
    s = _silu(s_ref[...]).astype(BF16)
    w = w_ref[0].astype(BF16)
    o_ref[0] = jnp.dot(s, w, preferred_element_type=F32) + b_ref[0]


def _modulation(cond, w_mod, b_mod):
    depth, d, n = w_mod.shape
    rows = cond.shape[0]
    tn = 1024
    return pl.pallas_call(
        _mod_kernel,
        grid=(depth, n // tn),
        in_specs=[
            pl.BlockSpec((rows, d), lambda i, j: (0, 0)),
            pl.BlockSpec((1, d, tn), lambda i, j: (i, 0, j)),
            pl.BlockSpec((1, 1, tn), lambda i, j: (i, 0, j)),
        ],
        out_specs=pl.BlockSpec((1, rows, tn), lambda i, j: (i, 0, j)),
        out_shape=jax.ShapeDtypeStruct((depth, rows, n), F32),
        compiler_params=_cparams("parallel", "parallel"),
        name="modulation",
    )(cond, w_mod, b_mod.reshape(depth, 1, n))


def _normmod_kernel(x_ref, g_ref, sh_ref, sc_ref, o_ref):
    o_ref[...] = _norm_rows(x_ref[...], g_ref[...], sh_ref[0], sc_ref[0], True).astype(o_ref.dtype)


def _normmod(x, g, shift, scale, rows_per_mod):
    m, d = x.shape
    tm = 512
    mod_spec = pl.BlockSpec((1, 1, d), lambda i: ((i * tm) // rows_per_mod, 0, 0))
    return pl.pallas_call(
        _normmod_kernel,
        grid=(m // tm,),
        in_specs=[
            pl.BlockSpec((tm, d), lambda i: (i, 0)),
            pl.BlockSpec((1, d), lambda i: (0, 0)),
            mod_spec,
            mod_spec,
        ],
        out_specs=pl.BlockSpec((tm, d), lambda i: (i, 0)),
        out_shape=jax.ShapeDtypeStruct((m, d), BF16),
        compiler_params=_cparams("parallel"),
        name="normmod",
    )(x, g.reshape(1, d), shift, scale)


def _mm_kernel(a_ref, w_ref, o_ref):
    o_ref[...] = jnp.dot(a_ref[...], w_ref[...], preferred_element_type=F32).astype(o_ref.dtype)


def _matmul(a, w, layer, out_dtype, tm=1024, tn=1024):
    m, k = a.shape
    n = w.shape[2]
    tm, tn = min(tm, m), min(tn, n)
    return pl.pallas_call(
        _mm_kernel,
        grid=(m // tm, n // tn),
        in_specs=[
            pl.BlockSpec((tm, k), lambda i, j: (i, 0)),
            pl.BlockSpec((None, k, tn), lambda i, j: (layer, 0, j)),
        ],
        out_specs=pl.BlockSpec((tm, tn), lambda i, j: (i, j)),
        out_shape=jax.ShapeDtypeStruct((m, n), out_dtype),
        compiler_params=_cparams("parallel", "parallel"),
        name="matmul",
    )(a, w)


def _mm_res_kernel(a_ref, w_ref, x_ref, gate_ref, g_ref, sh_ref, sc_ref, xo_ref, ao_ref, xn_scr, ssq_scr,
                   *, nn, modulate):
    j = pl.program_id(1)
    tn = w_ref.shape[2]
    xn = x_ref[...] + gate_ref[0] * jnp.dot(a_ref[...], w_ref[j], preferred_element_type=F32)
    xo_ref[...] = xn
    xn_scr[j] = xn
    ssq = jnp.sum(xn * xn, axis=-1, keepdims=True)

    @pl.when(j == 0)
    def _():
        ssq_scr[...] = ssq

    @pl.when(j > 0)
    def _():
        ssq_scr[...] += ssq

    @pl.when(j == nn - 1)
    def _():
        inv = lax.rsqrt(ssq_scr[...] * (1.0 / (nn * tn)) + EPS)
        for c in range(nn):
            cols = slice(c * tn, (c + 1) * tn)
            y = xn_scr[c] * inv * g_ref[:, cols]
            if modulate:
                y = y * (1.0 + sc_ref[0][:, cols]) + sh_ref[0][:, cols]
            ao_ref[:, cols] = y.astype(ao_ref.dtype)


def _residual_tile_cols(k):
    return 1024 if k <= 4096 else 512


def _column_tiles(w):
    layers, k, d = w.shape
    tn = _residual_tile_cols(k)
    return w.astype(BF16).reshape(layers, k, d // tn, tn).transpose(0, 2, 1, 3)


def _matmul_residual(a, w_tiles, layer, x, gate, g_next, shift, scale, rows_per_mod, modulate=True,
                     a_dtype=BF16):
    m, k = a.shape
    nn, tn = w_tiles.shape[1], w_tiles.shape[3]
    d = nn * tn
    tm = min(512, m)

    def mod_row(i, j):
        return ((i * tm) // rows_per_mod, 0, 0)

    mod_spec = pl.BlockSpec((1, 1, d), mod_row)
    return pl.pallas_call(
        functools.partial(_mm_res_kernel, nn=nn, modulate=modulate),
        grid=(m // tm, nn),
        in_specs=[
            pl.BlockSpec((tm, k), lambda i, j: (i, 0)),
            pl.BlockSpec((None, nn, k, tn), lambda i, j: (layer, 0, 0, 0), pipeline_mode=pl.Buffered(1)),
            pl.BlockSpec((tm, tn), lambda i, j: (i, j)),
            pl.BlockSpec((1, 1, tn), lambda i, j: ((i * tm) // rows_per_mod, 0, j)),
            pl.BlockSpec((1, d), lambda i, j: (0, 0)),
            mod_spec,
            mod_spec,
        ],
        out_specs=[
            pl.BlockSpec((tm, tn), lambda i, j: (i, j)),
            pl.BlockSpec((tm, d), lambda i, j: (i, 0)),
        ],
        out_shape=[
            jax.ShapeDtypeStruct((m, d), F32),
            jax.ShapeDtypeStruct((m, d), a_dtype),
        ],
        scratch_shapes=[pltpu.VMEM((nn, tm, tn), F32), pltpu.VMEM((tm, 1), F32)],
        compiler_params=_cparams("parallel", "arbitrary"),
        name="matmul_residual",
    )(a, w_tiles, x, gate, g_next.reshape(1, d), shift, scale)


def _proj_conv_kernel(*refs, rows, width, vertical, gated):
    refs = list(refs)
    a_ref, wg_ref = refs[:2]
    wv_ref = refs.pop(2) if gated else None
    cw_ref, cb_ref, o_ref, u_scr = refs[2:6]
    scratch = refs[6:]
    ul_scr, ur_scr = scratch[:2] if vertical else (None, None)
    v_scr = scratch[-1] if gated else None
    tn = o_ref.shape[1]
    pad = (u_scr.shape[0] - rows) // 2
    rc = min(EPILOGUE_ROWS, rows)
    a = a_ref[...]
    u_scr[pl.ds(pad, rows), :] = jnp.dot(a, wg_ref[...], preferred_element_type=F32)
    zeros = jnp.zeros((pad, tn), F32)
    u_scr[pl.ds(0, pad), :] = zeros
    u_scr[pl.ds(pad + rows, pad), :] = zeros
    if gated:
        v_scr[...] = jnp.dot(a, wv_ref[...], preferred_element_type=F32)

    def shifted(r):
        col = (lax.broadcasted_iota(jnp.int32, (rc, tn), 0) + r * rc) & (width - 1)
        left = jnp.where(col != 0, u_scr[pl.ds(pad + r * rc - 1, rc), :], 0.0)
        right = jnp.where(col != width - 1, u_scr[pl.ds(pad + r * rc + 1, rc), :], 0.0)
        return left, right

    cw = cw_ref[...]
    bias = cb_ref[...]
    if vertical:
        ul_scr[pl.ds(0, pad), :] = zeros
        ul_scr[pl.ds(pad + rows, pad), :] = zeros
        ur_scr[pl.ds(0, pad), :] = zeros
        ur_scr[pl.ds(pad + rows, pad), :] = zeros
        for r in range(rows // rc):
            left, right = shifted(r)
            ul_scr[pl.ds(pad + r * rc, rc), :] = left
            ur_scr[pl.ds(pad + r * rc, rc), :] = right
    for r in range(rows // rc):
        base = pad + r * rc
        if vertical:
            acc = bias
            for di in range(3):
                off = base + (di - 1) * width
                acc = acc + ul_scr[pl.ds(off, rc), :] * cw[3 * di:3 * di + 1, :]
                acc = acc + u_scr[pl.ds(off, rc), :] * cw[3 * di + 1:3 * di + 2, :]
                acc = acc + ur_scr[pl.ds(off, rc), :] * cw[3 * di + 2:3 * di + 3, :]
        else:
            left, right = shifted(r)
            acc = left * cw[0:1, :] + u_scr[pl.ds(base, rc), :] * cw[1:2, :] + right * cw[2:3, :] + bias
        y = _silu(acc)
        if gated:
            y = y * v_scr[pl.ds(r * rc, rc), :]
        o_ref[pl.ds(r * rc, rc), :] = y.astype(o_ref.dtype)


def _proj_conv(a, w, layer, conv_w, conv_b, *, width, vertical, n_out, val_offset=None, tn):
    m, k = a.shape
    rows = min(ROW_TILE, m)
    gated = val_offset is not None
    assert width & (width - 1) == 0 and rows % width == 0 and m % rows == 0
    pad = (width + SUBLANES) if vertical else SUBLANES
    taps = conv_w.shape[0]
    in_specs = [
        pl.BlockSpec((rows, k), lambda i, j: (i, 0)),
        pl.BlockSpec((None, k, tn), lambda i, j: (layer, 0, j)),
    ]
    operands = [a, w]
    padded = pltpu.VMEM((rows + 2 * pad, tn), F32)
    scratch = [padded] + ([padded, padded] if vertical else []) + ([pltpu.VMEM((rows, tn), F32)] if gated else [])
    if gated:
        voff = val_offset // tn
        in_specs.append(pl.BlockSpec((None, k, tn), lambda i, j: (layer, 0, j + voff)))
        operands.append(w)
    in_specs += [
        pl.BlockSpec((taps, tn), lambda i, j: (0, j)),
        pl.BlockSpec((1, tn), lambda i, j: (0, j)),
    ]
    operands += [conv_w, conv_b.reshape(1, n_out)]
    return pl.pallas_call(
        functools.partial(_proj_conv_kernel, rows=rows, width=width, vertical=vertical, gated=gated),
        grid=(m // rows, n_out // tn),
        in_specs=in_specs,
        out_specs=pl.BlockSpec((rows, tn), lambda i, j: (i, j)),
        out_shape=jax.ShapeDtypeStruct((m, n_out), BF16),
        scratch_shapes=scratch,
        compiler_params=_cparams("parallel", "parallel"),
        name="proj_conv",
    )(*operands)


def _dft_tables(n):
    k = jnp.arange(n, dtype=jnp.int32)
    idx = (k[:, None] * k[None, :]) % n
    ang = idx.astype(F32) * F32(2.0 * math.pi / n)
    s = F32(1.0 / math.sqrt(n))
    return jnp.cos(ang) * s, jnp.sin(ang) * s


def _chan_dft_kernel(a_ref, wc_ref, o_ref, *, groups):
    dg = wc_ref.shape[0]
    wc = wc_ref[...]
    for g in range(groups):
        cols = pl.ds(g * dg, dg)
        res = jnp.dot(a_ref[:, cols], wc, preferred_element_type=F32)
        o_ref[0, :, cols] = res[:, :dg].astype(o_ref.dtype)
        o_ref[1, :, cols] = res[:, dg:].astype(o_ref.dtype)


def _chan_dft(a, wc, nb, ln):
    d = a.shape[1]
    dg = wc.shape[0]
    tm = min(512, ln)
    mt = ln // tm
    return pl.pallas_call(
        functools.partial(_chan_dft_kernel, groups=d // dg),
        grid=(nb, mt),
        in_specs=[
            pl.BlockSpec((tm, d), lambda b, i: (b * mt + i, 0)),
            pl.BlockSpec((dg, 2 * dg), lambda b, i: (0, 0)),
        ],
        out_specs=pl.BlockSpec((None, 2, tm, d), lambda b, i: (b, 0, i, 0)),
        out_shape=jax.ShapeDtypeStruct((nb, 2, ln, d), BF16),
        compiler_params=_cparams("parallel", "parallel"),
        name="chan_dft",
    )(a, wc)


def _pos_dft(cs, y):
    nb, k2, d = y.shape
    ln = cs.shape[0]
    tm, tn = min(1024, ln), 1024
    return pl.pallas_call(
        _mm_kernel,
        grid=(nb, d // tn, ln // tm),
        in_specs=[
            pl.BlockSpec((tm, k2), lambda b, j, i: (i, 0)),
            pl.BlockSpec((None, k2, tn), lambda b, j, i: (b, 0, j)),
        ],
        out_specs=pl.BlockSpec((None, tm, tn), lambda b, j, i: (b, i, j)),
        out_shape=jax.ShapeDtypeStruct((nb, ln, d), BF16),
        compiler_params=_cparams("parallel", "parallel", "parallel"),
        name="pos_dft",
    )(cs, y)


def _fourier(a, nb, ln, wc, cs):
    d = a.shape[1]
    y = _chan_dft(a, wc, nb, ln)
    f = _pos_dft(cs, y.reshape(nb, 2 * ln, d))
    return f.reshape(nb * ln, d)


def _ssd_dt_kernel(a_ref, wt_ref, bias_ref, alog_ref, dtt_ref, acst_ref, *, heads, groups):
    ln = a_ref.shape[0]
    q = SSD_CHUNK
    r = heads // groups
    dtt = jax.nn.softplus(
        lax.dot_general(wt_ref[...], a_ref[...], (((1,), (1,)), ((), ())), preferred_element_type=F32)
        + bias_ref[...])
    dtat = dtt * (-jnp.exp(alog_ref[...]))
    ri = lax.broadcasted_iota(jnp.int32, (q, q), 0)
    ci = lax.broadcasted_iota(jnp.int32, (q, q), 1)
    lower = (ri >= ci).astype(F32)
    upper = (ri <= ci).astype(F32)
    fwd_row = lax.broadcasted_iota(jnp.int32, (2 * heads, q), 0) < heads
    for c in range(ln // q):
        xt = dtat[:, c * q:(c + 1) * q]
        prefix = jnp.dot(xt, upper, precision=_HIGHEST, preferred_element_type=F32)
        suffix = jnp.dot(xt, lower, precision=_HIGHEST, preferred_element_type=F32)
        acs = jnp.where(fwd_row, prefix, suffix)
        dtc = dtt[:, c * q:(c + 1) * q]
        for g in range(groups):
            for half in range(2):
                src = slice(half * heads + g * r, half * heads + (g + 1) * r)
                dst = pl.ds(half * r, r)
                acst_ref[g, c, dst, :] = acs[src, :]
                dtt_ref[g, c, dst, :] = dtc[src, :]


def _ssd_dt(a, w_dt_t, dt_bias, a_log, nb, ln):
    d = a.shape[1]
    h2 = w_dt_t.shape[0]
    groups = SSD_GROUPS
    nc = ln // SSD_CHUNK
    vec = pl.BlockSpec((h2, 1), lambda b: (0, 0))
    out = pl.BlockSpec((None, groups, nc, h2 // groups, SSD_CHUNK), lambda b: (b, 0, 0, 0, 0))
    shape = jax.ShapeDtypeStruct((nb, groups, nc, h2 // groups, SSD_CHUNK), F32)
    return pl.pallas_call(
        functools.partial(_ssd_dt_kernel, heads=h2 // 2, groups=groups),
        grid=(nb,),
        in_specs=[
            pl.BlockSpec((ln, d), lambda b: (b, 0)),
            pl.BlockSpec((h2, d), lambda b: (0, 0)),
            vec, vec,
        ],
        out_specs=[out, out],
        out_shape=[shape, shape],
        compiler_params=_cparams("parallel"),
        name="ssd_dt",
    )(a, w_dt_t, dt_bias.reshape(h2, 1), a_log.reshape(h2, 1))


def _ssd_scan_kernel(xs_ref, b_ref, c_ref, z_ref, acst_ref, dtt_ref, h0_ref, d0_ref, d1_ref,
                     ng_ref, o_ref, hfin_ref, yf_scr, h_scr, tok_scr, *, nc, heads_per_group):
    q = SSD_CHUNK
    r_heads = heads_per_group
    p = SSD_HEAD_DIM
    gw = r_heads * p
    expand = (lax.broadcasted_iota(jnp.int32, (2 * r_heads, 2 * gw), 1) // p
              == lax.broadcasted_iota(jnp.int32, (2 * r_heads, 2 * gw), 0)).astype(BF16)
    li = lax.broadcasted_iota(jnp.int32, (q, q), 0)
    si = lax.broadcasted_iota(jnp.int32, (q, q), 1)
    lane = lax.broadcasted_iota(jnp.int32, (q, gw), 1)
    even_head = (lane % (2 * p)) < p
    pad_rows = jnp.zeros((q - 3 * r_heads, q), F32)
    dskip = d0_ref[...] + d1_ref[...]
    norm_g = ng_ref[...]

    def chunk(c, direction):
        rows = pl.ds(pl.multiple_of(c * q, q), q)
        xs_bf = xs_ref[rows, :]
        xs = xs_bf.astype(F32)
        bc = b_ref[rows, :]
        cc = c_ref[rows, :]
        acs_rows = acst_ref[c]
        dt_rows = dtt_ref[c]
        tok = tok_scr[direction, c]
        fac = tok[:, :2 * r_heads]
        fac_hi = fac.astype(BF16)
        fac_lo = (fac - fac_hi.astype(F32)).astype(BF16)
        fac_e = (jnp.dot(fac_hi, expand, preferred_element_type=F32)
                 + jnp.dot(fac_lo, expand, preferred_element_type=F32))
        ea_e = fac_e[:, :gw]
        wf_e = fac_e[:, gw:]
        h_decay = ea_e[q - 1:q, :] if direction == 0 else ea_e[0:1, :]
        causal = (li >= si) if direction == 0 else (li <= si)
        scores = lax.dot_general(cc, bc, (((1,), (1,)), ((), ())), preferred_element_type=F32)
        h = h_scr[...]
        y = jnp.dot(cc, h.astype(BF16), preferred_element_type=F32) * ea_e
        xs_even = jnp.where(even_head, xs_bf, jnp.zeros_like(xs_bf))
        xs_odd = jnp.where(even_head, jnp.zeros_like(xs_bf), xs_bf)
        pieces = []
        for pair in range(r_heads // 2):
            ms = []
            for sub in range(2):
                r = 2 * pair + sub
                col = direction * r_heads + r
                seg = tok[:, 2 * r_heads + r:2 * r_heads + r + 1] - acs_rows[col:col + 1, :]
                decay = jnp.exp(jnp.where(causal, seg, -jnp.inf))
                ms.append((scores * decay * dt_rows[col:col + 1, :]).astype(BF16))
            cols = slice(pair * 2 * p, (pair + 1) * 2 * p)
            pieces.append(jnp.dot(jnp.concatenate(ms, axis=1),
                                  jnp.concatenate([xs_even[:, cols], xs_odd[:, cols]], axis=0),
                                  preferred_element_type=F32))
        y = y + jnp.concatenate(pieces, axis=1)
        w = (xs * wf_e).astype(BF16)
        h_scr[...] = h * h_decay + lax.dot_general(
            bc, w, (((0,), (0,)), ((), ())), preferred_element_type=F32)
        if direction == 0:
            yf_scr[rows, :] = y
        else:
            tot = yf_scr[rows, :] + y + dskip * xs
            gz = tot * _silu(z_ref[rows, :].astype(F32))
            gz = gz * lax.rsqrt(jnp.mean(gz * gz, axis=-1, keepdims=True) + EPS)
            o_ref[rows, :] = (gz * norm_g).astype(o_ref.dtype)

    def to_token_major(c, carry):
        acs_rows = acst_ref[c]
        dt_rows = dtt_ref[c]
        for direction in range(2):
            acs_d = acs_rows[direction * r_heads:(direction + 1) * r_heads, :]
            dt_d = dt_rows[direction * r_heads:(direction + 1) * r_heads, :]
            last = acs_d[:, q - 1:q] if direction == 0 else acs_d[:, 0:1]
            tok_scr[direction, c] = jnp.concatenate(
                [jnp.exp(acs_d), dt_d * jnp.exp(last - acs_d), acs_d, pad_rows], axis=0).T
        return carry

    lax.fori_loop(0, nc, to_token_major, 0, unroll=True)
    h_scr[...] = h0_ref[0]

    def fwd_body(c, carry):
        chunk(c, 0)
        return carry

    lax.fori_loop(0, nc, fwd_body, 0, unroll=min(SCAN_UNROLL, nc))
    hfin_ref[0] = h_scr[...]
    h_scr[...] = h0_ref[1]

    def bwd_body(i, carry):
        chunk(nc - 1 - i, 1)
        return carry

    lax.fori_loop(0, nc, bwd_body, 0, unroll=min(SCAN_UNROLL, nc))
    hfin_ref[1] = h_scr[...]


def _ssd_scan(xbc, z, acst4, dtt4, h0, d0e, d1e, norm_g, nb, ln):
    d_inner = z.shape[1]
    groups = SSD_GROUPS
    gw = d_inner // groups
    n = SSD_STATE
    r2 = acst4.shape[-2]
    nc = ln // SSD_CHUNK
    b_blk = d_inner // n
    c_blk = b_blk + groups
    chunk_g = pl.BlockSpec((None, None, nc, r2, SSD_CHUNK), lambda b, g: (b, g, 0, 0, 0))
    state = pl.BlockSpec((None, None, 2, n, gw), lambda b, g: (b, g, 0, 0, 0))
    vec = pl.BlockSpec((1, gw), lambda b, g: (0, g))
    return pl.pallas_call(
        functools.partial(_ssd_scan_kernel, nc=nc, heads_per_group=r2 // 2),
        grid=(nb, groups),
        in_specs=[
            pl.BlockSpec((ln, gw), lambda b, g: (b, g)),
            pl.BlockSpec((ln, n), lambda b, g: (b, b_blk + g)),
            pl.BlockSpec((ln, n), lambda b, g: (b, c_blk + g)),
            pl.BlockSpec((ln, gw), lambda b, g: (b, g)),
            chunk_g,
            chunk_g,
            state,
            vec, vec, vec,
        ],
        out_specs=[
            pl.BlockSpec((ln, gw), lambda b, g: (b, g)),
            state,
        ],
        out_shape=[
            jax.ShapeDtypeStruct((nb * ln, d_inner), BF16),
            jax.ShapeDtypeStruct(h0.shape, F32),
        ],
        scratch_shapes=[pltpu.VMEM((ln, gw), F32), pltpu.VMEM((n, gw), F32),
                        pltpu.VMEM((2, nc, SSD_CHUNK, SSD_CHUNK), F32)],
        compiler_params=_cparams("parallel", "parallel"),
        name="ssd_scan",
    )(xbc, xbc, xbc, z, acst4, dtt4, h0, d0e, d1e, norm_g)


def _ssd_mixer(a, nb, ln, weights, layer, h0):
    w_xbc, w_z, w_dt_t, conv_w, conv_b, dt_bias, a_log, d0e, d1e, norm_g = weights
    xbc = _proj_conv(a, w_xbc, layer, conv_w, conv_b, width=ln, vertical=False, n_out=w_xbc.shape[2], tn=512)
    z = _matmul(a, w_z, layer, BF16)
    dtt4, acst4 = _ssd_dt(a, w_dt_t, dt_bias, a_log, nb, ln)
    return _ssd_scan(xbc, z, acst4, dtt4, h0, d0e, d1e, norm_g, nb, ln)


def kernel(x, c, ctx, c_ctx, w_mod, b_mod, norm_mix_g, norm_ffn_g, four_w, ssd_w_in, ssd_conv_w, ssd_conv_b,
           ssd_dt_bias, ssd_a_log, ssd_d, ssd_norm_g, ssd_w_out, ffn_w_up, ffn_conv_w, ffn_conv_b, ffn_w_down,
           final_g):
    nb, ln, d = x.shape
    lc = ctx.shape[1]
    depth = w_mod.shape[0]
    d_ff = ffn_w_down.shape[1]
    d_inner = ssd_w_out.shape[1]
    heads = ssd_dt_bias.shape[2]
    gn = SSD_GROUPS * SSD_STATE
    t_lat, t_ctx = nb * ln, nb * lc

    mod_rows = -(-(nb + 1) // SUBLANES) * SUBLANES
    cond = jnp.zeros((mod_rows, d), F32).at[:nb].set(c).at[nb].set(c_ctx)
    mods = _modulation(cond, w_mod, b_mod)

    def lat_mod(i, j):
        return mods[i, :nb, j * d:(j + 1) * d].reshape(nb, 1, d)

    def ctx_mod(i, j):
        return mods[i, nb:nb + 1, j * d:(j + 1) * d].reshape(1, 1, d)

    four_tiles = _column_tiles(four_w)
    w_down_tiles = _column_tiles(ffn_w_down)
    w_out_tiles = _column_tiles(ssd_w_out)
    w_up_bf = ffn_w_up.astype(BF16)
    xb = d_inner + gn
    state_cols = xb + 2 * heads
    w_xbc_bf = jnp.concatenate([ssd_w_in[:, :, :xb], ssd_w_in[:, :, state_cols:state_cols + gn]],
                               axis=2).astype(BF16)
    w_z_bf = ssd_w_in[:, :, state_cols + gn:].astype(BF16)
    w_dt_t_bf = ssd_w_in[:, :, xb:state_cols].transpose(0, 2, 1).astype(BF16)
    conv9 = ffn_conv_w.reshape(depth, 9, d_ff)

    dg = d // FOURIER_GROUPS
    cos_c, sin_c = _dft_tables(dg)
    wc = jnp.concatenate([cos_c, sin_c], axis=1).astype(BF16)
    cos_l, sin_l = _dft_tables(ln)
    cs_lat = jnp.concatenate([cos_l, -sin_l], axis=1).astype(BF16)
    cos_x, sin_x = _dft_tables(lc)
    cs_ctx = jnp.concatenate([cos_x, -sin_x], axis=1).astype(BF16)

    xl = x.reshape(t_lat, d)
    xc = ctx.reshape(t_ctx, d)
    a_lat = _normmod(xl, norm_mix_g[0], lat_mod(0, 0), lat_mod(0, 1), ln)
    a_ctx = _normmod(xc, norm_mix_g[0], ctx_mod(0, 0), ctx_mod(0, 1), t_ctx)
    out = None
    for i in range(depth):
        last = i == depth - 1
        is_ssd = i % 2 == 1
        j = i // 2
        if is_ssd:
            d0e = jnp.repeat(ssd_d[j, 0], SSD_HEAD_DIM).reshape(1, d_inner)
            d1e = jnp.repeat(ssd_d[j, 1], SSD_HEAD_DIM).reshape(1, d_inner)
            weights = (w_xbc_bf, w_z_bf, w_dt_t_bf[j], ssd_conv_w[j], ssd_conv_b[j], ssd_dt_bias[j], ssd_a_log[j],
                       d0e, d1e, ssd_norm_g[j].reshape(1, d_inner))
            zeros = jnp.zeros((nb, SSD_GROUPS, 2, SSD_STATE, d_inner // SSD_GROUPS), F32)
            mix_ctx, h_ctx = _ssd_mixer(a_ctx, nb, lc, weights, j, zeros)
            mix_lat, _ = _ssd_mixer(a_lat, nb, ln, weights, j, h_ctx)
            w_mix = w_out_tiles
        else:
            mix_lat = _fourier(a_lat, nb, ln, wc, cs_lat)
            mix_ctx = None if last else _fourier(a_ctx, nb, lc, wc, cs_ctx)
            w_mix = four_tiles
        xl, b_lat = _matmul_residual(mix_lat, w_mix, j, xl, lat_mod(i, 2), norm_ffn_g[i], lat_mod(i, 3),
                                     lat_mod(i, 4), ln)
        act = _proj_conv(b_lat, w_up_bf, i, conv9[i], ffn_conv_b[i], width=GRID_W, vertical=True,
                         n_out=d_ff, val_offset=d_ff, tn=256)
        if last:
            zero_mod = jnp.zeros((nb, 1, d), F32)
            _, out = _matmul_residual(act, w_down_tiles, i, xl, lat_mod(i, 5), final_g, zero_mod, zero_mod, ln,
                                      modulate=False, a_dtype=F32)
        else:
            xl, a_lat = _matmul_residual(act, w_down_tiles, i, xl, lat_mod(i, 5), norm_mix_g[i + 1],
                                         lat_mod(i + 1, 0), lat_mod(i + 1, 1), ln)
            xc, b_ctx = _matmul_residual(mix_ctx, w_mix, j, xc, ctx_mod(i, 2), norm_ffn_g[i], ctx_mod(i, 3),
                                         ctx_mod(i, 4), t_ctx)
            act_c = _proj_conv(b_ctx, w_up_bf, i, ffn_conv_w[i, 1], ffn_conv_b[i], width=lc, vertical=False,
                               n_out=d_ff, val_offset=d_ff, tn=256)
            xc, a_ctx = _matmul_residual(act_c, w_down_tiles, i, xc, ctx_mod(i, 5), norm_mix_g[i + 1],
                                         ctx_mod(i + 1, 0), ctx_mod(i + 1, 1), t_ctx)
    return out.reshape(nb, ln, d)
```

```python
import functools
import math

import jax
import jax.numpy as jnp
from jax import lax
from jax.experimental import pallas as pl
from jax.experimental.pallas import tpu as pltpu

F32 = jnp.float32
BF16 = jnp.bfloat16

EPS = 1e-6
GRID_W = 64
FOURIER_GROUPS = 8
SSD_HEAD_DIM = 64
SSD_GROUPS = 8
SSD_STATE = 128
SSD_CHUNK = 128
SCAN_UNROLL = 4

LANES = 128
SUBLANES = 8
VMEM_LIMIT_BYTES = 56 * 1024 * 1024
ROW_TILE = 2048
EPILOGUE_ROWS = 256
PROJ_BLOCK_ROWS = 512

_HIGHEST = lax.Precision.HIGHEST


def _cparams(*sem, flags=None):
    return pltpu.CompilerParams(dimension_semantics=sem, vmem_limit_bytes=VMEM_LIMIT_BYTES, flags=flags)


def _norm_rows(x, g, shift, scale, modulate):
    ms = jnp.mean(x * x, axis=-1, keepdims=True)
    y = x * lax.rsqrt(ms + EPS) * g
    if modulate:
        y = y * (1.0 + scale) + shift
    return y


def _silu(v):
    return v * jax.nn.sigmoid(v)


def _mod_kernel(s_ref, w_ref, b_ref, o_ref):
    s = _silu(s_ref[...]).astype(BF16)
    w = w_ref[0].astype(BF16)
    o_ref[0] = jnp.dot(s, w, preferred_element_type=F32) + b_ref[0]


def _modulation(cond, w_mod, b_mod):
    depth, d, n = w_mod.shape
    rows = cond.shape[0]
    tn = 1024
    return pl.pallas_call(
        _mod_kernel,
        grid=(depth, n // tn),
        in_specs=[
            pl.BlockSpec((rows, d), lambda i, j: (0, 0)),
            pl.BlockSpec((1, d, tn), lambda i, j: (i, 0, j)),
            pl.BlockSpec((1, 1, tn), lambda i, j: (i, 0, j)),
        ],
        out_specs=pl.BlockSpec((1, rows, tn), lambda i, j: (i, 0, j)),
        out_shape=jax.ShapeDtypeStruct((depth, rows, n), F32),
        compiler_params=_cparams("parallel", "parallel"),
        name="modulation",
    )(cond, w_mod, b_mod.reshape(depth, 1, n))


def _normmod_kernel(x_ref, g_ref, sh_ref, sc_ref, o_ref):
    o_ref[...] = _norm_rows(x_ref[...], g_ref[...], sh_ref[0], sc_ref[0], True).astype(o_ref.dtype)


def _normmod(x, g, shift, scale, rows_per_mod):
    m, d = x.shape
    tm = 512
    mod_spec = pl.BlockSpec((1, 1, d), lambda i: ((i * tm) // rows_per_mod, 0, 0))
    return pl.pallas_call(
        _normmod_kernel,
        grid=(m // tm,),
        in_specs=[
            pl.BlockSpec((tm, d), lambda i: (i, 0)),
            pl.BlockSpec((1, d), lambda i: (0, 0)),
            mod_spec,
            mod_spec,
        ],
        out_specs=pl.BlockSpec((tm, d), lambda i: (i, 0)),
        out_shape=jax.ShapeDtypeStruct((m, d), BF16),
        compiler_params=_cparams("parallel"),
        name="normmod",
    )(x, g.reshape(1, d), shift, scale)


def _mm_kernel(a_ref, w_ref, o_ref):
    o_ref[...] = jnp.dot(a_ref[...], w_ref[...], preferred_element_type=F32).astype(o_ref.dtype)


def _matmul(a, w, layer, out_dtype, tm=1024, tn=1024):
    m, k = a.shape
    n = w.shape[2]
    tm, tn = min(tm, m), min(tn, n)
    return pl.pallas_call(
        _mm_kernel,
        grid=(m // tm, n // tn),
        in_specs=[
            pl.BlockSpec((tm, k), lambda i, j: (i, 0)),
            pl.BlockSpec((None, k, tn), lambda i, j: (layer, 0, j)),
        ],
        out_specs=pl.BlockSpec((tm, tn), lambda i, j: (i, j)),
        out_shape=jax.ShapeDtypeStruct((m, n), out_dtype),
        compiler_params=_cparams("parallel", "parallel"),
        name="matmul",
    )(a, w)


def _mm_res_kernel(a_ref, w_ref, x_ref, gate_ref, g_ref, sh_ref, sc_ref, xo_ref, ao_ref, xn_scr, ssq_scr,
                   *, nn, modulate):
    j = pl.program_id(1)
    tn = x_ref.shape[1]
    w = w_ref[:, pl.ds(pl.multiple_of(j * tn, tn), tn)]
    xn = x_ref[...] + gate_ref[0] * jnp.dot(a_ref[...], w, preferred_element_type=F32)
    xo_ref[...] = xn
    xn_scr[j] = xn
    ssq = jnp.sum(xn * xn, axis=-1, keepdims=True)

    @pl.when(j == 0)
    def _():
        ssq_scr[...] = ssq

    @pl.when(j > 0)
    def _():
        ssq_scr[...] += ssq

    @pl.when(j == nn - 1)
    def _():
        inv = lax.rsqrt(ssq_scr[...] * (1.0 / (nn * tn)) + EPS)
        for c in range(nn):
            cols = slice(c * tn, (c + 1) * tn)
            y = xn_scr[c] * inv * g_ref[:, cols]
            if modulate:
                y = y * (1.0 + sc_ref[0][:, cols]) + sh_ref[0][:, cols]
            ao_ref[:, cols] = y.astype(ao_ref.dtype)


def _residual_tile_cols(k):
    return 1024 if k <= 4096 else 512


def _matmul_residual(a, w, layer, x, gate, g_next, shift, scale, rows_per_mod, modulate=True, a_dtype=BF16):
    m, k = a.shape
    d = w.shape[2]
    tn = _residual_tile_cols(k)
    nn = d // tn
    tm = min(512, m)

    def mod_row(i, j):
        return ((i * tm) // rows_per_mod, 0, 0)

    mod_spec = pl.BlockSpec((1, 1, d), mod_row)
    return pl.pallas_call(
        functools.partial(_mm_res_kernel, nn=nn, modulate=modulate),
        grid=(m // tm, nn),
        in_specs=[
            pl.BlockSpec((tm, k), lambda i, j: (i, 0)),
            pl.BlockSpec((None, k, d), lambda i, j: (layer, 0, 0), pipeline_mode=pl.Buffered(1)),
            pl.BlockSpec((tm, tn), lambda i, j: (i, j)),
            pl.BlockSpec((1, 1, tn), lambda i, j: ((i * tm) // rows_per_mod, 0, j)),
            pl.BlockSpec((1, d), lambda i, j: (0, 0)),
            mod_spec,
            mod_spec,
        ],
        out_specs=[
            pl.BlockSpec((tm, tn), lambda i, j: (i, j)),
            pl.BlockSpec((tm, d), lambda i, j: (i, 0)),
        ],
        out_shape=[
            jax.ShapeDtypeStruct((m, d), F32),
            jax.ShapeDtypeStruct((m, d), a_dtype),
        ],
        scratch_shapes=[pltpu.VMEM((nn, tm, tn), F32), pltpu.VMEM((tm, 1), F32)],
        compiler_params=_cparams("parallel", "arbitrary"),
        name="matmul_residual",
    )(a, w, x, gate, g_next.reshape(1, d), shift, scale)


def _proj_conv_kernel(*refs, rows, width, vertical, gated):
    refs = list(refs)
    a_ref, wg_ref = refs[:2]
    wv_ref = refs.pop(2) if gated else None
    cw_ref, cb_ref, o_ref, u_scr = refs[2:6]
    scratch = refs[6:]
    ul_scr, ur_scr = scratch[:2] if vertical else (None, None)
    v_scr = scratch[-1] if gated else None
    tn = o_ref.shape[1]
    pad = (u_scr.shape[0] - rows) // 2
    rb = min(PROJ_BLOCK_ROWS, rows)
    rc = min(EPILOGUE_ROWS, rb)
    zeros = jnp.zeros((pad, tn), F32)
    for buf in [u_scr] + ([ul_scr, ur_scr] if vertical else []):
        buf[pl.ds(0, pad), :] = zeros
        buf[pl.ds(pad + rows, pad), :] = zeros
    cw = cw_ref[...]
    bias = cb_ref[...]
    row = lax.broadcasted_iota(jnp.int32, (rc, tn), 0)

    def project(blk):
        a = a_ref[pl.ds(blk * rb, rb), :]
        u_scr[pl.ds(pad + blk * rb, rb), :] = jnp.dot(a, wg_ref[...], preferred_element_type=F32)
        if gated:
            v_scr[pl.ds(blk * rb, rb), :] = jnp.dot(a, wv_ref[...], preferred_element_type=F32)
        if vertical:
            col = row & (width - 1)
            for r in range(blk * rb // rc, (blk + 1) * rb // rc):
                chunk = u_scr[pl.ds(pad + r * rc, rc), :]
                ul_scr[pl.ds(pad + r * rc, rc), :] = jnp.where(col != 0, pltpu.roll(chunk, 1, 0), 0.0)
                ur_scr[pl.ds(pad + r * rc, rc), :] = jnp.where(col != width - 1, pltpu.roll(chunk, rc - 1, 0), 0.0)

    def convolve(blk):
        for r in range(blk * rb // rc, (blk + 1) * rb // rc):
            base = pad + r * rc
            if vertical:
                acc = bias
                for di in range(3):
                    off = base + (di - 1) * width
                    acc = acc + ul_scr[pl.ds(off, rc), :] * cw[3 * di:3 * di + 1, :]
                    acc = acc + u_scr[pl.ds(off, rc), :] * cw[3 * di + 1:3 * di + 2, :]
                    acc = acc + ur_scr[pl.ds(off, rc), :] * cw[3 * di + 2:3 * di + 3, :]
            else:
                col = (row + r * rc) & (width - 1)
                left = jnp.where(col != 0, u_scr[pl.ds(base - 1, rc), :], 0.0)
                right = jnp.where(col != width - 1, u_scr[pl.ds(base + 1, rc), :], 0.0)
                acc = left * cw[0:1, :] + u_scr[pl.ds(base, rc), :] * cw[1:2, :] + right * cw[2:3, :] + bias
            y = _silu(acc)
            if gated:
                y = y * v_scr[pl.ds(r * rc, rc), :]
            o_ref[pl.ds(r * rc, rc), :] = y.astype(o_ref.dtype)

    n_blocks = rows // rb
    for blk in range(n_blocks):
        project(blk)
        if blk >= 1:
            convolve(blk - 1)
    convolve(n_blocks - 1)


def _proj_conv(a, w, layer, conv_w, conv_b, *, width, vertical, n_out, val_offset=None, tn):
    m, k = a.shape
    rows = min(ROW_TILE, m)
    gated = val_offset is not None
    assert width & (width - 1) == 0 and rows % width == 0 and m % rows == 0
    pad = (width + SUBLANES) if vertical else SUBLANES
    taps = conv_w.shape[0]
    in_specs = [
        pl.BlockSpec((rows, k), lambda i, j: (i, 0)),
        pl.BlockSpec((None, k, tn), lambda i, j: (layer, 0, j)),
    ]
    operands = [a, w]
    padded = pltpu.VMEM((rows + 2 * pad, tn), F32)
    scratch = [padded] + ([padded, padded] if vertical else []) + ([pltpu.VMEM((rows, tn), F32)] if gated else [])
    if gated:
        voff = val_offset // tn
        in_specs.append(pl.BlockSpec((None, k, tn), lambda i, j: (layer, 0, j + voff)))
        operands.append(w)
    in_specs += [
        pl.BlockSpec((taps, tn), lambda i, j: (0, j)),
        pl.BlockSpec((1, tn), lambda i, j: (0, j)),
    ]
    operands += [conv_w, conv_b.reshape(1, n_out)]
    return pl.pallas_call(
        functools.partial(_proj_conv_kernel, rows=rows, width=width, vertical=vertical, gated=gated),
        grid=(m // rows, n_out // tn),
        in_specs=in_specs,
        out_specs=pl.BlockSpec((rows, tn), lambda i, j: (i, j)),
        out_shape=jax.ShapeDtypeStruct((m, n_out), BF16),
        scratch_shapes=scratch,
        compiler_params=_cparams("parallel", "parallel"),
        name="proj_conv",
    )(*operands)


def _dft_tables(n):
    k = jnp.arange(n, dtype=jnp.int32)
    idx = (k[:, None] * k[None, :]) % n
    ang = idx.astype(F32) * F32(2.0 * math.pi / n)
    s = F32(1.0 / math.sqrt(n))
    return jnp.cos(ang) * s, jnp.sin(ang) * s


def _chan_dft_kernel(a_ref, wc_ref, o_ref, *, groups):
    dg = wc_ref.shape[0]
    wc = wc_ref[...]
    for g in range(groups):
        cols = pl.ds(g * dg, dg)
        res = jnp.dot(a_ref[:, cols], wc, preferred_element_type=F32)
        o_ref[0, :, cols] = res[:, :dg].astype(o_ref.dtype)
        o_ref[1, :, cols] = res[:, dg:].astype(o_ref.dtype)


def _chan_dft(a, wc, nb, ln):
    d = a.shape[1]
    dg = wc.shape[0]
    tm = min(512, ln)
    mt = ln // tm
    return pl.pallas_call(
        functools.partial(_chan_dft_kernel, groups=d // dg),
        grid=(nb, mt),
        in_specs=[
            pl.BlockSpec((tm, d), lambda b, i: (b * mt + i, 0)),
            pl.BlockSpec((dg, 2 * dg), lambda b, i: (0, 0)),
        ],
        out_specs=pl.BlockSpec((None, 2, tm, d), lambda b, i: (b, 0, i, 0)),
        out_shape=jax.ShapeDtypeStruct((nb, 2, ln, d), BF16),
        compiler_params=_cparams("parallel", "parallel"),
        name="chan_dft",
    )(a, wc)


def _pos_dft(cs, y):
    nb, k2, d = y.shape
    ln = cs.shape[0]
    tm, tn = min(1024, ln), 1024
    return pl.pallas_call(
        _mm_kernel,
        grid=(nb, d // tn, ln // tm),
        in_specs=[
            pl.BlockSpec((tm, k2), lambda b, j, i: (i, 0)),
            pl.BlockSpec((None, k2, tn), lambda b, j, i: (b, 0, j)),
        ],
        out_specs=pl.BlockSpec((None, tm, tn), lambda b, j, i: (b, i, j)),
        out_shape=jax.ShapeDtypeStruct((nb, ln, d), BF16),
        compiler_params=_cparams("parallel", "parallel", "parallel"),
        name="pos_dft",
    )(cs, y)


def _fourier(a, nb, ln, wc, cs):
    d = a.shape[1]
    y = _chan_dft(a, wc, nb, ln)
    f = _pos_dft(cs, y.reshape(nb, 2 * ln, d))
    return f.reshape(nb * ln, d)


def _ssd_dt_kernel(a_ref, wt_ref, bias_ref, alog_ref, dtt_ref, acst_ref, *, heads, groups):
    ln = a_ref.shape[0]
    q = SSD_CHUNK
    r = heads // groups
    dtt = jax.nn.softplus(
        lax.dot_general(wt_ref[...], a_ref[...], (((1,), (1,)), ((), ())), preferred_element_type=F32)
        + bias_ref[...])
    dtat = dtt * (-jnp.exp(alog_ref[...]))
    ri = lax.broadcasted_iota(jnp.int32, (q, q), 0)
    ci = lax.broadcasted_iota(jnp.int32, (q, q), 1)
    lower = (ri >= ci).astype(F32)
    upper = (ri <= ci).astype(F32)
    fwd_row = lax.broadcasted_iota(jnp.int32, (2 * heads, q), 0) < heads
    for c in range(ln // q):
        xt = dtat[:, c * q:(c + 1) * q]
        prefix = jnp.dot(xt, upper, precision=_HIGHEST, preferred_element_type=F32)
        suffix = jnp.dot(xt, lower, precision=_HIGHEST, preferred_element_type=F32)
        acs = jnp.where(fwd_row, prefix, suffix)
        dtc = dtt[:, c * q:(c + 1) * q]
        for g in range(groups):
            for half in range(2):
                src = slice(half * heads + g * r, half * heads + (g + 1) * r)
                dst = pl.ds(half * r, r)
                acst_ref[g, c, dst, :] = acs[src, :]
                dtt_ref[g, c, dst, :] = dtc[src, :]


def _ssd_dt(a, w_dt_t, dt_bias, a_log, nb, ln):
    d = a.shape[1]
    h2 = w_dt_t.shape[0]
    groups = SSD_GROUPS
    nc = ln // SSD_CHUNK
    vec = pl.BlockSpec((h2, 1), lambda b: (0, 0))
    out = pl.BlockSpec((None, groups, nc, h2 // groups, SSD_CHUNK), lambda b: (b, 0, 0, 0, 0))
    shape = jax.ShapeDtypeStruct((nb, groups, nc, h2 // groups, SSD_CHUNK), F32)
    return pl.pallas_call(
        functools.partial(_ssd_dt_kernel, heads=h2 // 2, groups=groups),
        grid=(nb,),
        in_specs=[
            pl.BlockSpec((ln, d), lambda b: (b, 0)),
            pl.BlockSpec((h2, d), lambda b: (0, 0)),
            vec, vec,
        ],
        out_specs=[out, out],
        out_shape=[shape, shape],
        compiler_params=_cparams("parallel"),
        name="ssd_dt",
    )(a, w_dt_t, dt_bias.reshape(h2, 1), a_log.reshape(h2, 1))


def _ssd_scan_kernel(xs_ref, b_ref, c_ref, z_ref, acst_ref, dtt_ref, h0_ref, d0_ref, d1_ref,
                     ng_ref, o_ref, hfin_ref, yf_scr, h_scr, tok_scr, *, nc, heads_per_group):
    q = SSD_CHUNK
    r_heads = heads_per_group
    p = SSD_HEAD_DIM
    gw = r_heads * p
    expand = (lax.broadcasted_iota(jnp.int32, (2 * r_heads, 2 * gw), 1) // p
              == lax.broadcasted_iota(jnp.int32, (2 * r_heads, 2 * gw), 0)).astype(BF16)
    li = lax.broadcasted_iota(jnp.int32, (q, q), 0)
    si = lax.broadcasted_iota(jnp.int32, (q, q), 1)
    lane = lax.broadcasted_iota(jnp.int32, (q, gw), 1)
    even_head = (lane % (2 * p)) < p
    pad_rows = jnp.zeros((q - 3 * r_heads, q), F32)
    dskip = d0_ref[...] + d1_ref[...]
    norm_g = ng_ref[...]

    def chunk(c, direction):
        rows = pl.ds(pl.multiple_of(c * q, q), q)
        xs_bf = xs_ref[rows, :]
        bc = b_ref[rows, :]
        cc = c_ref[rows, :]
        src_rows = acst_ref[c] - jnp.log(dtt_ref[c])
        tok = tok_scr[direction, c]
        fac = tok[:, :2 * r_heads]
        fac_hi = fac.astype(BF16)
        fac_lo = (fac - fac_hi.astype(F32)).astype(BF16)
        fac_e = (jnp.dot(fac_hi, expand, preferred_element_type=F32)
                 + jnp.dot(fac_lo, expand, preferred_element_type=F32))
        ea_e = fac_e[:, :gw]
        wf_e = fac_e[:, gw:]
        h_decay = ea_e[q - 1:q, :] if direction == 0 else ea_e[0:1, :]
        causal = (li >= si) if direction == 0 else (li <= si)
        scores = lax.dot_general(cc, bc, (((1,), (1,)), ((), ())), preferred_element_type=F32)
        h = h_scr[...]
        y = jnp.dot(cc, h.astype(BF16), preferred_element_type=F32) * ea_e
        xs_even = jnp.where(even_head, xs_bf, jnp.zeros_like(xs_bf))
        xs_odd = jnp.where(even_head, jnp.zeros_like(xs_bf), xs_bf)
        pieces = []
        for pair in range(r_heads // 2):
            ms = []
            for sub in range(2):
                r = 2 * pair + sub
                col = direction * r_heads + r
                seg = tok[:, 2 * r_heads + r:2 * r_heads + r + 1] - src_rows[col:col + 1, :]
                decay = jnp.exp(jnp.where(causal, seg, -jnp.inf))
                ms.append((scores * decay).astype(BF16))
            cols = slice(pair * 2 * p, (pair + 1) * 2 * p)
            pieces.append(jnp.dot(jnp.concatenate(ms, axis=1),
                                  jnp.concatenate([xs_even[:, cols], xs_odd[:, cols]], axis=0),
                                  preferred_element_type=F32))
        y = y + jnp.concatenate(pieces, axis=1)
        w = xs_bf * wf_e.astype(BF16)
        h_scr[...] = h * h_decay + lax.dot_general(
            bc, w, (((0,), (0,)), ((), ())), preferred_element_type=F32)
        if direction == 0:
            yf_scr[rows, :] = y
        else:
            tot = yf_scr[rows, :] + y + dskip * xs_bf.astype(F32)
            gz = tot * _silu(z_ref[rows, :].astype(F32))
            gz = gz * lax.rsqrt(jnp.mean(gz * gz, axis=-1, keepdims=True) + EPS)
            o_ref[rows, :] = (gz * norm_g).astype(o_ref.dtype)

    def to_token_major(c, carry):
        acs_rows = acst_ref[c]
        dt_rows = dtt_ref[c]
        for direction in range(2):
            acs_d = acs_rows[direction * r_heads:(direction + 1) * r_heads, :]
            dt_d = dt_rows[direction * r_heads:(direction + 1) * r_heads, :]
            last = acs_d[:, q - 1:q] if direction == 0 else acs_d[:, 0:1]
            tok_scr[direction, c] = jnp.concatenate(
                [jnp.exp(acs_d), dt_d * jnp.exp(last - acs_d), acs_d, pad_rows], axis=0).T
        return carry

    lax.fori_loop(0, nc, to_token_major, 0, unroll=True)
    h_scr[...] = h0_ref[0]

    def fwd_body(c, carry):
        chunk(c, 0)
        return carry

    lax.fori_loop(0, nc, fwd_body, 0, unroll=min(SCAN_UNROLL, nc))
    hfin_ref[0] = h_scr[...]
    h_scr[...] = h0_ref[1]

    def bwd_body(i, carry):
        chunk(nc - 1 - i, 1)
        return carry

    lax.fori_loop(0, nc, bwd_body, 0, unroll=min(SCAN_UNROLL, nc))
    hfin_ref[1] = h_scr[...]


def _ssd_scan(xbc, z, acst4, dtt4, h0, d0e, d1e, norm_g, nb, ln):
    d_inner = z.shape[1]
    groups = SSD_GROUPS
    gw = d_inner // groups
    n = SSD_STATE
    r2 = acst4.shape[-2]
    nc = ln // SSD_CHUNK
    b_blk = d_inner // n
    c_blk = b_blk + groups
    chunk_g = pl.BlockSpec((None, None, nc, r2, SSD_CHUNK), lambda b, g: (b, g, 0, 0, 0))
    state = pl.BlockSpec((None, None, 2, n, gw), lambda b, g: (b, g, 0, 0, 0))
    vec = pl.BlockSpec((1, gw), lambda b, g: (0, g))
    return pl.pallas_call(
        functools.partial(_ssd_scan_kernel, nc=nc, heads_per_group=r2 // 2),
        grid=(nb, groups),
        in_specs=[
            pl.BlockSpec((ln, gw), lambda b, g: (b, g)),
            pl.BlockSpec((ln, n), lambda b, g: (b, b_blk + g)),
            pl.BlockSpec((ln, n), lambda b, g: (b, c_blk + g)),
            pl.BlockSpec((ln, gw), lambda b, g: (b, g)),
            chunk_g,
            chunk_g,
            state,
            vec, vec, vec,
        ],
        out_specs=[
            pl.BlockSpec((ln, gw), lambda b, g: (b, g)),
            state,
        ],
        out_shape=[
            jax.ShapeDtypeStruct((nb * ln, d_inner), BF16),
            jax.ShapeDtypeStruct(h0.shape, F32),
        ],
        scratch_shapes=[pltpu.VMEM((ln, gw), F32), pltpu.VMEM((n, gw), F32),
                        pltpu.VMEM((2, nc, SSD_CHUNK, SSD_CHUNK), F32)],
        compiler_params=_cparams("parallel", "parallel"),
        name="ssd_scan",
    )(xbc, xbc, xbc, z, acst4, dtt4, h0, d0e, d1e, norm_g)


def _ssd_mixer(a, nb, ln, weights, layer, h0):
    w_xbc, w_z, w_dt_t, conv_w, conv_b, dt_bias, a_log, d0e, d1e, norm_g = weights
    xbc = _proj_conv(a, w_xbc, layer, conv_w, conv_b, width=ln, vertical=False, n_out=w_xbc.shape[2], tn=512)
    z = _matmul(a, w_z, layer, BF16)
    dtt4, acst4 = _ssd_dt(a, w_dt_t, dt_bias, a_log, nb, ln)
    return _ssd_scan(xbc, z, acst4, dtt4, h0, d0e, d1e, norm_g, nb, ln)


def kernel(x, c, ctx, c_ctx, w_mod, b_mod, norm_mix_g, norm_ffn_g, four_w, ssd_w_in, ssd_conv_w, ssd_conv_b,
           ssd_dt_bias, ssd_a_log, ssd_d, ssd_norm_g, ssd_w_out, ffn_w_up, ffn_conv_w, ffn_conv_b, ffn_w_down,
           final_g):
    nb, ln, d = x.shape
    lc = ctx.shape[1]
    depth = w_mod.shape[0]
    d_ff = ffn_w_down.shape[1]
    d_inner = ssd_w_out.shape[1]
    heads = ssd_dt_bias.shape[2]
    gn = SSD_GROUPS * SSD_STATE
    t_lat, t_ctx = nb * ln, nb * lc

    mod_rows = -(-(nb + 1) // SUBLANES) * SUBLANES
    cond = jnp.zeros((mod_rows, d), F32).at[:nb].set(c).at[nb].set(c_ctx)
    mods = _modulation(cond, w_mod, b_mod)

    def lat_mod(i, j):
        return mods[i, :nb, j * d:(j + 1) * d].reshape(nb, 1, d)

    def ctx_mod(i, j):
        return mods[i, nb:nb + 1, j * d:(j + 1) * d].reshape(1, 1, d)

    four_bf = four_w.astype(BF16)
    w_down_bf = ffn_w_down.astype(BF16)
    w_out_bf = ssd_w_out.astype(BF16)
    w_up_bf = ffn_w_up.astype(BF16)
    xb = d_inner + gn
    state_cols = xb + 2 * heads
    w_xbc_bf = jnp.concatenate([ssd_w_in[:, :, :xb], ssd_w_in[:, :, state_cols:state_cols + gn]],
                               axis=2).astype(BF16)
    w_z_bf = ssd_w_in[:, :, state_cols + gn:].astype(BF16)
    w_dt_t_bf = ssd_w_in[:, :, xb:state_cols].transpose(0, 2, 1).astype(BF16)
    conv9 = ffn_conv_w.reshape(depth, 9, d_ff)

    dg = d // FOURIER_GROUPS
    cos_c, sin_c = _dft_tables(dg)
    wc = jnp.concatenate([cos_c, sin_c], axis=1).astype(BF16)
    cos_l, sin_l = _dft_tables(ln)
    cs_lat = jnp.concatenate([cos_l, -sin_l], axis=1).astype(BF16)
    cos_x, sin_x = _dft_tables(lc)
    cs_ctx = jnp.concatenate([cos_x, -sin_x], axis=1).astype(BF16)

    xl = x.reshape(t_lat, d)
    xc = ctx.reshape(t_ctx, d)
    a_lat = _normmod(xl, norm_mix_g[0], lat_mod(0, 0), lat_mod(0, 1), ln)
    a_ctx = _normmod(xc, norm_mix_g[0], ctx_mod(0, 0), ctx_mod(0, 1), t_ctx)
    out = None
    for i in range(depth):
        last = i == depth - 1
        is_ssd = i % 2 == 1
        j = i // 2
        if is_ssd:
            d0e = jnp.repeat(ssd_d[j, 0], SSD_HEAD_DIM).reshape(1, d_inner)
            d1e = jnp.repeat(ssd_d[j, 1], SSD_HEAD_DIM).reshape(1, d_inner)
            weights = (w_xbc_bf, w_z_bf, w_dt_t_bf[j], ssd_conv_w[j], ssd_conv_b[j], ssd_dt_bias[j], ssd_a_log[j],
                       d0e, d1e, ssd_norm_g[j].reshape(1, d_inner))
            zeros = jnp.zeros((nb, SSD_GROUPS, 2, SSD_STATE, d_inner // SSD_GROUPS), F32)
            mix_ctx, h_ctx = _ssd_mixer(a_ctx, nb, lc, weights, j, zeros)
            mix_lat, _ = _ssd_mixer(a_lat, nb, ln, weights, j, h_ctx)
            w_mix = w_out_bf
        else:
            mix_lat = _fourier(a_lat, nb, ln, wc, cs_lat)
            mix_ctx = None if last else _fourier(a_ctx, nb, lc, wc, cs_ctx)
            w_mix = four_bf
        xl, b_lat = _matmul_residual(mix_lat, w_mix, j, xl, lat_mod(i, 2), norm_ffn_g[i], lat_mod(i, 3),
                                     lat_mod(i, 4), ln)
        act = _proj_conv(b_lat, w_up_bf, i, conv9[i], ffn_conv_b[i], width=GRID_W, vertical=True,
                         n_out=d_ff, val_offset=d_ff, tn=256)
        if last:
            zero_mod = jnp.zeros((nb, 1, d), F32)
            _, out = _matmul_residual(act, w_down_bf, i, xl, lat_mod(i, 5), final_g, zero_mod, zero_mod, ln,
                                      modulate=False, a_dtype=F32)
        else:
            xl, a_lat = _matmul_residual(act, w_down_bf, i, xl, lat_mod(i, 5), norm_mix_g[i + 1],
                                         lat_mod(i + 1, 0), lat_mod(i + 1, 1), ln)
            xc, b_ctx = _matmul_residual(mix_ctx, w_mix, j, xc, ctx_mod(i, 2), norm_ffn_g[i], ctx_mod(i, 3),
                                         ctx_mod(i, 4), t_ctx)
            act_c = _proj_conv(b_ctx, w_up_bf, i, ffn_conv_w[i, 1], ffn_conv_b[i], width=lc, vertical=False,
                               n_out=d_ff, val_offset=d_ff, tn=256)
            xc, a_ctx = _matmul_residual(act_c, w_down_bf, i, xc, ctx_mod(i, 5), norm_mix_g[i + 1],
                                         ctx_mod(i + 1, 0), ctx_mod(i + 1, 1), t_ctx)
    return out.reshape(nb, ln, d)
```

```python
import functools
import math

import jax
import jax.numpy as jnp
from jax import lax
from jax.experimental import pallas as pl
from jax.experimental.pallas import tpu as pltpu

F32 = jnp.float32
BF16 = jnp.bfloat16

EPS = 1e-6
GRID_W = 64
FOURIER_GROUPS = 8
SSD_HEAD_DIM = 64
SSD_GROUPS = 8
SSD_STATE = 128
SSD_CHUNK = 128
SCAN_UNROLL = 8

LANES = 128
SUBLANES = 8
VMEM_LIMIT_BYTES = 56 * 1024 * 1024
ROW_TILE = 2048
EPILOGUE_ROWS = 256
PROJ_BLOCK_ROWS = 512

_HIGHEST = lax.Precision.HIGHEST


def _cparams(*sem, flags=None):
    return pltpu.CompilerParams(dimension_semantics=sem, vmem_limit_bytes=VMEM_LIMIT_BYTES, flags=flags)


def _norm_rows(x, g, shift, scale, modulate):
    ms = jnp.mean(x * x, axis=-1, keepdims=True)
    y = x * lax.rsqrt(ms + EPS) * g
    if modulate:
        y = y * (1.0 + scale) + shift
    return y


def _silu(v):
    return v * jax.nn.sigmoid(v)


def _mod_kernel(s_ref, w_ref, b_ref, o_ref):
    s = _silu(s_ref[...]).astype(BF16)
    w = w_ref[0].astype(BF16)
    o_ref[0] = jnp.dot(s, w, preferred_element_type=F32) + b_ref[0]


def _modulation(cond, w_mod, b_mod):
    depth, d, n = w_mod.shape
    rows = cond.shape[0]
    tn = 1024
    return pl.pallas_call(
        _mod_kernel,
        grid=(depth, n // tn),
        in_specs=[
            pl.BlockSpec((rows, d), lambda i, j: (0, 0)),
            pl.BlockSpec((1, d, tn), lambda i, j: (i, 0, j)),
            pl.BlockSpec((1, 1, tn), lambda i, j: (i, 0, j)),
        ],
        out_specs=pl.BlockSpec((1, rows, tn), lambda i, j: (i, 0, j)),
        out_shape=jax.ShapeDtypeStruct((depth, rows, n), F32),
        compiler_params=_cparams("parallel", "parallel"),
        name="modulation",
    )(cond, w_mod, b_mod.reshape(depth, 1, n))


def _normmod_kernel(x_ref, g_ref, sh_ref, sc_ref, o_ref):
    o_ref[...] = _norm_rows(x_ref[...], g_ref[...], sh_ref[0], sc_ref[0], True).astype(o_ref.dtype)


def _normmod(x, g, shift, scale, rows_per_mod):
    m, d = x.shape
    tm = 512
    mod_spec = pl.BlockSpec((1, 1, d), lambda i: ((i * tm) // rows_per_mod, 0, 0))
    return pl.pallas_call(
        _normmod_kernel,
        grid=(m // tm,),
        in_specs=[
            pl.BlockSpec((tm, d), lambda i: (i, 0)),
            pl.BlockSpec((1, d), lambda i: (0, 0)),
            mod_spec,
            mod_spec,
        ],
        out_specs=pl.BlockSpec((tm, d), lambda i: (i, 0)),
        out_shape=jax.ShapeDtypeStruct((m, d), BF16),
        compiler_params=_cparams("parallel"),
        name="normmod",
    )(x, g.reshape(1, d), shift, scale)


def _mm_kernel(a_ref, w_ref, o_ref):
    o_ref[...] = jnp.dot(a_ref[...], w_ref[...], preferred_element_type=F32).astype(o_ref.dtype)


def _matmul(a, w, layer, out_dtype, tm=1024, tn=1024):
    m, k = a.shape
    n = w.shape[2]
    tm, tn = min(tm, m), min(tn, n)
    return pl.pallas_call(
        _mm_kernel,
        grid=(m // tm, n // tn),
        in_specs=[
            pl.BlockSpec((tm, k), lambda i, j: (i, 0)),
            pl.BlockSpec((None, k, tn), lambda i, j: (layer, 0, j)),
        ],
        out_specs=pl.BlockSpec((tm, tn), lambda i, j: (i, j)),
        out_shape=jax.ShapeDtypeStruct((m, n), out_dtype),
        compiler_params=_cparams("parallel", "parallel"),
        name="matmul",
    )(a, w)


def _mm_res_kernel(a_ref, w_ref, x_ref, gate_ref, g_ref, sh_ref, sc_ref, xo_ref, ao_ref, xn_scr, ssq_scr,
                   *, nn, modulate):
    j = pl.program_id(1)
    tn = x_ref.shape[1]
    w = w_ref[:, pl.ds(pl.multiple_of(j * tn, tn), tn)]
    xn = x_ref[...] + gate_ref[0] * jnp.dot(a_ref[...], w, preferred_element_type=F32)
    xo_ref[...] = xn
    xn_scr[j] = xn
    ssq = jnp.sum(xn * xn, axis=-1, keepdims=True)

    @pl.when(j == 0)
    def _():
        ssq_scr[...] = ssq

    @pl.when(j > 0)
    def _():
        ssq_scr[...] += ssq

    @pl.when(j == nn - 1)
    def _():
        inv = lax.rsqrt(ssq_scr[...] * (1.0 / (nn * tn)) + EPS)
        for c in range(nn):
            cols = slice(c * tn, (c + 1) * tn)
            y = xn_scr[c] * inv * g_ref[:, cols]
            if modulate:
                y = y * (1.0 + sc_ref[0][:, cols]) + sh_ref[0][:, cols]
            ao_ref[:, cols] = y.astype(ao_ref.dtype)


def _residual_tile_cols(k):
    return 1024 if k <= 4096 else 512


def _matmul_residual(a, w, layer, x, gate, g_next, shift, scale, rows_per_mod, modulate=True, a_dtype=BF16):
    m, k = a.shape
    d = w.shape[2]
    tn = _residual_tile_cols(k)
    nn = d // tn
    tm = min(512, m)

    def mod_row(i, j):
        return ((i * tm) // rows_per_mod, 0, 0)

    mod_spec = pl.BlockSpec((1, 1, d), mod_row)
    return pl.pallas_call(
        functools.partial(_mm_res_kernel, nn=nn, modulate=modulate),
        grid=(m // tm, nn),
        in_specs=[
            pl.BlockSpec((tm, k), lambda i, j: (i, 0)),
            pl.BlockSpec((None, k, d), lambda i, j: (layer, 0, 0), pipeline_mode=pl.Buffered(1)),
            pl.BlockSpec((tm, tn), lambda i, j: (i, j)),
            pl.BlockSpec((1, 1, tn), lambda i, j: ((i * tm) // rows_per_mod, 0, j)),
            pl.BlockSpec((1, d), lambda i, j: (0, 0)),
            mod_spec,
            mod_spec,
        ],
        out_specs=[
            pl.BlockSpec((tm, tn), lambda i, j: (i, j)),
            pl.BlockSpec((tm, d), lambda i, j: (i, 0)),
        ],
        out_shape=[
            jax.ShapeDtypeStruct((m, d), F32),
            jax.ShapeDtypeStruct((m, d), a_dtype),
        ],
        scratch_shapes=[pltpu.VMEM((nn, tm, tn), F32), pltpu.VMEM((tm, 1), F32)],
        compiler_params=_cparams("parallel", "arbitrary"),
        name="matmul_residual",
    )(a, w, x, gate, g_next.reshape(1, d), shift, scale)


def _proj_conv_kernel(*refs, rows, width, vertical, gated):
    refs = list(refs)
    a_ref, wg_ref = refs[:2]
    wv_ref = refs.pop(2) if gated else None
    cw_ref, cb_ref, o_ref, u_scr = refs[2:6]
    scratch = refs[6:]
    if vertical:
        (ul_scr, ur_scr), scratch = scratch[:2], scratch[2:]
    if gated:
        v_scr, scratch = scratch[0], scratch[1:]
    if wg_ref.dtype != BF16:
        scratch[0][...] = wg_ref[...].astype(BF16)
        wg_ref = scratch[0]
        if gated:
            scratch[1][...] = wv_ref[...].astype(BF16)
            wv_ref = scratch[1]
    tn = o_ref.shape[1]
    pad = (u_scr.shape[0] - rows) // 2
    rb = min(PROJ_BLOCK_ROWS, rows)
    rc = min(EPILOGUE_ROWS, rb)
    zeros = jnp.zeros((pad, tn), F32)
    for buf in [u_scr] + ([ul_scr, ur_scr] if vertical else []):
        buf[pl.ds(0, pad), :] = zeros
        buf[pl.ds(pad + rows, pad), :] = zeros
    cw = cw_ref[...]
    bias = cb_ref[...]
    row = lax.broadcasted_iota(jnp.int32, (rc, tn), 0)

    def project(blk):
        a = a_ref[pl.ds(blk * rb, rb), :]
        u_scr[pl.ds(pad + blk * rb, rb), :] = jnp.dot(a, wg_ref[...], preferred_element_type=F32)
        if gated:
            v_scr[pl.ds(blk * rb, rb), :] = jnp.dot(a, wv_ref[...], preferred_element_type=F32)
        if vertical:
            col = row & (width - 1)
            for r in range(blk * rb // rc, (blk + 1) * rb // rc):
                chunk = u_scr[pl.ds(pad + r * rc, rc), :]
                ul_scr[pl.ds(pad + r * rc, rc), :] = jnp.where(col != 0, pltpu.roll(chunk, 1, 0), 0.0)
                ur_scr[pl.ds(pad + r * rc, rc), :] = jnp.where(col != width - 1, pltpu.roll(chunk, rc - 1, 0), 0.0)

    def convolve(blk):
        for r in range(blk * rb // rc, (blk + 1) * rb // rc):
            base = pad + r * rc
            if vertical:
                acc = bias
                for di in range(3):
                    off = base + (di - 1) * width
                    acc = acc + ul_scr[pl.ds(off, rc), :] * cw[3 * di:3 * di + 1, :]
                    acc = acc + u_scr[pl.ds(off, rc), :] * cw[3 * di + 1:3 * di + 2, :]
                    acc = acc + ur_scr[pl.ds(off, rc), :] * cw[3 * di + 2:3 * di + 3, :]
            else:
                col = (row + r * rc) & (width - 1)
                left = jnp.where(col != 0, u_scr[pl.ds(base - 1, rc), :], 0.0)
                right = jnp.where(col != width - 1, u_scr[pl.ds(base + 1, rc), :], 0.0)
                acc = left * cw[0:1, :] + u_scr[pl.ds(base, rc), :] * cw[1:2, :] + right * cw[2:3, :] + bias
            y = _silu(acc)
            if gated:
                y = y * v_scr[pl.ds(r * rc, rc), :]
            o_ref[pl.ds(r * rc, rc), :] = y.astype(o_ref.dtype)

    n_blocks = rows // rb
    for blk in range(n_blocks):
        project(blk)
        if blk >= 1:
            convolve(blk - 1)
    convolve(n_blocks - 1)


def _proj_conv(a, w, layer, conv_w, conv_b, *, width, vertical, n_out, val_offset=None, tn):
    m, k = a.shape
    rows = min(ROW_TILE, m)
    gated = val_offset is not None
    assert width & (width - 1) == 0 and rows % width == 0 and m % rows == 0
    pad = (width + SUBLANES) if vertical else SUBLANES
    taps = conv_w.shape[0]
    in_specs = [
        pl.BlockSpec((rows, k), lambda i, j: (i, 0)),
        pl.BlockSpec((None, k, tn), lambda i, j: (layer, 0, j)),
    ]
    operands = [a, w]
    padded = pltpu.VMEM((rows + 2 * pad, tn), F32)
    scratch = [padded] + ([padded, padded] if vertical else []) + ([pltpu.VMEM((rows, tn), F32)] if gated else [])
    if w.dtype != BF16:
        scratch += [pltpu.VMEM((k, tn), BF16)] * (2 if gated else 1)
    if gated:
        voff = val_offset // tn
        in_specs.append(pl.BlockSpec((None, k, tn), lambda i, j: (layer, 0, j + voff)))
        operands.append(w)
    in_specs += [
        pl.BlockSpec((taps, tn), lambda i, j: (0, j)),
        pl.BlockSpec((1, tn), lambda i, j: (0, j)),
    ]
    operands += [conv_w, conv_b.reshape(1, n_out)]
    return pl.pallas_call(
        functools.partial(_proj_conv_kernel, rows=rows, width=width, vertical=vertical, gated=gated),
        grid=(m // rows, n_out // tn),
        in_specs=in_specs,
        out_specs=pl.BlockSpec((rows, tn), lambda i, j: (i, j)),
        out_shape=jax.ShapeDtypeStruct((m, n_out), BF16),
        scratch_shapes=scratch,
        compiler_params=_cparams("parallel", "parallel"),
        name="proj_conv",
    )(*operands)


def _dft_tables(n):
    s = 64
    k = jnp.arange(n, dtype=jnp.int32)[:, None]

    def narrow(cols):
        ang = ((k * cols[None, :]) % n).astype(F32) * F32(2.0 * math.pi / n)
        return jnp.cos(ang), jnp.sin(ang)

    ca, sa = narrow(jnp.arange(n // s, dtype=jnp.int32) * s)
    cb, sb = narrow(jnp.arange(s, dtype=jnp.int32))
    scale = F32(1.0 / math.sqrt(n))
    cos = (ca[:, :, None] * cb[:, None, :] - sa[:, :, None] * sb[:, None, :]).reshape(n, n) * scale
    sin = (sa[:, :, None] * cb[:, None, :] + ca[:, :, None] * sb[:, None, :]).reshape(n, n) * scale
    return cos, sin


def _chan_dft_kernel(a_ref, wc_ref, o_ref, *, groups):
    dg = wc_ref.shape[0]
    wc = wc_ref[...]
    for g in range(groups):
        cols = pl.ds(g * dg, dg)
        res = jnp.dot(a_ref[:, cols], wc, preferred_element_type=F32)
        o_ref[0, :, cols] = res[:, :dg].astype(o_ref.dtype)
        o_ref[1, :, cols] = res[:, dg:].astype(o_ref.dtype)


def _chan_dft(a, wc, nb, ln):
    d = a.shape[1]
    dg = wc.shape[0]
    tm = min(512, ln)
    mt = ln // tm
    return pl.pallas_call(
        functools.partial(_chan_dft_kernel, groups=d // dg),
        grid=(nb, mt),
        in_specs=[
            pl.BlockSpec((tm, d), lambda b, i: (b * mt + i, 0)),
            pl.BlockSpec((dg, 2 * dg), lambda b, i: (0, 0)),
        ],
        out_specs=pl.BlockSpec((None, 2, tm, d), lambda b, i: (b, 0, i, 0)),
        out_shape=jax.ShapeDtypeStruct((nb, 2, ln, d), BF16),
        compiler_params=_cparams("parallel", "parallel"),
        name="chan_dft",
    )(a, wc)


def _pos_dft(cs, y):
    nb, k2, d = y.shape
    ln = cs.shape[0]
    tm, tn = min(1024, ln), 1024
    return pl.pallas_call(
        _mm_kernel,
        grid=(nb, d // tn, ln // tm),
        in_specs=[
            pl.BlockSpec((tm, k2), lambda b, j, i: (i, 0)),
            pl.BlockSpec((None, k2, tn), lambda b, j, i: (b, 0, j)),
        ],
        out_specs=pl.BlockSpec((None, tm, tn), lambda b, j, i: (b, i, j)),
        out_shape=jax.ShapeDtypeStruct((nb, ln, d), BF16),
        compiler_params=_cparams("parallel", "parallel", "parallel"),
        name="pos_dft",
    )(cs, y)


def _fourier(a, nb, ln, wc, cs):
    d = a.shape[1]
    y = _chan_dft(a, wc, nb, ln)
    f = _pos_dft(cs, y.reshape(nb, 2 * ln, d))
    return f.reshape(nb * ln, d)


def _ssd_dt_kernel(a_ref, wt_ref, bias_ref, alog_ref, dtt_ref, acst_ref, *, heads, groups):
    ln = a_ref.shape[0]
    q = SSD_CHUNK
    r = heads // groups
    dtt = jax.nn.softplus(
        lax.dot_general(wt_ref[...], a_ref[...], (((1,), (1,)), ((), ())), preferred_element_type=F32)
        + bias_ref[...])
    dtat = dtt * (-jnp.exp(alog_ref[...]))
    ri = lax.broadcasted_iota(jnp.int32, (q, q), 0)
    ci = lax.broadcasted_iota(jnp.int32, (q, q), 1)
    lower = (ri >= ci).astype(F32)
    upper = (ri <= ci).astype(F32)
    fwd_row = lax.broadcasted_iota(jnp.int32, (2 * heads, q), 0) < heads
    for c in range(ln // q):
        xt = dtat[:, c * q:(c + 1) * q]
        prefix = jnp.dot(xt, upper, precision=_HIGHEST, preferred_element_type=F32)
        suffix = jnp.dot(xt, lower, precision=_HIGHEST, preferred_element_type=F32)
        acs = jnp.where(fwd_row, prefix, suffix)
        dtc = dtt[:, c * q:(c + 1) * q]
        for g in range(groups):
            for half in range(2):
                src = slice(half * heads + g * r, half * heads + (g + 1) * r)
                dst = pl.ds(half * r, r)
                acst_ref[g, c, dst, :] = acs[src, :]
                dtt_ref[g, c, dst, :] = dtc[src, :]


def _ssd_dt(a, w_dt_t, dt_bias, a_log, nb, ln):
    d = a.shape[1]
    h2 = w_dt_t.shape[0]
    groups = SSD_GROUPS
    nc = ln // SSD_CHUNK
    vec = pl.BlockSpec((h2, 1), lambda b: (0, 0))
    out = pl.BlockSpec((None, groups, nc, h2 // groups, SSD_CHUNK), lambda b: (b, 0, 0, 0, 0))
    shape = jax.ShapeDtypeStruct((nb, groups, nc, h2 // groups, SSD_CHUNK), F32)
    return pl.pallas_call(
        functools.partial(_ssd_dt_kernel, heads=h2 // 2, groups=groups),
        grid=(nb,),
        in_specs=[
            pl.BlockSpec((ln, d), lambda b: (b, 0)),
            pl.BlockSpec((h2, d), lambda b: (0, 0)),
            vec, vec,
        ],
        out_specs=[out, out],
        out_shape=[shape, shape],
        compiler_params=_cparams("parallel"),
        name="ssd_dt",
    )(a, w_dt_t, dt_bias.reshape(h2, 1), a_log.reshape(h2, 1))


def _ssd_scan_kernel(*refs, nc, heads_per_group, emit):
    if emit:
        (xs_ref, b_ref, c_ref, z_ref, acst_ref, dtt_ref, h0_ref, d0_ref, d1_ref, ng_ref, o_ref, hfin_ref,
         yf_scr, h_scr, tok_scr) = refs
    else:
        xs_ref, b_ref, acst_ref, dtt_ref, h0_ref, hfin_ref, h_scr, tok_scr = refs
    q = SSD_CHUNK
    r_heads = heads_per_group
    p = SSD_HEAD_DIM
    gw = r_heads * p
    expand = (lax.broadcasted_iota(jnp.int32, (2 * r_heads, 2 * gw), 1) // p
              == lax.broadcasted_iota(jnp.int32, (2 * r_heads, 2 * gw), 0)).astype(BF16)
    li = lax.broadcasted_iota(jnp.int32, (q, q), 0)
    si = lax.broadcasted_iota(jnp.int32, (q, q), 1)
    lane = lax.broadcasted_iota(jnp.int32, (q, gw), 1)
    even_head = (lane % (2 * p)) < p
    pad_rows = jnp.zeros((q - 3 * r_heads, q), F32)
    if emit:
        dskip = d0_ref[...] + d1_ref[...]
        norm_g = ng_ref[...]

    def chunk(c, direction):
        rows = pl.ds(pl.multiple_of(c * q, q), q)
        xs_bf = xs_ref[rows, :]
        bc = b_ref[rows, :]
        tok = tok_scr[direction, c]
        fac = tok[:, :2 * r_heads]
        fac_hi = fac.astype(BF16)
        fac_lo = (fac - fac_hi.astype(F32)).astype(BF16)
        fac_e = (jnp.dot(fac_hi, expand, preferred_element_type=F32)
                 + jnp.dot(fac_lo, expand, preferred_element_type=F32))
        ea_e = fac_e[:, :gw]
        wf_e = fac_e[:, gw:]
        h_decay = ea_e[q - 1:q, :] if direction == 0 else ea_e[0:1, :]
        h = h_scr[...]
        w = xs_bf * wf_e.astype(BF16)
        h_scr[...] = h * h_decay + lax.dot_general(
            bc, w, (((0,), (0,)), ((), ())), preferred_element_type=F32)
        if not emit:
            return
        cc = c_ref[rows, :]
        src_rows = acst_ref[c] - jnp.log(dtt_ref[c])
        causal = (li >= si) if direction == 0 else (li <= si)
        scores = lax.dot_general(cc, bc, (((1,), (1,)), ((), ())), preferred_element_type=F32)
        y = jnp.dot(cc, h.astype(BF16), preferred_element_type=F32) * ea_e
        xs_even = jnp.where(even_head, xs_bf, jnp.zeros_like(xs_bf))
        xs_odd = jnp.where(even_head, jnp.zeros_like(xs_bf), xs_bf)
        pieces = []
        for pair in range(r_heads // 2):
            ms = []
            for sub in range(2):
                r = 2 * pair + sub
                col = direction * r_heads + r
                seg = tok[:, 2 * r_heads + r:2 * r_heads + r + 1] - src_rows[col:col + 1, :]
                decay = jnp.exp(jnp.where(causal, seg, -jnp.inf))
                ms.append((scores * decay).astype(BF16))
            cols = slice(pair * 2 * p, (pair + 1) * 2 * p)
            pieces.append(jnp.dot(jnp.concatenate(ms, axis=1),
                                  jnp.concatenate([xs_even[:, cols], xs_odd[:, cols]], axis=0),
                                  preferred_element_type=F32))
        y = y + jnp.concatenate(pieces, axis=1)
        if direction == 0:
            yf_scr[rows, :] = y
        else:
            tot = yf_scr[rows, :] + y + dskip * xs_bf.astype(F32)
            gz = tot * _silu(z_ref[rows, :].astype(F32))
            gz = gz * lax.rsqrt(jnp.mean(gz * gz, axis=-1, keepdims=True) + EPS)
            o_ref[rows, :] = (gz * norm_g).astype(o_ref.dtype)

    def to_token_major(c, carry):
        acs_rows = acst_ref[c]
        dt_rows = dtt_ref[c]
        for direction in range(2):
            acs_d = acs_rows[direction * r_heads:(direction + 1) * r_heads, :]
            dt_d = dt_rows[direction * r_heads:(direction + 1) * r_heads, :]
            last = acs_d[:, q - 1:q] if direction == 0 else acs_d[:, 0:1]
            tok_scr[direction, c] = jnp.concatenate(
                [jnp.exp(acs_d), dt_d * jnp.exp(last - acs_d), acs_d, pad_rows], axis=0).T
        return carry

    lax.fori_loop(0, nc, to_token_major, 0, unroll=True)
    h_scr[...] = h0_ref[0]

    def fwd_body(c, carry):
        chunk(c, 0)
        return carry

    lax.fori_loop(0, nc, fwd_body, 0, unroll=min(SCAN_UNROLL, nc))
    hfin_ref[0] = h_scr[...]
    h_scr[...] = h0_ref[1]

    def bwd_body(i, carry):
        chunk(nc - 1 - i, 1)
        return carry

    lax.fori_loop(0, nc, bwd_body, 0, unroll=min(SCAN_UNROLL, nc))
    hfin_ref[1] = h_scr[...]


def _ssd_scan(xbc, z, acst4, dtt4, h0, d0e, d1e, norm_g, nb, ln):
    emit = z is not None
    groups = SSD_GROUPS
    n = SSD_STATE
    gw = h0.shape[-1]
    d_inner = gw * groups
    r2 = acst4.shape[-2]
    nc = ln // SSD_CHUNK
    b_blk = d_inner // n
    c_blk = b_blk + groups
    chunk_g = pl.BlockSpec((None, None, nc, r2, SSD_CHUNK), lambda b, g: (b, g, 0, 0, 0))
    state = pl.BlockSpec((None, None, 2, n, gw), lambda b, g: (b, g, 0, 0, 0))
    vec = pl.BlockSpec((1, gw), lambda b, g: (0, g))
    seq_x = pl.BlockSpec((ln, gw), lambda b, g: (b, g))
    seq_b = pl.BlockSpec((ln, n), lambda b, g: (b, b_blk + g))
    seq_c = pl.BlockSpec((ln, n), lambda b, g: (b, c_blk + g))
    state_shape = jax.ShapeDtypeStruct(h0.shape, F32)
    scratch = [pltpu.VMEM((n, gw), F32), pltpu.VMEM((2, nc, SSD_CHUNK, SSD_CHUNK), F32)]
    if emit:
        in_specs = [seq_x, seq_b, seq_c, seq_x, chunk_g, chunk_g, state, vec, vec, vec]
        operands = (xbc, xbc, xbc, z, acst4, dtt4, h0, d0e, d1e, norm_g)
        out_specs = [seq_x, state]
        out_shape = [jax.ShapeDtypeStruct((nb * ln, d_inner), BF16), state_shape]
        scratch = [pltpu.VMEM((ln, gw), F32)] + scratch
    else:
        in_specs = [seq_x, seq_b, chunk_g, chunk_g, state]
        operands = (xbc, xbc, acst4, dtt4, h0)
        out_specs = [state]
        out_shape = [state_shape]
    res = pl.pallas_call(
        functools.partial(_ssd_scan_kernel, nc=nc, heads_per_group=r2 // 2, emit=emit),
        grid=(nb, groups),
        in_specs=in_specs,
        out_specs=out_specs,
        out_shape=out_shape,
        scratch_shapes=scratch,
        compiler_params=_cparams("parallel", "parallel"),
        name="ssd_scan",
    )(*operands)
    return res if emit else (None, res[0])


def _ssd_mixer(a, nb, ln, weights, layer, h0, emit=True):
    w_xbc, w_z, w_dt_t, conv_w, conv_b, dt_bias, a_log, d0e, d1e, norm_g = weights
    d_inner = w_z.shape[2]
    n_out = w_xbc.shape[2] if emit else d_inner + SSD_GROUPS * SSD_STATE
    xbc = _proj_conv(a, w_xbc, layer, conv_w[:, :n_out], conv_b[:n_out], width=ln, vertical=False, n_out=n_out,
                     tn=512)
    z = _matmul(a, w_z, layer, BF16) if emit else None
    dtt4, acst4 = _ssd_dt(a, w_dt_t, dt_bias, a_log, nb, ln)
    return _ssd_scan(xbc, z, acst4, dtt4, h0, d0e, d1e, norm_g, nb, ln)


def kernel(x, c, ctx, c_ctx, w_mod, b_mod, norm_mix_g, norm_ffn_g, four_w, ssd_w_in, ssd_conv_w, ssd_conv_b,
           ssd_dt_bias, ssd_a_log, ssd_d, ssd_norm_g, ssd_w_out, ffn_w_up, ffn_conv_w, ffn_conv_b, ffn_w_down,
           final_g):
    nb, ln, d = x.shape
    lc = ctx.shape[1]
    depth = w_mod.shape[0]
    d_ff = ffn_w_down.shape[1]
    d_inner = ssd_w_out.shape[1]
    heads = ssd_dt_bias.shape[2]
    gn = SSD_GROUPS * SSD_STATE
    t_lat, t_ctx = nb * ln, nb * lc

    mod_rows = -(-(nb + 1) // SUBLANES) * SUBLANES
    cond = jnp.zeros((mod_rows, d), F32).at[:nb].set(c).at[nb].set(c_ctx)
    mods = _modulation(cond, w_mod, b_mod)

    def lat_mod(i, j):
        return mods[i, :nb, j * d:(j + 1) * d].reshape(nb, 1, d)

    def ctx_mod(i, j):
        return mods[i, nb:nb + 1, j * d:(j + 1) * d].reshape(1, 1, d)

    four_bf = four_w.astype(BF16)
    w_down_bf = ffn_w_down.astype(BF16)
    w_out_bf = ssd_w_out.astype(BF16)
    xb = d_inner + gn
    state_cols = xb + 2 * heads
    w_xbc_bf = jnp.concatenate([ssd_w_in[:, :, :xb], ssd_w_in[:, :, state_cols:state_cols + gn]],
                               axis=2).astype(BF16)
    w_z_bf = ssd_w_in[:, :, state_cols + gn:].astype(BF16)
    w_dt_t_bf = ssd_w_in[:, :, xb:state_cols].transpose(0, 2, 1).astype(BF16)
    conv9 = ffn_conv_w.reshape(depth, 9, d_ff)

    dg = d // FOURIER_GROUPS
    cos_c, sin_c = _dft_tables(dg)
    wc = jnp.concatenate([cos_c, sin_c], axis=1).astype(BF16)
    cos_l, sin_l = _dft_tables(ln)
    cs_lat = jnp.concatenate([cos_l, -sin_l], axis=1).astype(BF16)
    cos_x, sin_x = _dft_tables(lc)
    cs_ctx = jnp.concatenate([cos_x, -sin_x], axis=1).astype(BF16)

    xl = x.reshape(t_lat, d)
    xc = ctx.reshape(t_ctx, d)
    a_lat = _normmod(xl, norm_mix_g[0], lat_mod(0, 0), lat_mod(0, 1), ln)
    a_ctx = _normmod(xc, norm_mix_g[0], ctx_mod(0, 0), ctx_mod(0, 1), t_ctx)
    out = None
    for i in range(depth):
        last = i == depth - 1
        is_ssd = i % 2 == 1
        j = i // 2
        if is_ssd:
            d0e = jnp.repeat(ssd_d[j, 0], SSD_HEAD_DIM).reshape(1, d_inner)
            d1e = jnp.repeat(ssd_d[j, 1], SSD_HEAD_DIM).reshape(1, d_inner)
            weights = (w_xbc_bf, w_z_bf, w_dt_t_bf[j], ssd_conv_w[j], ssd_conv_b[j], ssd_dt_bias[j], ssd_a_log[j],
                       d0e, d1e, ssd_norm_g[j].reshape(1, d_inner))
            zeros = jnp.zeros((nb, SSD_GROUPS, 2, SSD_STATE, d_inner // SSD_GROUPS), F32)
            mix_ctx, h_ctx = _ssd_mixer(a_ctx, nb, lc, weights, j, zeros, emit=not last)
            mix_lat, _ = _ssd_mixer(a_lat, nb, ln, weights, j, h_ctx)
            w_mix = w_out_bf
        else:
            mix_lat = _fourier(a_lat, nb, ln, wc, cs_lat)
            mix_ctx = None if last else _fourier(a_ctx, nb, lc, wc, cs_ctx)
            w_mix = four_bf
        xl, b_lat = _matmul_residual(mix_lat, w_mix, j, xl, lat_mod(i, 2), norm_ffn_g[i], lat_mod(i, 3),
                                     lat_mod(i, 4), ln)
        act = _proj_conv(b_lat, ffn_w_up, i, conv9[i], ffn_conv_b[i], width=GRID_W, vertical=True,
                         n_out=d_ff, val_offset=d_ff, tn=256)
        if last:
            zero_mod = jnp.zeros((nb, 1, d), F32)
            _, out = _matmul_residual(act, w_down_bf, i, xl, lat_mod(i, 5), final_g, zero_mod, zero_mod, ln,
                                      modulate=False, a_dtype=F32)
        else:
            xl, a_lat = _matmul_residual(act, w_down_bf, i, xl, lat_mod(i, 5), norm_mix_g[i + 1],
                                         lat_mod(i + 1, 0), lat_mod(i + 1, 1), ln)
            xc, b_ctx = _matmul_residual(mix_ctx, w_mix, j, xc, ctx_mod(i, 2), norm_ffn_g[i], ctx_mod(i, 3),
                                         ctx_mod(i, 4), t_ctx)
            act_c = _proj_conv(b_ctx, ffn_w_up, i, ffn_conv_w[i, 1], ffn_conv_b[i], width=lc, vertical=False,
                               n_out=d_ff, val_offset=d_ff, tn=256)
            xc, a_ctx = _matmul_residual(act_c, w_down_bf, i, xc, ctx_mod(i, 5), norm_mix_g[i + 1],
                                         ctx_mod(i + 1, 0), ctx_mod(i + 1, 1), t_ctx)
    return out.reshape(nb, ln, d)
```

```python
import functools
import math

import jax
import jax.numpy as jnp
from jax import lax
from jax.experimental import pallas as pl
from jax.experimental.pallas import tpu as pltpu

F32 = jnp.float32
BF16 = jnp.bfloat16

EPS = 1e-6
GRID_W = 64
FOURIER_GROUPS = 8
SSD_HEAD_DIM = 64
SSD_GROUPS = 8
SSD_STATE = 128
SSD_CHUNK = 128
SCAN_UNROLL = 16

LANES = 128
SUBLANES = 8
VMEM_LIMIT_BYTES = 56 * 1024 * 1024
ROW_TILE = 2048
EPILOGUE_ROWS = 256
PROJ_BLOCK_ROWS = 512

_HIGHEST = lax.Precision.HIGHEST


def _cparams(*sem, flags=None):
    return pltpu.CompilerParams(dimension_semantics=sem, vmem_limit_bytes=VMEM_LIMIT_BYTES, flags=flags)


def _norm_rows(x, g, shift, scale, modulate):
    ms = jnp.mean(x * x, axis=-1, keepdims=True)
    y = x * lax.rsqrt(ms + EPS) * g
    if modulate:
        y = y * (1.0 + scale) + shift
    return y


def _silu(v):
    return v * jax.nn.sigmoid(v)


def _mod_kernel(s_ref, w_ref, b_ref, o_ref):
    s = _silu(s_ref[...]).astype(BF16)
    w = w_ref[0].astype(BF16)
    o_ref[0] = jnp.dot(s, w, preferred_element_type=F32) + b_ref[0]


def _modulation(cond, w_mod, b_mod):
    depth, d, n = w_mod.shape
    rows = cond.shape[0]
    tn = 1024
    return pl.pallas_call(
        _mod_kernel,
        grid=(depth, n // tn),
        in_specs=[
            pl.BlockSpec((rows, d), lambda i, j: (0, 0)),
            pl.BlockSpec((1, d, tn), lambda i, j: (i, 0, j)),
            pl.BlockSpec((1, 1, tn), lambda i, j: (i, 0, j)),
        ],
        out_specs=pl.BlockSpec((1, rows, tn), lambda i, j: (i, 0, j)),
        out_shape=jax.ShapeDtypeStruct((depth, rows, n), F32),
        compiler_params=_cparams("parallel", "parallel"),
        name="modulation",
    )(cond, w_mod, b_mod.reshape(depth, 1, n))


def _normmod_kernel(x_ref, g_ref, sh_ref, sc_ref, o_ref):
    o_ref[...] = _norm_rows(x_ref[...], g_ref[...], sh_ref[0], sc_ref[0], True).astype(o_ref.dtype)


def _normmod(x, g, shift, scale, rows_per_mod):
    m, d = x.shape
    tm = 512
    mod_spec = pl.BlockSpec((1, 1, d), lambda i: ((i * tm) // rows_per_mod, 0, 0))
    return pl.pallas_call(
        _normmod_kernel,
        grid=(m // tm,),
        in_specs=[
            pl.BlockSpec((tm, d), lambda i: (i, 0)),
            pl.BlockSpec((1, d), lambda i: (0, 0)),
            mod_spec,
            mod_spec,
        ],
        out_specs=pl.BlockSpec((tm, d), lambda i: (i, 0)),
        out_shape=jax.ShapeDtypeStruct((m, d), BF16),
        compiler_params=_cparams("parallel"),
        name="normmod",
    )(x, g.reshape(1, d), shift, scale)


def _mm_kernel(a_ref, w_ref, o_ref):
    o_ref[...] = jnp.dot(a_ref[...], w_ref[...], preferred_element_type=F32).astype(o_ref.dtype)


def _matmul(a, w, layer, out_dtype, tm=1024, tn=1024):
    m, k = a.shape
    n = w.shape[2]
    tm, tn = min(tm, m), min(tn, n)
    return pl.pallas_call(
        _mm_kernel,
        grid=(m // tm, n // tn),
        in_specs=[
            pl.BlockSpec((tm, k), lambda i, j: (i, 0)),
            pl.BlockSpec((None, k, tn), lambda i, j: (layer, 0, j)),
        ],
        out_specs=pl.BlockSpec((tm, tn), lambda i, j: (i, j)),
        out_shape=jax.ShapeDtypeStruct((m, n), out_dtype),
        compiler_params=_cparams("parallel", "parallel"),
        name="matmul",
    )(a, w)


def _mm_res_kernel(a_ref, w_ref, x_ref, gate_ref, g_ref, sh_ref, sc_ref, xo_ref, ao_ref, xn_scr, ssq_scr,
                   *, nn, modulate):
    j = pl.program_id(1)
    tn = x_ref.shape[1]
    w = w_ref[:, pl.ds(pl.multiple_of(j * tn, tn), tn)]
    xn = x_ref[...] + gate_ref[0] * jnp.dot(a_ref[...], w, preferred_element_type=F32)
    xo_ref[...] = xn
    xn_scr[j] = xn
    ssq = jnp.sum(xn * xn, axis=-1, keepdims=True)

    @pl.when(j == 0)
    def _():
        ssq_scr[...] = ssq

    @pl.when(j > 0)
    def _():
        ssq_scr[...] += ssq

    @pl.when(j == nn - 1)
    def _():
        inv = lax.rsqrt(ssq_scr[...] * (1.0 / (nn * tn)) + EPS)
        for c in range(nn):
            cols = slice(c * tn, (c + 1) * tn)
            y = xn_scr[c] * inv * g_ref[:, cols]
            if modulate:
                y = y * (1.0 + sc_ref[0][:, cols]) + sh_ref[0][:, cols]
            ao_ref[:, cols] = y.astype(ao_ref.dtype)


def _residual_tile_cols(k):
    return 1024 if k <= 4096 else 512


def _matmul_residual(a, w, layer, x, gate, g_next, shift, scale, rows_per_mod, modulate=True, a_dtype=BF16):
    m, k = a.shape
    d = w.shape[2]
    tn = _residual_tile_cols(k)
    nn = d // tn
    tm = min(512, m)

    def mod_row(i, j):
        return ((i * tm) // rows_per_mod, 0, 0)

    mod_spec = pl.BlockSpec((1, 1, d), mod_row)
    return pl.pallas_call(
        functools.partial(_mm_res_kernel, nn=nn, modulate=modulate),
        grid=(m // tm, nn),
        in_specs=[
            pl.BlockSpec((tm, k), lambda i, j: (i, 0)),
            pl.BlockSpec((None, k, d), lambda i, j: (layer, 0, 0), pipeline_mode=pl.Buffered(1)),
            pl.BlockSpec((tm, tn), lambda i, j: (i, j)),
            pl.BlockSpec((1, 1, tn), lambda i, j: ((i * tm) // rows_per_mod, 0, j)),
            pl.BlockSpec((1, d), lambda i, j: (0, 0)),
            mod_spec,
            mod_spec,
        ],
        out_specs=[
            pl.BlockSpec((tm, tn), lambda i, j: (i, j)),
            pl.BlockSpec((tm, d), lambda i, j: (i, 0)),
        ],
        out_shape=[
            jax.ShapeDtypeStruct((m, d), F32),
            jax.ShapeDtypeStruct((m, d), a_dtype),
        ],
        scratch_shapes=[pltpu.VMEM((nn, tm, tn), F32), pltpu.VMEM((tm, 1), F32)],
        compiler_params=_cparams("parallel", "arbitrary"),
        name="matmul_residual",
    )(a, w, x, gate, g_next.reshape(1, d), shift, scale)


def _proj_conv_kernel(*refs, rows, width, vertical, gated):
    refs = list(refs)
    a_ref, wg_ref = refs[:2]
    wv_ref = refs.pop(2) if gated else None
    cw_ref, cb_ref, o_ref, u_scr = refs[2:6]
    scratch = refs[6:]
    if vertical:
        (ul_scr, ur_scr), scratch = scratch[:2], scratch[2:]
    if gated:
        v_scr, scratch = scratch[0], scratch[1:]
    if wg_ref.dtype != BF16:
        scratch[0][...] = wg_ref[...].astype(BF16)
        wg_ref = scratch[0]
        if gated:
            scratch[1][...] = wv_ref[...].astype(BF16)
            wv_ref = scratch[1]
    tn = o_ref.shape[1]
    pad = (u_scr.shape[0] - rows) // 2
    rb = min(PROJ_BLOCK_ROWS, rows)
    rc = min(EPILOGUE_ROWS, rb)
    zeros = jnp.zeros((pad, tn), F32)
    for buf in [u_scr] + ([ul_scr, ur_scr] if vertical else []):
        buf[pl.ds(0, pad), :] = zeros
        buf[pl.ds(pad + rows, pad), :] = zeros
    cw = cw_ref[...]
    bias = cb_ref[...]
    row = lax.broadcasted_iota(jnp.int32, (rc, tn), 0)

    def project(blk):
        a = a_ref[pl.ds(blk * rb, rb), :]
        u_scr[pl.ds(pad + blk * rb, rb), :] = jnp.dot(a, wg_ref[...], preferred_element_type=F32)
        if gated:
            v_scr[pl.ds(blk * rb, rb), :] = jnp.dot(a, wv_ref[...], preferred_element_type=F32)
        if vertical:
            col = row & (width - 1)
            for r in range(blk * rb // rc, (blk + 1) * rb // rc):
                chunk = u_scr[pl.ds(pad + r * rc, rc), :]
                ul_scr[pl.ds(pad + r * rc, rc), :] = jnp.where(col != 0, pltpu.roll(chunk, 1, 0), 0.0)
                ur_scr[pl.ds(pad + r * rc, rc), :] = jnp.where(col != width - 1, pltpu.roll(chunk, rc - 1, 0), 0.0)

    def convolve(blk):
        for r in range(blk * rb // rc, (blk + 1) * rb // rc):
            base = pad + r * rc
            if vertical:
                acc = bias
                for di in range(3):
                    off = base + (di - 1) * width
                    acc = acc + ul_scr[pl.ds(off, rc), :] * cw[3 * di:3 * di + 1, :]
                    acc = acc + u_scr[pl.ds(off, rc), :] * cw[3 * di + 1:3 * di + 2, :]
                    acc = acc + ur_scr[pl.ds(off, rc), :] * cw[3 * di + 2:3 * di + 3, :]
            else:
                col = (row + r * rc) & (width - 1)
                left = jnp.where(col != 0, u_scr[pl.ds(base - 1, rc), :], 0.0)
                right = jnp.where(col != width - 1, u_scr[pl.ds(base + 1, rc), :], 0.0)
                acc = left * cw[0:1, :] + u_scr[pl.ds(base, rc), :] * cw[1:2, :] + right * cw[2:3, :] + bias
            y = _silu(acc)
            if gated:
                y = y * v_scr[pl.ds(r * rc, rc), :]
            o_ref[pl.ds(r * rc, rc), :] = y.astype(o_ref.dtype)

    n_blocks = rows // rb
    for blk in range(n_blocks):
        project(blk)
        if blk >= 1:
            convolve(blk - 1)
    convolve(n_blocks - 1)


def _proj_conv(a, w, layer, conv_w, conv_b, *, width, vertical, n_out, val_offset=None, tn):
    m, k = a.shape
    rows = min(ROW_TILE, m)
    gated = val_offset is not None
    assert width & (width - 1) == 0 and rows % width == 0 and m % rows == 0
    pad = (width + SUBLANES) if vertical else SUBLANES
    taps = conv_w.shape[0]
    in_specs = [
        pl.BlockSpec((rows, k), lambda i, j: (i, 0)),
        pl.BlockSpec((None, k, tn), lambda i, j: (layer, 0, j)),
    ]
    operands = [a, w]
    padded = pltpu.VMEM((rows + 2 * pad, tn), F32)
    scratch = [padded] + ([padded, padded] if vertical else []) + ([pltpu.VMEM((rows, tn), F32)] if gated else [])
    if w.dtype != BF16:
        scratch += [pltpu.VMEM((k, tn), BF16)] * (2 if gated else 1)
    if gated:
        voff = val_offset // tn
        in_specs.append(pl.BlockSpec((None, k, tn), lambda i, j: (layer, 0, j + voff)))
        operands.append(w)
    in_specs += [
        pl.BlockSpec((taps, tn), lambda i, j: (0, j)),
        pl.BlockSpec((1, tn), lambda i, j: (0, j)),
    ]
    operands += [conv_w, conv_b.reshape(1, n_out)]
    return pl.pallas_call(
        functools.partial(_proj_conv_kernel, rows=rows, width=width, vertical=vertical, gated=gated),
        grid=(m // rows, n_out // tn),
        in_specs=in_specs,
        out_specs=pl.BlockSpec((rows, tn), lambda i, j: (i, j)),
        out_shape=jax.ShapeDtypeStruct((m, n_out), BF16),
        scratch_shapes=scratch,
        compiler_params=_cparams("parallel", "parallel"),
        name="proj_conv",
    )(*operands)


def _dft_tables(n):
    s = 64
    k = jnp.arange(n, dtype=jnp.int32)[:, None]

    def narrow(cols):
        ang = ((k * cols[None, :]) % n).astype(F32) * F32(2.0 * math.pi / n)
        return jnp.cos(ang), jnp.sin(ang)

    ca, sa = narrow(jnp.arange(n // s, dtype=jnp.int32) * s)
    cb, sb = narrow(jnp.arange(s, dtype=jnp.int32))
    scale = F32(1.0 / math.sqrt(n))
    cos = (ca[:, :, None] * cb[:, None, :] - sa[:, :, None] * sb[:, None, :]).reshape(n, n) * scale
    sin = (sa[:, :, None] * cb[:, None, :] + ca[:, :, None] * sb[:, None, :]).reshape(n, n) * scale
    return cos, sin


def _chan_dft_kernel(a_ref, wc_ref, o_ref, *, groups):
    dg = wc_ref.shape[0]
    wc = wc_ref[...]
    for g in range(groups):
        cols = pl.ds(g * dg, dg)
        res = jnp.dot(a_ref[:, cols], wc, preferred_element_type=F32)
        o_ref[0, :, cols] = res[:, :dg].astype(o_ref.dtype)
        o_ref[1, :, cols] = res[:, dg:].astype(o_ref.dtype)


def _chan_dft(a, wc, nb, ln):
    d = a.shape[1]
    dg = wc.shape[0]
    tm = min(512, ln)
    mt = ln // tm
    return pl.pallas_call(
        functools.partial(_chan_dft_kernel, groups=d // dg),
        grid=(nb, mt),
        in_specs=[
            pl.BlockSpec((tm, d), lambda b, i: (b * mt + i, 0)),
            pl.BlockSpec((dg, 2 * dg), lambda b, i: (0, 0)),
        ],
        out_specs=pl.BlockSpec((None, 2, tm, d), lambda b, i: (b, 0, i, 0)),
        out_shape=jax.ShapeDtypeStruct((nb, 2, ln, d), BF16),
        compiler_params=_cparams("parallel", "parallel"),
        name="chan_dft",
    )(a, wc)


def _pos_dft(cs, y):
    nb, k2, d = y.shape
    ln = cs.shape[0]
    tm, tn = min(1024, ln), 1024
    return pl.pallas_call(
        _mm_kernel,
        grid=(nb, d // tn, ln // tm),
        in_specs=[
            pl.BlockSpec((tm, k2), lambda b, j, i: (i, 0)),
            pl.BlockSpec((None, k2, tn), lambda b, j, i: (b, 0, j)),
        ],
        out_specs=pl.BlockSpec((None, tm, tn), lambda b, j, i: (b, i, j)),
        out_shape=jax.ShapeDtypeStruct((nb, ln, d), BF16),
        compiler_params=_cparams("parallel", "parallel", "parallel"),
        name="pos_dft",
    )(cs, y)


def _fourier(a, nb, ln, wc, cs):
    d = a.shape[1]
    y = _chan_dft(a, wc, nb, ln)
    f = _pos_dft(cs, y.reshape(nb, 2 * ln, d))
    return f.reshape(nb * ln, d)


def _ssd_dt_kernel(a_ref, wt_ref, bias_ref, alog_ref, dtt_ref, acst_ref, *, heads, groups):
    ln = a_ref.shape[0]
    q = SSD_CHUNK
    r = heads // groups
    dtt = jax.nn.softplus(
        lax.dot_general(wt_ref[...], a_ref[...], (((1,), (1,)), ((), ())), preferred_element_type=F32)
        + bias_ref[...])
    dtat = dtt * (-jnp.exp(alog_ref[...]))
    ri = lax.broadcasted_iota(jnp.int32, (q, q), 0)
    ci = lax.broadcasted_iota(jnp.int32, (q, q), 1)
    lower = (ri >= ci).astype(F32)
    upper = (ri <= ci).astype(F32)
    fwd_row = lax.broadcasted_iota(jnp.int32, (2 * heads, q), 0) < heads
    for c in range(ln // q):
        xt = dtat[:, c * q:(c + 1) * q]
        prefix = jnp.dot(xt, upper, precision=_HIGHEST, preferred_element_type=F32)
        suffix = jnp.dot(xt, lower, precision=_HIGHEST, preferred_element_type=F32)
        acs = jnp.where(fwd_row, prefix, suffix)
        dtc = dtt[:, c * q:(c + 1) * q]
        for g in range(groups):
            for half in range(2):
                src = slice(half * heads + g * r, half * heads + (g + 1) * r)
                dst = pl.ds(half * r, r)
                acst_ref[g, c, dst, :] = acs[src, :]
                dtt_ref[g, c, dst, :] = dtc[src, :]


def _ssd_dt(a, w_dt_t, dt_bias, a_log, nb, ln):
    d = a.shape[1]
    h2 = w_dt_t.shape[0]
    groups = SSD_GROUPS
    nc = ln // SSD_CHUNK
    vec = pl.BlockSpec((h2, 1), lambda b: (0, 0))
    out = pl.BlockSpec((None, groups, nc, h2 // groups, SSD_CHUNK), lambda b: (b, 0, 0, 0, 0))
    shape = jax.ShapeDtypeStruct((nb, groups, nc, h2 // groups, SSD_CHUNK), F32)
    return pl.pallas_call(
        functools.partial(_ssd_dt_kernel, heads=h2 // 2, groups=groups),
        grid=(nb,),
        in_specs=[
            pl.BlockSpec((ln, d), lambda b: (b, 0)),
            pl.BlockSpec((h2, d), lambda b: (0, 0)),
            vec, vec,
        ],
        out_specs=[out, out],
        out_shape=[shape, shape],
        compiler_params=_cparams("parallel"),
        name="ssd_dt",
    )(a, w_dt_t, dt_bias.reshape(h2, 1), a_log.reshape(h2, 1))


def _ssd_scan_kernel(*refs, nc, heads_per_group, emit):
    if emit:
        (xs_ref, b_ref, c_ref, z_ref, acst_ref, dtt_ref, h0_ref, d0_ref, d1_ref, ng_ref, o_ref, hfin_ref,
         yf_scr, h_scr, tok_scr) = refs
    else:
        xs_ref, b_ref, acst_ref, dtt_ref, h0_ref, hfin_ref, h_scr, tok_scr = refs
    q = SSD_CHUNK
    r_heads = heads_per_group
    p = SSD_HEAD_DIM
    gw = r_heads * p
    expand = (lax.broadcasted_iota(jnp.int32, (2 * r_heads, 2 * gw), 1) // p
              == lax.broadcasted_iota(jnp.int32, (2 * r_heads, 2 * gw), 0)).astype(BF16)
    li = lax.broadcasted_iota(jnp.int32, (q, q), 0)
    si = lax.broadcasted_iota(jnp.int32, (q, q), 1)
    lane = lax.broadcasted_iota(jnp.int32, (q, gw), 1)
    even_head = (lane % (2 * p)) < p
    pad_rows = jnp.zeros((q - 3 * r_heads, q), F32)
    if emit:
        dskip = d0_ref[...] + d1_ref[...]
        norm_g = ng_ref[...]

    def chunk(c, direction):
        rows = pl.ds(pl.multiple_of(c * q, q), q)
        xs_bf = xs_ref[rows, :]
        bc = b_ref[rows, :]
        tok = tok_scr[direction, c]
        fac = tok[:, :2 * r_heads]
        fac_hi = fac.astype(BF16)
        fac_lo = (fac - fac_hi.astype(F32)).astype(BF16)
        fac_e = (jnp.dot(fac_hi, expand, preferred_element_type=F32)
                 + jnp.dot(fac_lo, expand, preferred_element_type=F32))
        ea_e = fac_e[:, :gw]
        wf_e = fac_e[:, gw:]
        h_decay = ea_e[q - 1:q, :] if direction == 0 else ea_e[0:1, :]
        h = h_scr[...]

        def update_state():
            w = xs_bf * wf_e.astype(BF16)
            h_scr[...] = h * h_decay + lax.dot_general(
                bc, w, (((0,), (0,)), ((), ())), preferred_element_type=F32)

        if not emit:
            update_state()
            return
        cc = c_ref[rows, :]
        src_rows = acst_ref[c] - jnp.log(dtt_ref[c])
        causal = (li >= si) if direction == 0 else (li <= si)
        scores = lax.dot_general(cc, bc, (((1,), (1,)), ((), ())), preferred_element_type=F32)
        y = jnp.dot(cc, h.astype(BF16), preferred_element_type=F32) * ea_e
        xs_even = jnp.where(even_head, xs_bf, jnp.zeros_like(xs_bf))
        xs_odd = jnp.where(even_head, jnp.zeros_like(xs_bf), xs_bf)
        pieces = []
        for pair in range(r_heads // 2):
            ms = []
            for sub in range(2):
                r = 2 * pair + sub
                col = direction * r_heads + r
                seg = tok[:, 2 * r_heads + r:2 * r_heads + r + 1] - src_rows[col:col + 1, :]
                decay = jnp.exp(jnp.where(causal, seg, -jnp.inf))
                ms.append((scores * decay).astype(BF16))
            cols = slice(pair * 2 * p, (pair + 1) * 2 * p)
            pieces.append(jnp.dot(jnp.concatenate(ms, axis=1),
                                  jnp.concatenate([xs_even[:, cols], xs_odd[:, cols]], axis=0),
                                  preferred_element_type=F32))
        y = y + jnp.concatenate(pieces, axis=1)
        update_state()
        if direction == 0:
            yf_scr[rows, :] = y
        else:
            tot = yf_scr[rows, :] + y + dskip * xs_bf.astype(F32)
            gz = tot * _silu(z_ref[rows, :].astype(F32))
            gz = gz * lax.rsqrt(jnp.mean(gz * gz, axis=-1, keepdims=True) + EPS)
            o_ref[rows, :] = (gz * norm_g).astype(o_ref.dtype)

    def to_token_major(c, carry):
        acs_rows = acst_ref[c]
        dt_rows = dtt_ref[c]
        for direction in range(2):
            acs_d = acs_rows[direction * r_heads:(direction + 1) * r_heads, :]
            dt_d = dt_rows[direction * r_heads:(direction + 1) * r_heads, :]
            last = acs_d[:, q - 1:q] if direction == 0 else acs_d[:, 0:1]
            tok_scr[direction, c] = jnp.concatenate(
                [jnp.exp(acs_d), dt_d * jnp.exp(last - acs_d), acs_d, pad_rows], axis=0).T
        return carry

    lax.fori_loop(0, nc, to_token_major, 0, unroll=True)
    h_scr[...] = h0_ref[0]

    def fwd_body(c, carry):
        chunk(c, 0)
        return carry

    lax.fori_loop(0, nc, fwd_body, 0, unroll=min(SCAN_UNROLL, nc))
    hfin_ref[0] = h_scr[...]
    h_scr[...] = h0_ref[1]

    def bwd_body(i, carry):
        chunk(nc - 1 - i, 1)
        return carry

    lax.fori_loop(0, nc, bwd_body, 0, unroll=min(SCAN_UNROLL, nc))
    hfin_ref[1] = h_scr[...]


def _ssd_scan(xbc, z, acst4, dtt4, h0, d0e, d1e, norm_g, nb, ln):
    emit = z is not None
    groups = SSD_GROUPS
    n = SSD_STATE
    gw = h0.shape[-1]
    d_inner = gw * groups
    r2 = acst4.shape[-2]
    nc = ln // SSD_CHUNK
    b_blk = d_inner // n
    c_blk = b_blk + groups
    chunk_g = pl.BlockSpec((None, None, nc, r2, SSD_CHUNK), lambda b, g: (b, g, 0, 0, 0))
    state = pl.BlockSpec((None, None, 2, n, gw), lambda b, g: (b, g, 0, 0, 0))
    vec = pl.BlockSpec((1, gw), lambda b, g: (0, g))
    seq_x = pl.BlockSpec((ln, gw), lambda b, g: (b, g))
    seq_b = pl.BlockSpec((ln, n), lambda b, g: (b, b_blk + g))
    seq_c = pl.BlockSpec((ln, n), lambda b, g: (b, c_blk + g))
    state_shape = jax.ShapeDtypeStruct(h0.shape, F32)
    scratch = [pltpu.VMEM((n, gw), F32), pltpu.VMEM((2, nc, SSD_CHUNK, SSD_CHUNK), F32)]
    if emit:
        in_specs = [seq_x, seq_b, seq_c, seq_x, chunk_g, chunk_g, state, vec, vec, vec]
        operands = (xbc, xbc, xbc, z, acst4, dtt4, h0, d0e, d1e, norm_g)
        out_specs = [seq_x, state]
        out_shape = [jax.ShapeDtypeStruct((nb * ln, d_inner), BF16), state_shape]
        scratch = [pltpu.VMEM((ln, gw), F32)] + scratch
    else:
        in_specs = [seq_x, seq_b, chunk_g, chunk_g, state]
        operands = (xbc, xbc, acst4, dtt4, h0)
        out_specs = [state]
        out_shape = [state_shape]
    res = pl.pallas_call(
        functools.partial(_ssd_scan_kernel, nc=nc, heads_per_group=r2 // 2, emit=emit),
        grid=(nb, groups),
        in_specs=in_specs,
        out_specs=out_specs,
        out_shape=out_shape,
        scratch_shapes=scratch,
        compiler_params=_cparams("parallel", "parallel"),
        name="ssd_scan",
    )(*operands)
    return res if emit else (None, res[0])


def _ssd_mixer(a, nb, ln, weights, layer, h0, emit=True):
    w_xbc, w_z, w_dt_t, conv_w, conv_b, dt_bias, a_log, d0e, d1e, norm_g = weights
    d_inner = w_z.shape[2]
    n_out = w_xbc.shape[2] if emit else d_inner + SSD_GROUPS * SSD_STATE
    xbc = _proj_conv(a, w_xbc, layer, conv_w[:, :n_out], conv_b[:n_out], width=ln, vertical=False, n_out=n_out,
                     tn=512)
    z = _matmul(a, w_z, layer, BF16) if emit else None
    dtt4, acst4 = _ssd_dt(a, w_dt_t, dt_bias, a_log, nb, ln)
    return _ssd_scan(xbc, z, acst4, dtt4, h0, d0e, d1e, norm_g, nb, ln)


def kernel(x, c, ctx, c_ctx, w_mod, b_mod, norm_mix_g, norm_ffn_g, four_w, ssd_w_in, ssd_conv_w, ssd_conv_b,
           ssd_dt_bias, ssd_a_log, ssd_d, ssd_norm_g, ssd_w_out, ffn_w_up, ffn_conv_w, ffn_conv_b, ffn_w_down,
           final_g):
    nb, ln, d = x.shape
    lc = ctx.shape[1]
    depth = w_mod.shape[0]
    d_ff = ffn_w_down.shape[1]
    d_inner = ssd_w_out.shape[1]
    heads = ssd_dt_bias.shape[2]
    gn = SSD_GROUPS * SSD_STATE
    t_lat, t_ctx = nb * ln, nb * lc

    mod_rows = -(-(nb + 1) // SUBLANES) * SUBLANES
    cond = jnp.zeros((mod_rows, d), F32).at[:nb].set(c).at[nb].set(c_ctx)
    mods = _modulation(cond, w_mod, b_mod)

    def lat_mod(i, j):
        return mods[i, :nb, j * d:(j + 1) * d].reshape(nb, 1, d)

    def ctx_mod(i, j):
        return mods[i, nb:nb + 1, j * d:(j + 1) * d].reshape(1, 1, d)

    four_bf = four_w.astype(BF16)
    w_down_bf = ffn_w_down.astype(BF16)
    w_out_bf = ssd_w_out.astype(BF16)
    xb = d_inner + gn
    state_cols = xb + 2 * heads
    w_xbc_bf = jnp.concatenate([ssd_w_in[:, :, :xb], ssd_w_in[:, :, state_cols:state_cols + gn]],
                               axis=2).astype(BF16)
    w_z_bf = ssd_w_in[:, :, state_cols + gn:].astype(BF16)
    w_dt_t_bf = ssd_w_in[:, :, xb:state_cols].transpose(0, 2, 1).astype(BF16)
    conv9 = ffn_conv_w.reshape(depth, 9, d_ff)

    dg = d // FOURIER_GROUPS
    cos_c, sin_c = _dft_tables(dg)
    wc = jnp.concatenate([cos_c, sin_c], axis=1).astype(BF16)
    cos_l, sin_l = _dft_tables(ln)
    cs_lat = jnp.concatenate([cos_l, -sin_l], axis=1).astype(BF16)
    cos_x, sin_x = _dft_tables(lc)
    cs_ctx = jnp.concatenate([cos_x, -sin_x], axis=1).astype(BF16)

    xl = x.reshape(t_lat, d)
    xc = ctx.reshape(t_ctx, d)
    a_lat = _normmod(xl, norm_mix_g[0], lat_mod(0, 0), lat_mod(0, 1), ln)
    a_ctx = _normmod(xc, norm_mix_g[0], ctx_mod(0, 0), ctx_mod(0, 1), t_ctx)
    out = None
    for i in range(depth):
        last = i == depth - 1
        is_ssd = i % 2 == 1
        j = i // 2
        if is_ssd:
            d0e = jnp.repeat(ssd_d[j, 0], SSD_HEAD_DIM).reshape(1, d_inner)
            d1e = jnp.repeat(ssd_d[j, 1], SSD_HEAD_DIM).reshape(1, d_inner)
            weights = (w_xbc_bf, w_z_bf, w_dt_t_bf[j], ssd_conv_w[j], ssd_conv_b[j], ssd_dt_bias[j], ssd_a_log[j],
                       d0e, d1e, ssd_norm_g[j].reshape(1, d_inner))
            zeros = jnp.zeros((nb, SSD_GROUPS, 2, SSD_STATE, d_inner // SSD_GROUPS), F32)
            mix_ctx, h_ctx = _ssd_mixer(a_ctx, nb, lc, weights, j, zeros, emit=not last)
            mix_lat, _ = _ssd_mixer(a_lat, nb, ln, weights, j, h_ctx)
            w_mix = w_out_bf
        else:
            mix_lat = _fourier(a_lat, nb, ln, wc, cs_lat)
            mix_ctx = None if last else _fourier(a_ctx, nb, lc, wc, cs_ctx)
            w_mix = four_bf
        xl, b_lat = _matmul_residual(mix_lat, w_mix, j, xl, lat_mod(i, 2), norm_ffn_g[i], lat_mod(i, 3),
                                     lat_mod(i, 4), ln)
        act = _proj_conv(b_lat, ffn_w_up, i, conv9[i], ffn_conv_b[i], width=GRID_W, vertical=True,
                         n_out=d_ff, val_offset=d_ff, tn=256)
        if last:
            zero_mod = jnp.zeros((nb, 1, d), F32)
            _, out = _matmul_residual(act, w_down_bf, i, xl, lat_mod(i, 5), final_g, zero_mod, zero_mod, ln,
                                      modulate=False, a_dtype=F32)
        else:
            xl, a_lat = _matmul_residual(act, w_down_bf, i, xl, lat_mod(i, 5), norm_mix_g[i + 1],
                                         lat_mod(i + 1, 0), lat_mod(i + 1, 1), ln)
            xc, b_ctx = _matmul_residual(mix_ctx, w_mix, j, xc, ctx_mod(i, 2), norm_ffn_g[i], ctx_mod(i, 3),
                                         ctx_mod(i, 4), t_ctx)
            act_c = _proj_conv(b_ctx, ffn_w_up, i, ffn_conv_w[i, 1], ffn_conv_b[i], width=lc, vertical=False,
                               n_out=d_ff, val_offset=d_ff, tn=256)
            xc, a_ctx = _matmul_residual(act_c, w_down_bf, i, xc, ctx_mod(i, 5), norm_mix_g[i + 1],
                                         ctx_mod(i + 1, 0), ctx_mod(i + 1, 1), t_ctx)
    return out.reshape(nb, ln, d)
```

```python
import functools
import math

import jax
import jax.numpy as jnp
from jax import lax
from jax.experimental import pallas as pl
from jax.experimental.pallas import tpu as pltpu

F32 = jnp.float32
BF16 = jnp.bfloat16

EPS = 1e-6
GRID_W = 64
FOURIER_GROUPS = 8
SSD_HEAD_DIM = 64
SSD_GROUPS = 8
SSD_STATE = 128
SSD_CHUNK = 128
SCAN_UNROLL = 16

LANES = 128
SUBLANES = 8
VMEM_LIMIT_BYTES = 56 * 1024 * 1024
ROW_TILE = 2048
EPILOGUE_ROWS = 256
PROJ_BLOCK_ROWS = 512

_HIGHEST = lax.Precision.HIGHEST


def _cparams(*sem, flags=None):
    return pltpu.CompilerParams(dimension_semantics=sem, vmem_limit_bytes=VMEM_LIMIT_BYTES, flags=flags)


def _norm_rows(x, g, shift, scale, modulate):
    ms = jnp.mean(x * x, axis=-1, keepdims=True)
    y = x * lax.rsqrt(ms + EPS) * g
    if modulate:
        y = y * (1.0 + scale) + shift
    return y


def _silu(v):
    return v * jax.nn.sigmoid(v)


def _mod_kernel(s_ref, w_ref, b_ref, o_ref):
    s = _silu(s_ref[...]).astype(BF16)
    w = w_ref[0].astype(BF16)
    o_ref[0] = jnp.dot(s, w, preferred_element_type=F32) + b_ref[0]


def _modulation(cond, w_mod, b_mod):
    depth, d, n = w_mod.shape
    rows = cond.shape[0]
    tn = 1024
    return pl.pallas_call(
        _mod_kernel,
        grid=(depth, n // tn),
        in_specs=[
            pl.BlockSpec((rows, d), lambda i, j: (0, 0)),
            pl.BlockSpec((1, d, tn), lambda i, j: (i, 0, j)),
            pl.BlockSpec((1, 1, tn), lambda i, j: (i, 0, j)),
        ],
        out_specs=pl.BlockSpec((1, rows, tn), lambda i, j: (i, 0, j)),
        out_shape=jax.ShapeDtypeStruct((depth, rows, n), F32),
        compiler_params=_cparams("parallel", "parallel"),
        name="modulation",
    )(cond, w_mod, b_mod.reshape(depth, 1, n))


def _normmod_kernel(x_ref, g_ref, sh_ref, sc_ref, o_ref):
    o_ref[...] = _norm_rows(x_ref[...], g_ref[...], sh_ref[0], sc_ref[0], True).astype(o_ref.dtype)


def _normmod(x, g, shift, scale, rows_per_mod):
    m, d = x.shape
    tm = 512
    mod_spec = pl.BlockSpec((1, 1, d), lambda i: ((i * tm) // rows_per_mod, 0, 0))
    return pl.pallas_call(
        _normmod_kernel,
        grid=(m // tm,),
        in_specs=[
            pl.BlockSpec((tm, d), lambda i: (i, 0)),
            pl.BlockSpec((1, d), lambda i: (0, 0)),
            mod_spec,
            mod_spec,
        ],
        out_specs=pl.BlockSpec((tm, d), lambda i: (i, 0)),
        out_shape=jax.ShapeDtypeStruct((m, d), BF16),
        compiler_params=_cparams("parallel"),
        name="normmod",
    )(x, g.reshape(1, d), shift, scale)


def _mm_kernel(a_ref, w_ref, o_ref):
    o_ref[...] = jnp.dot(a_ref[...], w_ref[...], preferred_element_type=F32).astype(o_ref.dtype)


def _matmul(a, w, layer, out_dtype, tm=1024, tn=1024):
    m, k = a.shape
    n = w.shape[2]
    tm, tn = min(tm, m), min(tn, n)
    return pl.pallas_call(
        _mm_kernel,
        grid=(m // tm, n // tn),
        in_specs=[
            pl.BlockSpec((tm, k), lambda i, j: (i, 0)),
            pl.BlockSpec((None, k, tn), lambda i, j: (layer, 0, j)),
        ],
        out_specs=pl.BlockSpec((tm, tn), lambda i, j: (i, j)),
        out_shape=jax.ShapeDtypeStruct((m, n), out_dtype),
        compiler_params=_cparams("parallel", "parallel"),
        name="matmul",
    )(a, w)


def _mm_res_kernel(a_ref, w_ref, x_ref, gate_ref, g_ref, sh_ref, sc_ref, xo_ref, ao_ref, xn_scr, ssq_scr,
                   *, nn, modulate):
    j = pl.program_id(1)
    tn = x_ref.shape[1]
    w = w_ref[:, pl.ds(pl.multiple_of(j * tn, tn), tn)]
    xn = x_ref[...] + gate_ref[0] * jnp.dot(a_ref[...], w, preferred_element_type=F32)
    xo_ref[...] = xn
    xn_scr[j] = xn
    ssq = jnp.sum(xn * xn, axis=-1, keepdims=True)

    @pl.when(j == 0)
    def _():
        ssq_scr[...] = ssq

    @pl.when(j > 0)
    def _():
        ssq_scr[...] += ssq

    @pl.when(j == nn - 1)
    def _():
        inv = lax.rsqrt(ssq_scr[...] * (1.0 / (nn * tn)) + EPS)
        for c in range(nn):
            cols = slice(c * tn, (c + 1) * tn)
            y = xn_scr[c] * inv * g_ref[:, cols]
            if modulate:
                y = y * (1.0 + sc_ref[0][:, cols]) + sh_ref[0][:, cols]
            ao_ref[:, cols] = y.astype(ao_ref.dtype)


def _residual_tile_cols(k):
    return 1024 if k <= 4096 else 512


def _matmul_residual(a, w, layer, x, gate, g_next, shift, scale, rows_per_mod, modulate=True, a_dtype=BF16):
    m, k = a.shape
    d = w.shape[2]
    tn = _residual_tile_cols(k)
    nn = d // tn
    tm = min(512, m)

    def mod_row(i, j):
        return ((i * tm) // rows_per_mod, 0, 0)

    mod_spec = pl.BlockSpec((1, 1, d), mod_row)
    return pl.pallas_call(
        functools.partial(_mm_res_kernel, nn=nn, modulate=modulate),
        grid=(m // tm, nn),
        in_specs=[
            pl.BlockSpec((tm, k), lambda i, j: (i, 0)),
            pl.BlockSpec((None, k, d), lambda i, j: (layer, 0, 0), pipeline_mode=pl.Buffered(1)),
            pl.BlockSpec((tm, tn), lambda i, j: (i, j)),
            pl.BlockSpec((1, 1, tn), lambda i, j: ((i * tm) // rows_per_mod, 0, j)),
            pl.BlockSpec((1, d), lambda i, j: (0, 0)),
            mod_spec,
            mod_spec,
        ],
        out_specs=[
            pl.BlockSpec((tm, tn), lambda i, j: (i, j)),
            pl.BlockSpec((tm, d), lambda i, j: (i, 0)),
        ],
        out_shape=[
            jax.ShapeDtypeStruct((m, d), F32),
            jax.ShapeDtypeStruct((m, d), a_dtype),
        ],
        scratch_shapes=[pltpu.VMEM((nn, tm, tn), F32), pltpu.VMEM((tm, 1), F32)],
        compiler_params=_cparams("parallel", "arbitrary"),
        name="matmul_residual",
    )(a, w, x, gate, g_next.reshape(1, d), shift, scale)


def _proj_conv_kernel(*refs, rows, width, vertical, gated):
    refs = list(refs)
    a_ref, wg_ref = refs[:2]
    wv_ref = refs.pop(2) if gated else None
    cw_ref, cb_ref, o_ref, u_scr = refs[2:6]
    scratch = refs[6:]
    if vertical:
        (ul_scr, ur_scr), scratch = scratch[:2], scratch[2:]
    if gated:
        v_scr, scratch = scratch[0], scratch[1:]
    if wg_ref.dtype != BF16:
        scratch[0][...] = wg_ref[...].astype(BF16)
        wg_ref = scratch[0]
        if gated:
            scratch[1][...] = wv_ref[...].astype(BF16)
            wv_ref = scratch[1]
    tn = o_ref.shape[1]
    pad = (u_scr.shape[0] - rows) // 2
    rb = min(PROJ_BLOCK_ROWS, rows)
    rc = min(EPILOGUE_ROWS, rb)
    zeros = jnp.zeros((pad, tn), F32)
    for buf in [u_scr] + ([ul_scr, ur_scr] if vertical else []):
        buf[pl.ds(0, pad), :] = zeros
        buf[pl.ds(pad + rows, pad), :] = zeros
    cw = cw_ref[...]
    bias = cb_ref[...]
    row = lax.broadcasted_iota(jnp.int32, (rc, tn), 0)

    def project(blk):
        a = a_ref[pl.ds(blk * rb, rb), :]
        u_scr[pl.ds(pad + blk * rb, rb), :] = jnp.dot(a, wg_ref[...], preferred_element_type=F32)
        if vertical:
            col = row & (width - 1)
            for r in range(blk * rb // rc, (blk + 1) * rb // rc):
                chunk = u_scr[pl.ds(pad + r * rc, rc), :]
                ul_scr[pl.ds(pad + r * rc, rc), :] = jnp.where(col != 0, pltpu.roll(chunk, 1, 0), 0.0)
                ur_scr[pl.ds(pad + r * rc, rc), :] = jnp.where(col != width - 1, pltpu.roll(chunk, rc - 1, 0), 0.0)

    def convolve(blk):
        for r in range(blk * rb // rc, (blk + 1) * rb // rc):
            base = pad + r * rc
            if vertical:
                acc = bias
                for di in range(3):
                    off = base + (di - 1) * width
                    acc = acc + ul_scr[pl.ds(off, rc), :] * cw[3 * di:3 * di + 1, :]
                    acc = acc + u_scr[pl.ds(off, rc), :] * cw[3 * di + 1:3 * di + 2, :]
                    acc = acc + ur_scr[pl.ds(off, rc), :] * cw[3 * di + 2:3 * di + 3, :]
            else:
                col = (row + r * rc) & (width - 1)
                left = jnp.where(col != 0, u_scr[pl.ds(base - 1, rc), :], 0.0)
                right = jnp.where(col != width - 1, u_scr[pl.ds(base + 1, rc), :], 0.0)
                acc = left * cw[0:1, :] + u_scr[pl.ds(base, rc), :] * cw[1:2, :] + right * cw[2:3, :] + bias
            y = _silu(acc)
            if gated:
                y = y * v_scr[pl.ds(r * rc, rc), :]
            o_ref[pl.ds(r * rc, rc), :] = y.astype(o_ref.dtype)

    n_blocks = rows // rb
    if gated:
        v_scr[...] = jnp.dot(a_ref[...], wv_ref[...], preferred_element_type=F32)
    for blk in range(n_blocks):
        project(blk)
        if blk >= 1:
            convolve(blk - 1)
    convolve(n_blocks - 1)


def _proj_conv(a, w, layer, conv_w, conv_b, *, width, vertical, n_out, val_offset=None, tn):
    m, k = a.shape
    rows = min(ROW_TILE, m)
    gated = val_offset is not None
    assert width & (width - 1) == 0 and rows % width == 0 and m % rows == 0
    pad = (width + SUBLANES) if vertical else SUBLANES
    taps = conv_w.shape[0]
    in_specs = [
        pl.BlockSpec((rows, k), lambda i, j: (i, 0)),
        pl.BlockSpec((None, k, tn), lambda i, j: (layer, 0, j)),
    ]
    operands = [a, w]
    padded = pltpu.VMEM((rows + 2 * pad, tn), F32)
    scratch = [padded] + ([padded, padded] if vertical else []) + ([pltpu.VMEM((rows, tn), F32)] if gated else [])
    if w.dtype != BF16:
        scratch += [pltpu.VMEM((k, tn), BF16)] * (2 if gated else 1)
    if gated:
        voff = val_offset // tn
        in_specs.append(pl.BlockSpec((None, k, tn), lambda i, j: (layer, 0, j + voff)))
        operands.append(w)
    in_specs += [
        pl.BlockSpec((taps, tn), lambda i, j: (0, j)),
        pl.BlockSpec((1, tn), lambda i, j: (0, j)),
    ]
    operands += [conv_w, conv_b.reshape(1, n_out)]
    return pl.pallas_call(
        functools.partial(_proj_conv_kernel, rows=rows, width=width, vertical=vertical, gated=gated),
        grid=(m // rows, n_out // tn),
        in_specs=in_specs,
        out_specs=pl.BlockSpec((rows, tn), lambda i, j: (i, j)),
        out_shape=jax.ShapeDtypeStruct((m, n_out), BF16),
        scratch_shapes=scratch,
        compiler_params=_cparams("parallel", "parallel"),
        name="proj_conv",
    )(*operands)


def _dft_tables(n):
    s = 64
    k = jnp.arange(n, dtype=jnp.int32)[:, None]

    def narrow(cols):
        ang = ((k * cols[None, :]) % n).astype(F32) * F32(2.0 * math.pi / n)
        return jnp.cos(ang), jnp.sin(ang)

    ca, sa = narrow(jnp.arange(n // s, dtype=jnp.int32) * s)
    cb, sb = narrow(jnp.arange(s, dtype=jnp.int32))
    scale = F32(1.0 / math.sqrt(n))
    cos = (ca[:, :, None] * cb[:, None, :] - sa[:, :, None] * sb[:, None, :]).reshape(n, n) * scale
    sin = (sa[:, :, None] * cb[:, None, :] + ca[:, :, None] * sb[:, None, :]).reshape(n, n) * scale
    return cos, sin


def _chan_dft_kernel(a_ref, wc_ref, o_ref, *, groups):
    dg = wc_ref.shape[0]
    wc = wc_ref[...]
    for g in range(groups):
        cols = pl.ds(g * dg, dg)
        res = jnp.dot(a_ref[:, cols], wc, preferred_element_type=F32)
        o_ref[0, :, cols] = res[:, :dg].astype(o_ref.dtype)
        o_ref[1, :, cols] = res[:, dg:].astype(o_ref.dtype)


def _chan_dft(a, wc, nb, ln):
    d = a.shape[1]
    dg = wc.shape[0]
    tm = min(512, ln)
    mt = ln // tm
    return pl.pallas_call(
        functools.partial(_chan_dft_kernel, groups=d // dg),
        grid=(nb, mt),
        in_specs=[
            pl.BlockSpec((tm, d), lambda b, i: (b * mt + i, 0)),
            pl.BlockSpec((dg, 2 * dg), lambda b, i: (0, 0)),
        ],
        out_specs=pl.BlockSpec((None, 2, tm, d), lambda b, i: (b, 0, i, 0)),
        out_shape=jax.ShapeDtypeStruct((nb, 2, ln, d), BF16),
        compiler_params=_cparams("parallel", "parallel"),
        name="chan_dft",
    )(a, wc)


def _pos_dft(cs, y):
    nb, k2, d = y.shape
    ln = cs.shape[0]
    tm, tn = min(1024, ln), 1024
    return pl.pallas_call(
        _mm_kernel,
        grid=(nb, d // tn, ln // tm),
        in_specs=[
            pl.BlockSpec((tm, k2), lambda b, j, i: (i, 0)),
            pl.BlockSpec((None, k2, tn), lambda b, j, i: (b, 0, j)),
        ],
        out_specs=pl.BlockSpec((None, tm, tn), lambda b, j, i: (b, i, j)),
        out_shape=jax.ShapeDtypeStruct((nb, ln, d), BF16),
        compiler_params=_cparams("parallel", "parallel", "parallel"),
        name="pos_dft",
    )(cs, y)


def _fourier(a, nb, ln, wc, cs):
    d = a.shape[1]
    y = _chan_dft(a, wc, nb, ln)
    f = _pos_dft(cs, y.reshape(nb, 2 * ln, d))
    return f.reshape(nb * ln, d)


def _ssd_dt_kernel(a_ref, wt_ref, bias_ref, alog_ref, dtt_ref, acst_ref, *, heads, groups):
    ln = a_ref.shape[0]
    q = SSD_CHUNK
    r = heads // groups
    dtt = jax.nn.softplus(
        lax.dot_general(wt_ref[...], a_ref[...], (((1,), (1,)), ((), ())), preferred_element_type=F32)
        + bias_ref[...])
    dtat = dtt * (-jnp.exp(alog_ref[...]))
    ri = lax.broadcasted_iota(jnp.int32, (q, q), 0)
    ci = lax.broadcasted_iota(jnp.int32, (q, q), 1)
    lower = (ri >= ci).astype(F32)
    upper = (ri <= ci).astype(F32)
    fwd_row = lax.broadcasted_iota(jnp.int32, (2 * heads, q), 0) < heads
    for c in range(ln // q):
        xt = dtat[:, c * q:(c + 1) * q]
        prefix = jnp.dot(xt, upper, precision=_HIGHEST, preferred_element_type=F32)
        suffix = jnp.dot(xt, lower, precision=_HIGHEST, preferred_element_type=F32)
        acs = jnp.where(fwd_row, prefix, suffix)
        dtc = dtt[:, c * q:(c + 1) * q]
        for g in range(groups):
            for half in range(2):
                src = slice(half * heads + g * r, half * heads + (g + 1) * r)
                dst = pl.ds(half * r, r)
                acst_ref[g, c, dst, :] = acs[src, :]
                dtt_ref[g, c, dst, :] = dtc[src, :]


def _ssd_dt(a, w_dt_t, dt_bias, a_log, nb, ln):
    d = a.shape[1]
    h2 = w_dt_t.shape[0]
    groups = SSD_GROUPS
    nc = ln // SSD_CHUNK
    vec = pl.BlockSpec((h2, 1), lambda b: (0, 0))
    out = pl.BlockSpec((None, groups, nc, h2 // groups, SSD_CHUNK), lambda b: (b, 0, 0, 0, 0))
    shape = jax.ShapeDtypeStruct((nb, groups, nc, h2 // groups, SSD_CHUNK), F32)
    return pl.pallas_call(
        functools.partial(_ssd_dt_kernel, heads=h2 // 2, groups=groups),
        grid=(nb,),
        in_specs=[
            pl.BlockSpec((ln, d), lambda b: (b, 0)),
            pl.BlockSpec((h2, d), lambda b: (0, 0)),
            vec, vec,
        ],
        out_specs=[out, out],
        out_shape=[shape, shape],
        compiler_params=_cparams("parallel"),
        name="ssd_dt",
    )(a, w_dt_t, dt_bias.reshape(h2, 1), a_log.reshape(h2, 1))


def _ssd_scan_kernel(*refs, nc, heads_per_group, emit):
    if emit:
        (xs_ref, b_ref, c_ref, z_ref, acst_ref, dtt_ref, h0_ref, d0_ref, d1_ref, ng_ref, o_ref, hfin_ref,
         yf_scr, h_scr, tok_scr) = refs
    else:
        xs_ref, b_ref, acst_ref, dtt_ref, h0_ref, hfin_ref, h_scr, tok_scr = refs
    q = SSD_CHUNK
    r_heads = heads_per_group
    p = SSD_HEAD_DIM
    gw = r_heads * p
    expand = (lax.broadcasted_iota(jnp.int32, (2 * r_heads, 2 * gw), 1) // p
              == lax.broadcasted_iota(jnp.int32, (2 * r_heads, 2 * gw), 0)).astype(BF16)
    li = lax.broadcasted_iota(jnp.int32, (q, q), 0)
    si = lax.broadcasted_iota(jnp.int32, (q, q), 1)
    lane = lax.broadcasted_iota(jnp.int32, (q, gw), 1)
    even_head = (lane % (2 * p)) < p
    pad_rows = jnp.zeros((q - 3 * r_heads, q), F32)
    if emit:
        dskip = d0_ref[...] + d1_ref[...]
        norm_g = ng_ref[...]

    def chunk(c, direction):
        rows = pl.ds(pl.multiple_of(c * q, q), q)
        xs_bf = xs_ref[rows, :]
        bc = b_ref[rows, :]
        tok = tok_scr[direction, c]
        fac = tok[:, :2 * r_heads]
        fac_hi = fac.astype(BF16)
        fac_lo = (fac - fac_hi.astype(F32)).astype(BF16)
        fac_e = (jnp.dot(fac_hi, expand, preferred_element_type=F32)
                 + jnp.dot(fac_lo, expand, preferred_element_type=F32))
        ea_e = fac_e[:, :gw]
        wf_e = fac_e[:, gw:]
        h_decay = ea_e[q - 1:q, :] if direction == 0 else ea_e[0:1, :]
        h = h_scr[...]

        def update_state():
            w = xs_bf * wf_e.astype(BF16)
            h_scr[...] = h * h_decay + lax.dot_general(
                bc, w, (((0,), (0,)), ((), ())), preferred_element_type=F32)

        if not emit:
            update_state()
            return
        cc = c_ref[rows, :]
        src_rows = acst_ref[c] - jnp.log(dtt_ref[c])
        causal = (li >= si) if direction == 0 else (li <= si)
        scores = lax.dot_general(cc, bc, (((1,), (1,)), ((), ())), preferred_element_type=F32)
        y = jnp.dot(cc, h.astype(BF16), preferred_element_type=F32) * ea_e
        xs_even = jnp.where(even_head, xs_bf, jnp.zeros_like(xs_bf))
        xs_odd = jnp.where(even_head, jnp.zeros_like(xs_bf), xs_bf)
        pieces = []
        for pair in range(r_heads // 2):
            ms = []
            for sub in range(2):
                r = 2 * pair + sub
                col = direction * r_heads + r
                seg = tok[:, 2 * r_heads + r:2 * r_heads + r + 1] - src_rows[col:col + 1, :]
                decay = jnp.exp(jnp.where(causal, seg, -jnp.inf))
                ms.append((scores * decay).astype(BF16))
            cols = slice(pair * 2 * p, (pair + 1) * 2 * p)
            pieces.append(jnp.dot(jnp.concatenate(ms, axis=1),
                                  jnp.concatenate([xs_even[:, cols], xs_odd[:, cols]], axis=0),
                                  preferred_element_type=F32))
        y = y + jnp.concatenate(pieces, axis=1)
        update_state()
        if direction == 0:
            yf_scr[rows, :] = y
        else:
            tot = yf_scr[rows, :] + y + dskip * xs_bf.astype(F32)
            gz = tot * _silu(z_ref[rows, :].astype(F32))
            gz = gz * lax.rsqrt(jnp.mean(gz * gz, axis=-1, keepdims=True) + EPS)
            o_ref[rows, :] = (gz * norm_g).astype(o_ref.dtype)

    def to_token_major(c, carry):
        acs_rows = acst_ref[c]
        dt_rows = dtt_ref[c]
        for direction in range(2):
            acs_d = acs_rows[direction * r_heads:(direction + 1) * r_heads, :]
            dt_d = dt_rows[direction * r_heads:(direction + 1) * r_heads, :]
            last = acs_d[:, q - 1:q] if direction == 0 else acs_d[:, 0:1]
            tok_scr[direction, c] = jnp.concatenate(
                [jnp.exp(acs_d), dt_d * jnp.exp(last - acs_d), acs_d, pad_rows], axis=0).T
        return carry

    lax.fori_loop(0, nc, to_token_major, 0, unroll=True)
    h_scr[...] = h0_ref[0]

    def fwd_body(c, carry):
        chunk(c, 0)
        return carry

    lax.fori_loop(0, nc, fwd_body, 0, unroll=min(SCAN_UNROLL, nc))
    hfin_ref[0] = h_scr[...]
    h_scr[...] = h0_ref[1]

    def bwd_body(i, carry):
        chunk(nc - 1 - i, 1)
        return carry

    lax.fori_loop(0, nc, bwd_body, 0, unroll=min(SCAN_UNROLL, nc))
    hfin_ref[1] = h_scr[...]


def _ssd_scan(xb, cm, z, acst4, dtt4, h0, d0e, d1e, norm_g, nb, ln):
    emit = z is not None
    groups = SSD_GROUPS
    n = SSD_STATE
    gw = h0.shape[-1]
    d_inner = gw * groups
    r2 = acst4.shape[-2]
    nc = ln // SSD_CHUNK
    b_blk = d_inner // n
    chunk_g = pl.BlockSpec((None, None, nc, r2, SSD_CHUNK), lambda b, g: (b, g, 0, 0, 0))
    state = pl.BlockSpec((None, None, 2, n, gw), lambda b, g: (b, g, 0, 0, 0))
    vec = pl.BlockSpec((1, gw), lambda b, g: (0, g))
    seq_x = pl.BlockSpec((ln, gw), lambda b, g: (b, g))
    seq_b = pl.BlockSpec((ln, n), lambda b, g: (b, b_blk + g))
    seq_c = pl.BlockSpec((ln, n), lambda b, g: (b, g))
    state_shape = jax.ShapeDtypeStruct(h0.shape, F32)
    scratch = [pltpu.VMEM((n, gw), F32), pltpu.VMEM((2, nc, SSD_CHUNK, SSD_CHUNK), F32)]
    if emit:
        in_specs = [seq_x, seq_b, seq_c, seq_x, chunk_g, chunk_g, state, vec, vec, vec]
        operands = (xb, xb, cm, z, acst4, dtt4, h0, d0e, d1e, norm_g)
        out_specs = [seq_x, state]
        out_shape = [jax.ShapeDtypeStruct((nb * ln, d_inner), BF16), state_shape]
        scratch = [pltpu.VMEM((ln, gw), F32)] + scratch
    else:
        in_specs = [seq_x, seq_b, chunk_g, chunk_g, state]
        operands = (xb, xb, acst4, dtt4, h0)
        out_specs = [state]
        out_shape = [state_shape]
    res = pl.pallas_call(
        functools.partial(_ssd_scan_kernel, nc=nc, heads_per_group=r2 // 2, emit=emit),
        grid=(nb, groups),
        in_specs=in_specs,
        out_specs=out_specs,
        out_shape=out_shape,
        scratch_shapes=scratch,
        compiler_params=_cparams("parallel", "parallel"),
        name="ssd_scan",
    )(*operands)
    return res if emit else (None, res[0])


def _ssd_mixer(a, nb, ln, weights, layer, h0, emit=True):
    w_in, w_c, w_z, w_dt_t, conv_w, conv_b, dt_bias, a_log, d0e, d1e, norm_g = weights
    d_inner = w_z.shape[2]
    n_xb = d_inner + SSD_GROUPS * SSD_STATE
    xb = _proj_conv(a, w_in, layer, conv_w[:, :n_xb], conv_b[:n_xb], width=ln, vertical=False, n_out=n_xb, tn=512)
    cm = z = None
    if emit:
        cm = _proj_conv(a, w_c, layer, conv_w[:, n_xb:], conv_b[n_xb:], width=ln, vertical=False,
                        n_out=w_c.shape[2], tn=512)
        z = _matmul(a, w_z, layer, BF16)
    dtt4, acst4 = _ssd_dt(a, w_dt_t, dt_bias, a_log, nb, ln)
    return _ssd_scan(xb, cm, z, acst4, dtt4, h0, d0e, d1e, norm_g, nb, ln)


def kernel(x, c, ctx, c_ctx, w_mod, b_mod, norm_mix_g, norm_ffn_g, four_w, ssd_w_in, ssd_conv_w, ssd_conv_b,
           ssd_dt_bias, ssd_a_log, ssd_d, ssd_norm_g, ssd_w_out, ffn_w_up, ffn_conv_w, ffn_conv_b, ffn_w_down,
           final_g):
    nb, ln, d = x.shape
    lc = ctx.shape[1]
    depth = w_mod.shape[0]
    d_ff = ffn_w_down.shape[1]
    d_inner = ssd_w_out.shape[1]
    heads = ssd_dt_bias.shape[2]
    gn = SSD_GROUPS * SSD_STATE
    t_lat, t_ctx = nb * ln, nb * lc

    mod_rows = -(-(nb + 1) // SUBLANES) * SUBLANES
    cond = jnp.zeros((mod_rows, d), F32).at[:nb].set(c).at[nb].set(c_ctx)
    mods = _modulation(cond, w_mod, b_mod)

    def lat_mod(i, j):
        return mods[i, :nb, j * d:(j + 1) * d].reshape(nb, 1, d)

    def ctx_mod(i, j):
        return mods[i, nb:nb + 1, j * d:(j + 1) * d].reshape(1, 1, d)

    four_bf = four_w.astype(BF16)
    w_down_bf = ffn_w_down.astype(BF16)
    w_out_bf = ssd_w_out.astype(BF16)
    xb = d_inner + gn
    state_cols = xb + 2 * heads
    w_c_bf = ssd_w_in[:, :, state_cols:state_cols + gn].astype(BF16)
    w_z_bf = ssd_w_in[:, :, state_cols + gn:].astype(BF16)
    w_dt_t_bf = ssd_w_in[:, :, xb:state_cols].transpose(0, 2, 1).astype(BF16)
    conv9 = ffn_conv_w.reshape(depth, 9, d_ff)

    dg = d // FOURIER_GROUPS
    cos_c, sin_c = _dft_tables(dg)
    wc = jnp.concatenate([cos_c, sin_c], axis=1).astype(BF16)
    cos_l, sin_l = _dft_tables(ln)
    cs_lat = jnp.concatenate([cos_l, -sin_l], axis=1).astype(BF16)
    cos_x, sin_x = _dft_tables(lc)
    cs_ctx = jnp.concatenate([cos_x, -sin_x], axis=1).astype(BF16)

    xl = x.reshape(t_lat, d)
    xc = ctx.reshape(t_ctx, d)
    a_lat = _normmod(xl, norm_mix_g[0], lat_mod(0, 0), lat_mod(0, 1), ln)
    a_ctx = _normmod(xc, norm_mix_g[0], ctx_mod(0, 0), ctx_mod(0, 1), t_ctx)
    out = None
    for i in range(depth):
        last = i == depth - 1
        is_ssd = i % 2 == 1
        j = i // 2
        if is_ssd:
            d0e = jnp.repeat(ssd_d[j, 0], SSD_HEAD_DIM).reshape(1, d_inner)
            d1e = jnp.repeat(ssd_d[j, 1], SSD_HEAD_DIM).reshape(1, d_inner)
            weights = (ssd_w_in, w_c_bf, w_z_bf, w_dt_t_bf[j], ssd_conv_w[j], ssd_conv_b[j], ssd_dt_bias[j], ssd_a_log[j],
                       d0e, d1e, ssd_norm_g[j].reshape(1, d_inner))
            zeros = jnp.zeros((nb, SSD_GROUPS, 2, SSD_STATE, d_inner // SSD_GROUPS), F32)
            mix_ctx, h_ctx = _ssd_mixer(a_ctx, nb, lc, weights, j, zeros, emit=not last)
            mix_lat, _ = _ssd_mixer(a_lat, nb, ln, weights, j, h_ctx)
            w_mix = w_out_bf
        else:
            mix_lat = _fourier(a_lat, nb, ln, wc, cs_lat)
            mix_ctx = None if last else _fourier(a_ctx, nb, lc, wc, cs_ctx)
            w_mix = four_bf
        xl, b_lat = _matmul_residual(mix_lat, w_mix, j, xl, lat_mod(i, 2), norm_ffn_g[i], lat_mod(i, 3),
                                     lat_mod(i, 4), ln)
        act = _proj_conv(b_lat, ffn_w_up, i, conv9[i], ffn_conv_b[i], width=GRID_W, vertical=True,
                         n_out=d_ff, val_offset=d_ff, tn=256)
        if last:
            zero_mod = jnp.zeros((nb, 1, d), F32)
            _, out = _matmul_residual(act, w_down_bf, i, xl, lat_mod(i, 5), final_g, zero_mod, zero_mod, ln,
                                      modulate=False, a_dtype=F32)
        else:
            xl, a_lat = _matmul_residual(act, w_down_bf, i, xl, lat_mod(i, 5), norm_mix_g[i + 1],
                                         lat_mod(i + 1, 0), lat_mod(i + 1, 1), ln)
            xc, b_ctx = _matmul_residual(mix_ctx, w_mix, j, xc, ctx_mod(i, 2), norm_ffn_g[i], ctx_mod(i, 3),
                                         ctx_mod(i, 4), t_ctx)
            act_c = _proj_conv(b_ctx, ffn_w_up, i, ffn_conv_w[i, 1], ffn_conv_b[i], width=lc, vertical=False,
                               n_out=d_ff, val_offset=d_ff, tn=256)
            xc, a_ctx = _matmul_residual(act_c, w_down_bf, i, xc, ctx_mod(i, 5), norm_mix_g[i + 1],
                                         ctx_mod(i + 1, 0), ctx_mod(i + 1, 1), t_ctx)
    return out.reshape(nb, ln, d)
```

```python
import functools
import math

import jax
import jax.numpy as jnp
from jax import lax
from jax.experimental import pallas as pl
from jax.experimental.pallas import tpu as pltpu

F32 = jnp.float32
BF16 = jnp.bfloat16

EPS = 1e-6
GRID_W = 64
FOURIER_GROUPS = 8
SSD_HEAD_DIM = 64
SSD_GROUPS = 8
SSD_STATE = 128
SSD_CHUNK = 128
SCAN_UNROLL = 16

LANES = 128
SUBLANES = 8
VMEM_LIMIT_BYTES = 56 * 1024 * 1024
ROW_TILE = 2048
EPILOGUE_ROWS = 256
PROJ_BLOCK_ROWS = 512

_HIGHEST = lax.Precision.HIGHEST


def _cparams(*sem, flags=None):
    return pltpu.CompilerParams(dimension_semantics=sem, vmem_limit_bytes=VMEM_LIMIT_BYTES, flags=flags)


def _norm_rows(x, g, shift, scale, modulate):
    ms = jnp.mean(x * x, axis=-1, keepdims=True)
    y = x * lax.rsqrt(ms + EPS) * g
    if modulate:
        y = y * (1.0 + scale) + shift
    return y


def _silu(v):
    return v * jax.nn.sigmoid(v)


def _mod_kernel(s_ref, w_ref, b_ref, o_ref):
    s = _silu(s_ref[...]).astype(BF16)
    w = w_ref[0].astype(BF16)
    o_ref[0] = jnp.dot(s, w, preferred_element_type=F32) + b_ref[0]


def _modulation(cond, w_mod, b_mod):
    depth, d, n = w_mod.shape
    rows = cond.shape[0]
    tn = 1024
    return pl.pallas_call(
        _mod_kernel,
        grid=(depth, n // tn),
        in_specs=[
            pl.BlockSpec((rows, d), lambda i, j: (0, 0)),
            pl.BlockSpec((1, d, tn), lambda i, j: (i, 0, j)),
            pl.BlockSpec((1, 1, tn), lambda i, j: (i, 0, j)),
        ],
        out_specs=pl.BlockSpec((1, rows, tn), lambda i, j: (i, 0, j)),
        out_shape=jax.ShapeDtypeStruct((depth, rows, n), F32),
        compiler_params=_cparams("parallel", "parallel"),
        name="modulation",
    )(cond, w_mod, b_mod.reshape(depth, 1, n))


def _normmod_kernel(x_ref, g_ref, sh_ref, sc_ref, o_ref):
    o_ref[...] = _norm_rows(x_ref[...], g_ref[...], sh_ref[0], sc_ref[0], True).astype(o_ref.dtype)


def _normmod(x, g, shift, scale, rows_per_mod):
    m, d = x.shape
    tm = 512
    mod_spec = pl.BlockSpec((1, 1, d), lambda i: ((i * tm) // rows_per_mod, 0, 0))
    return pl.pallas_call(
        _normmod_kernel,
        grid=(m // tm,),
        in_specs=[
            pl.BlockSpec((tm, d), lambda i: (i, 0)),
            pl.BlockSpec((1, d), lambda i: (0, 0)),
            mod_spec,
            mod_spec,
        ],
        out_specs=pl.BlockSpec((tm, d), lambda i: (i, 0)),
        out_shape=jax.ShapeDtypeStruct((m, d), BF16),
        compiler_params=_cparams("parallel"),
        name="normmod",
    )(x, g.reshape(1, d), shift, scale)


def _mm_kernel(a_ref, w_ref, o_ref):
    o_ref[...] = jnp.dot(a_ref[...], w_ref[...], preferred_element_type=F32).astype(o_ref.dtype)


def _matmul(a, w, layer, out_dtype, tm=1024, tn=1024):
    m, k = a.shape
    n = w.shape[2]
    tm, tn = min(tm, m), min(tn, n)
    return pl.pallas_call(
        _mm_kernel,
        grid=(m // tm, n // tn),
        in_specs=[
            pl.BlockSpec((tm, k), lambda i, j: (i, 0)),
            pl.BlockSpec((None, k, tn), lambda i, j: (layer, 0, j)),
        ],
        out_specs=pl.BlockSpec((tm, tn), lambda i, j: (i, j)),
        out_shape=jax.ShapeDtypeStruct((m, n), out_dtype),
        compiler_params=_cparams("parallel", "parallel"),
        name="matmul",
    )(a, w)


def _mm_res_kernel(a_ref, w_ref, x_ref, gate_ref, g_ref, sh_ref, sc_ref, xo_ref, ao_ref, xn_scr, ssq_scr,
                   *, nn, modulate):
    j = pl.program_id(1)
    tn = x_ref.shape[1]
    w = w_ref[:, pl.ds(pl.multiple_of(j * tn, tn), tn)]
    xn = x_ref[...] + gate_ref[0] * jnp.dot(a_ref[...], w, preferred_element_type=F32)
    xo_ref[...] = xn
    xn_scr[j] = xn
    ssq = jnp.sum(xn * xn, axis=-1, keepdims=True)

    @pl.when(j == 0)
    def _():
        ssq_scr[...] = ssq

    @pl.when(j > 0)
    def _():
        ssq_scr[...] += ssq

    @pl.when(j == nn - 1)
    def _():
        inv = lax.rsqrt(ssq_scr[...] * (1.0 / (nn * tn)) + EPS)
        for c in range(nn):
            cols = slice(c * tn, (c + 1) * tn)
            y = xn_scr[c] * inv * g_ref[:, cols]
            if modulate:
                y = y * (1.0 + sc_ref[0][:, cols]) + sh_ref[0][:, cols]
            ao_ref[:, cols] = y.astype(ao_ref.dtype)


def _residual_tile_cols(k):
    return 1024 if k <= 4096 else 512


def _matmul_residual(a, w, layer, x, gate, g_next, shift, scale, rows_per_mod, modulate=True, a_dtype=BF16):
    m, k = a.shape
    d = w.shape[2]
    tn = _residual_tile_cols(k)
    nn = d // tn
    tm = min(512, m)

    def mod_row(i, j):
        return ((i * tm) // rows_per_mod, 0, 0)

    mod_spec = pl.BlockSpec((1, 1, d), mod_row)
    return pl.pallas_call(
        functools.partial(_mm_res_kernel, nn=nn, modulate=modulate),
        grid=(m // tm, nn),
        in_specs=[
            pl.BlockSpec((tm, k), lambda i, j: (i, 0)),
            pl.BlockSpec((None, k, d), lambda i, j: (layer, 0, 0), pipeline_mode=pl.Buffered(1)),
            pl.BlockSpec((tm, tn), lambda i, j: (i, j)),
            pl.BlockSpec((1, 1, tn), lambda i, j: ((i * tm) // rows_per_mod, 0, j)),
            pl.BlockSpec((1, d), lambda i, j: (0, 0)),
            mod_spec,
            mod_spec,
        ],
        out_specs=[
            pl.BlockSpec((tm, tn), lambda i, j: (i, j)),
            pl.BlockSpec((tm, d), lambda i, j: (i, 0)),
        ],
        out_shape=[
            jax.ShapeDtypeStruct((m, d), F32),
            jax.ShapeDtypeStruct((m, d), a_dtype),
        ],
        scratch_shapes=[pltpu.VMEM((nn, tm, tn), F32), pltpu.VMEM((tm, 1), F32)],
        compiler_params=_cparams("parallel", "arbitrary"),
        name="matmul_residual",
    )(a, w, x, gate, g_next.reshape(1, d), shift, scale)


def _proj_conv_kernel(*refs, rows, width, vertical, gated):
    refs = list(refs)
    a_ref, wg_ref = refs[:2]
    wv_ref = refs.pop(2) if gated else None
    cw_ref, cb_ref, o_ref, u_scr = refs[2:6]
    scratch = refs[6:]
    if vertical:
        (ul_scr, ur_scr), scratch = scratch[:2], scratch[2:]
    if gated:
        v_scr, scratch = scratch[0], scratch[1:]
    if wg_ref.dtype != BF16:
        scratch[0][...] = wg_ref[...].astype(BF16)
        wg_ref = scratch[0]
        if gated:
            scratch[1][...] = wv_ref[...].astype(BF16)
            wv_ref = scratch[1]
    tn = o_ref.shape[1]
    pad = (u_scr.shape[0] - rows) // 2
    rb = min(PROJ_BLOCK_ROWS, rows)
    rc = min(EPILOGUE_ROWS, rb)
    zeros = jnp.zeros((pad, tn), F32)
    for buf in [u_scr] + ([ul_scr, ur_scr] if vertical else []):
        buf[pl.ds(0, pad), :] = zeros
        buf[pl.ds(pad + rows, pad), :] = zeros
    cw = cw_ref[...]
    bias = cb_ref[...]
    row = lax.broadcasted_iota(jnp.int32, (rc, tn), 0)
    everything = slice(0, tn)

    def project(blk, cols=everything):
        a = a_ref[pl.ds(blk * rb, rb), :]
        u_scr[pl.ds(pad + blk * rb, rb), cols] = jnp.dot(a, wg_ref[:, cols], preferred_element_type=F32)
        if vertical:
            col = row & (width - 1)
            for r in range(blk * rb // rc, (blk + 1) * rb // rc):
                chunk = u_scr[pl.ds(pad + r * rc, rc), :]
                ul_scr[pl.ds(pad + r * rc, rc), :] = jnp.where(col != 0, pltpu.roll(chunk, 1, 0), 0.0)
                ur_scr[pl.ds(pad + r * rc, rc), :] = jnp.where(col != width - 1, pltpu.roll(chunk, rc - 1, 0), 0.0)

    def convolve(blk, cols=everything):
        for r in range(blk * rb // rc, (blk + 1) * rb // rc):
            base = pad + r * rc
            if vertical:
                acc = bias
                for di in range(3):
                    off = base + (di - 1) * width
                    acc = acc + ul_scr[pl.ds(off, rc), :] * cw[3 * di:3 * di + 1, :]
                    acc = acc + u_scr[pl.ds(off, rc), :] * cw[3 * di + 1:3 * di + 2, :]
                    acc = acc + ur_scr[pl.ds(off, rc), :] * cw[3 * di + 2:3 * di + 3, :]
            else:
                col = (lax.broadcasted_iota(jnp.int32, (rc, cols.stop - cols.start), 0) + r * rc) & (width - 1)
                left = jnp.where(col != 0, u_scr[pl.ds(base - 1, rc), cols], 0.0)
                right = jnp.where(col != width - 1, u_scr[pl.ds(base + 1, rc), cols], 0.0)
                acc = (left * cw_ref[0:1, cols] + u_scr[pl.ds(base, rc), cols] * cw_ref[1:2, cols]
                       + right * cw_ref[2:3, cols] + cb_ref[:, cols])
            y = _silu(acc)
            if gated:
                y = y * v_scr[pl.ds(r * rc, rc), :]
            o_ref[pl.ds(r * rc, rc), cols] = y.astype(o_ref.dtype)

    n_blocks = rows // rb
    if gated:
        v_scr[...] = jnp.dot(a_ref[...], wv_ref[...], preferred_element_type=F32)
    for blk in range(n_blocks):
        project(blk)
        if blk >= 1:
            convolve(blk - 1)
    convolve(n_blocks - 1)


def _proj_conv(a, w, layer, conv_w, conv_b, *, width, vertical, n_out, val_offset=None, tn):
    m, k = a.shape
    rows = min(ROW_TILE, m)
    gated = val_offset is not None
    assert width & (width - 1) == 0 and rows % width == 0 and m % rows == 0
    pad = (width + SUBLANES) if vertical else SUBLANES
    taps = conv_w.shape[0]
    in_specs = [
        pl.BlockSpec((rows, k), lambda i, j: (i, 0)),
        pl.BlockSpec((None, k, tn), lambda i, j: (layer, 0, j)),
    ]
    operands = [a, w]
    padded = pltpu.VMEM((rows + 2 * pad, tn), F32)
    scratch = [padded] + ([padded, padded] if vertical else []) + ([pltpu.VMEM((rows, tn), F32)] if gated else [])
    if w.dtype != BF16:
        scratch += [pltpu.VMEM((k, tn), BF16)] * (2 if gated else 1)
    if gated:
        voff = val_offset // tn
        in_specs.append(pl.BlockSpec((None, k, tn), lambda i, j: (layer, 0, j + voff)))
        operands.append(w)
    in_specs += [
        pl.BlockSpec((taps, tn), lambda i, j: (0, j)),
        pl.BlockSpec((1, tn), lambda i, j: (0, j)),
    ]
    operands += [conv_w, conv_b.reshape(1, n_out)]
    return pl.pallas_call(
        functools.partial(_proj_conv_kernel, rows=rows, width=width, vertical=vertical, gated=gated),
        grid=(m // rows, n_out // tn),
        in_specs=in_specs,
        out_specs=pl.BlockSpec((rows, tn), lambda i, j: (i, j)),
        out_shape=jax.ShapeDtypeStruct((m, n_out), BF16),
        scratch_shapes=scratch,
        compiler_params=_cparams("parallel", "parallel"),
        name="proj_conv",
    )(*operands)


def _dft_tables(n, sin_sign):
    s = 64
    k = jnp.arange(n, dtype=jnp.int32)[:, None]

    def narrow(cols):
        ang = ((k * cols[None, :]) % n).astype(F32) * F32(2.0 * math.pi / n)
        return jnp.cos(ang), jnp.sin(ang)

    ca, sa = narrow(jnp.arange(n // s, dtype=jnp.int32) * s)
    cb, sb = narrow(jnp.arange(s, dtype=jnp.int32))
    scale = F32(1.0 / math.sqrt(n))
    p = jnp.stack([ca, sin_sign * sa], axis=1)[:, :, :, None] * scale
    q = jnp.stack([-sa, sin_sign * ca], axis=1)[:, :, :, None] * scale
    return (p * cb[:, None, None, :] + q * sb[:, None, None, :]).reshape(n, 2 * n)


def _chan_dft_kernel(a_ref, wc_ref, o_ref, *, groups):
    dg = wc_ref.shape[0]
    wc = wc_ref[...]
    for g in range(groups):
        cols = pl.ds(g * dg, dg)
        res = jnp.dot(a_ref[:, cols], wc, preferred_element_type=F32)
        o_ref[0, :, cols] = res[:, :dg].astype(o_ref.dtype)
        o_ref[1, :, cols] = res[:, dg:].astype(o_ref.dtype)


def _chan_dft(a, wc, nb, ln):
    d = a.shape[1]
    dg = wc.shape[0]
    tm = min(512, ln)
    mt = ln // tm
    return pl.pallas_call(
        functools.partial(_chan_dft_kernel, groups=d // dg),
        grid=(nb, mt),
        in_specs=[
            pl.BlockSpec((tm, d), lambda b, i: (b * mt + i, 0)),
            pl.BlockSpec((dg, 2 * dg), lambda b, i: (0, 0)),
        ],
        out_specs=pl.BlockSpec((None, 2, tm, d), lambda b, i: (b, 0, i, 0)),
        out_shape=jax.ShapeDtypeStruct((nb, 2, ln, d), BF16),
        compiler_params=_cparams("parallel", "parallel"),
        name="chan_dft",
    )(a, wc)


def _pos_dft(cs, y):
    nb, k2, d = y.shape
    ln = cs.shape[0]
    tm, tn = min(1024, ln), 1024
    return pl.pallas_call(
        _mm_kernel,
        grid=(nb, d // tn, ln // tm),
        in_specs=[
            pl.BlockSpec((tm, k2), lambda b, j, i: (i, 0)),
            pl.BlockSpec((None, k2, tn), lambda b, j, i: (b, 0, j)),
        ],
        out_specs=pl.BlockSpec((None, tm, tn), lambda b, j, i: (b, i, j)),
        out_shape=jax.ShapeDtypeStruct((nb, ln, d), BF16),
        compiler_params=_cparams("parallel", "parallel", "parallel"),
        name="pos_dft",
    )(cs, y)


def _fourier(a, nb, ln, wc, cs):
    d = a.shape[1]
    y = _chan_dft(a, wc, nb, ln)
    f = _pos_dft(cs, y.reshape(nb, 2 * ln, d))
    return f.reshape(nb * ln, d)


def _ssd_dt_kernel(a_ref, wt_ref, bias_ref, alog_ref, dtt_ref, acst_ref, *, heads, groups):
    ln = a_ref.shape[0]
    q = SSD_CHUNK
    r = heads // groups
    dtt = jax.nn.softplus(
        lax.dot_general(wt_ref[...], a_ref[...], (((1,), (1,)), ((), ())), preferred_element_type=F32)
        + bias_ref[...])
    dtat = dtt * (-jnp.exp(alog_ref[...]))
    ri = lax.broadcasted_iota(jnp.int32, (q, q), 0)
    ci = lax.broadcasted_iota(jnp.int32, (q, q), 1)
    lower = (ri >= ci).astype(F32)
    upper = (ri <= ci).astype(F32)
    fwd_row = lax.broadcasted_iota(jnp.int32, (2 * heads, q), 0) < heads
    for c in range(ln // q):
        xt = dtat[:, c * q:(c + 1) * q]
        prefix = jnp.dot(xt, upper, precision=_HIGHEST, preferred_element_type=F32)
        suffix = jnp.dot(xt, lower, precision=_HIGHEST, preferred_element_type=F32)
        acs = jnp.where(fwd_row, prefix, suffix)
        dtc = dtt[:, c * q:(c + 1) * q]
        for g in range(groups):
            for half in range(2):
                src = slice(half * heads + g * r, half * heads + (g + 1) * r)
                dst = pl.ds(half * r, r)
                acst_ref[g, c, dst, :] = acs[src, :]
                dtt_ref[g, c, dst, :] = dtc[src, :]


def _ssd_dt(a, w_dt_t, dt_bias, a_log, nb, ln):
    d = a.shape[1]
    h2 = w_dt_t.shape[0]
    groups = SSD_GROUPS
    nc = ln // SSD_CHUNK
    vec = pl.BlockSpec((h2, 1), lambda b: (0, 0))
    out = pl.BlockSpec((None, groups, nc, h2 // groups, SSD_CHUNK), lambda b: (b, 0, 0, 0, 0))
    shape = jax.ShapeDtypeStruct((nb, groups, nc, h2 // groups, SSD_CHUNK), F32)
    return pl.pallas_call(
        functools.partial(_ssd_dt_kernel, heads=h2 // 2, groups=groups),
        grid=(nb,),
        in_specs=[
            pl.BlockSpec((ln, d), lambda b: (b, 0)),
            pl.BlockSpec((h2, d), lambda b: (0, 0)),
            vec, vec,
        ],
        out_specs=[out, out],
        out_shape=[shape, shape],
        compiler_params=_cparams("parallel"),
        name="ssd_dt",
    )(a, w_dt_t, dt_bias.reshape(h2, 1), a_log.reshape(h2, 1))


def _ssd_scan_kernel(*refs, nc, heads_per_group, emit):
    if emit:
        (xs_ref, b_ref, c_ref, z_ref, acst_ref, dtt_ref, h0_ref, d0_ref, d1_ref, ng_ref, o_ref, hfin_ref,
         yf_scr, h_scr, tok_scr) = refs
    else:
        xs_ref, b_ref, acst_ref, dtt_ref, h0_ref, hfin_ref, h_scr, tok_scr = refs
    q = SSD_CHUNK
    r_heads = heads_per_group
    p = SSD_HEAD_DIM
    gw = r_heads * p
    expand = (lax.broadcasted_iota(jnp.int32, (2 * r_heads, 2 * gw), 1) // p
              == lax.broadcasted_iota(jnp.int32, (2 * r_heads, 2 * gw), 0)).astype(BF16)
    li = lax.broadcasted_iota(jnp.int32, (q, q), 0)
    si = lax.broadcasted_iota(jnp.int32, (q, q), 1)
    lane = lax.broadcasted_iota(jnp.int32, (q, gw), 1)
    even_head = (lane % (2 * p)) < p
    pad_rows = jnp.zeros((q - 3 * r_heads, q), F32)
    if emit:
        dskip = d0_ref[...] + d1_ref[...]
        norm_g = ng_ref[...]

    def chunk(c, direction):
        rows = pl.ds(pl.multiple_of(c * q, q), q)
        xs_bf = xs_ref[rows, :]
        bc = b_ref[rows, :]
        tok = tok_scr[direction, c]
        fac = tok[:, :2 * r_heads]
        fac_hi = fac.astype(BF16)
        fac_lo = (fac - fac_hi.astype(F32)).astype(BF16)
        fac_e = (jnp.dot(fac_hi, expand, preferred_element_type=F32)
                 + jnp.dot(fac_lo, expand, preferred_element_type=F32))
        ea_e = fac_e[:, :gw]
        wf_e = fac_e[:, gw:]
        h_decay = ea_e[q - 1:q, :] if direction == 0 else ea_e[0:1, :]
        h = h_scr[...]

        def update_state():
            w = xs_bf * wf_e.astype(BF16)
            h_scr[...] = h * h_decay + lax.dot_general(
                bc, w, (((0,), (0,)), ((), ())), preferred_element_type=F32)

        if not emit:
            update_state()
            return
        cc = c_ref[rows, :]
        src_rows = acst_ref[c] - jnp.log(dtt_ref[c])
        causal = (li >= si) if direction == 0 else (li <= si)
        scores = lax.dot_general(cc, bc, (((1,), (1,)), ((), ())), preferred_element_type=F32)
        y = jnp.dot(cc, h.astype(BF16), preferred_element_type=F32) * ea_e
        xs_even = jnp.where(even_head, xs_bf, jnp.zeros_like(xs_bf))
        xs_odd = jnp.where(even_head, jnp.zeros_like(xs_bf), xs_bf)
        pieces = []
        for pair in range(r_heads // 2):
            ms = []
            for sub in range(2):
                r = 2 * pair + sub
                col = direction * r_heads + r
                seg = tok[:, 2 * r_heads + r:2 * r_heads + r + 1] - src_rows[col:col + 1, :]
                decay = jnp.exp(jnp.where(causal, seg, -jnp.inf))
                ms.append((scores * decay).astype(BF16))
            cols = slice(pair * 2 * p, (pair + 1) * 2 * p)
            pieces.append(jnp.dot(jnp.concatenate(ms, axis=1),
                                  jnp.concatenate([xs_even[:, cols], xs_odd[:, cols]], axis=0),
                                  preferred_element_type=F32))
        y = y + jnp.concatenate(pieces, axis=1)
        update_state()
        if direction == 0:
            yf_scr[rows, :] = y
        else:
            tot = yf_scr[rows, :] + y + dskip * xs_bf.astype(F32)
            gz = tot * _silu(z_ref[rows, :].astype(F32))
            gz = gz * lax.rsqrt(jnp.mean(gz * gz, axis=-1, keepdims=True) + EPS)
            o_ref[rows, :] = (gz * norm_g).astype(o_ref.dtype)

    def to_token_major(c, carry):
        acs_rows = acst_ref[c]
        dt_rows = dtt_ref[c]
        for direction in range(2):
            acs_d = acs_rows[direction * r_heads:(direction + 1) * r_heads, :]
            dt_d = dt_rows[direction * r_heads:(direction + 1) * r_heads, :]
            last = acs_d[:, q - 1:q] if direction == 0 else acs_d[:, 0:1]
            tok_scr[direction, c] = jnp.concatenate(
                [jnp.exp(acs_d), dt_d * jnp.exp(last - acs_d), acs_d, pad_rows], axis=0).T
        return carry

    lax.fori_loop(0, nc, to_token_major, 0, unroll=True)
    h_scr[...] = h0_ref[0]

    def fwd_body(c, carry):
        chunk(c, 0)
        return carry

    lax.fori_loop(0, nc, fwd_body, 0, unroll=min(SCAN_UNROLL, nc))
    hfin_ref[0] = h_scr[...]
    h_scr[...] = h0_ref[1]

    def bwd_body(i, carry):
        chunk(nc - 1 - i, 1)
        return carry

    lax.fori_loop(0, nc, bwd_body, 0, unroll=min(SCAN_UNROLL, nc))
    hfin_ref[1] = h_scr[...]


def _ssd_scan(xb, cm, z, acst4, dtt4, h0, d0e, d1e, norm_g, nb, ln):
    emit = z is not None
    groups = SSD_GROUPS
    n = SSD_STATE
    gw = h0.shape[-1]
    d_inner = gw * groups
    r2 = acst4.shape[-2]
    nc = ln // SSD_CHUNK
    b_blk = d_inner // n
    chunk_g = pl.BlockSpec((None, None, nc, r2, SSD_CHUNK), lambda b, g: (b, g, 0, 0, 0))
    state = pl.BlockSpec((None, None, 2, n, gw), lambda b, g: (b, g, 0, 0, 0))
    vec = pl.BlockSpec((1, gw), lambda b, g: (0, g))
    seq_x = pl.BlockSpec((ln, gw), lambda b, g: (b, g))
    seq_b = pl.BlockSpec((ln, n), lambda b, g: (b, b_blk + g))
    seq_c = pl.BlockSpec((ln, n), lambda b, g: (b, g))
    state_shape = jax.ShapeDtypeStruct(h0.shape, F32)
    scratch = [pltpu.VMEM((n, gw), F32), pltpu.VMEM((2, nc, SSD_CHUNK, SSD_CHUNK), F32)]
    if emit:
        in_specs = [seq_x, seq_b, seq_c, seq_x, chunk_g, chunk_g, state, vec, vec, vec]
        operands = (xb, xb, cm, z, acst4, dtt4, h0, d0e, d1e, norm_g)
        out_specs = [seq_x, state]
        out_shape = [jax.ShapeDtypeStruct((nb * ln, d_inner), BF16), state_shape]
        scratch = [pltpu.VMEM((ln, gw), F32)] + scratch
    else:
        in_specs = [seq_x, seq_b, chunk_g, chunk_g, state]
        operands = (xb, xb, acst4, dtt4, h0)
        out_specs = [state]
        out_shape = [state_shape]
    res = pl.pallas_call(
        functools.partial(_ssd_scan_kernel, nc=nc, heads_per_group=r2 // 2, emit=emit),
        grid=(nb, groups),
        in_specs=in_specs,
        out_specs=out_specs,
        out_shape=out_shape,
        scratch_shapes=scratch,
        compiler_params=_cparams("parallel", "parallel"),
        name="ssd_scan",
    )(*operands)
    return res if emit else (None, res[0])


def _ssd_mixer(a, nb, ln, weights, layer, h0, emit=True):
    w_in, w_c, w_z, w_dt_t, conv_w, conv_b, dt_bias, a_log, d0e, d1e, norm_g = weights
    d_inner = w_z.shape[2]
    n_xb = d_inner + SSD_GROUPS * SSD_STATE
    xb = _proj_conv(a, w_in, layer, conv_w[:, :n_xb], conv_b[:n_xb], width=ln, vertical=False, n_out=n_xb, tn=512)
    cm = z = None
    if emit:
        cm = _proj_conv(a, w_c, layer, conv_w[:, n_xb:], conv_b[n_xb:], width=ln, vertical=False,
                        n_out=w_c.shape[2], tn=512)
        z = _matmul(a, w_z, layer, BF16)
    dtt4, acst4 = _ssd_dt(a, w_dt_t, dt_bias, a_log, nb, ln)
    return _ssd_scan(xb, cm, z, acst4, dtt4, h0, d0e, d1e, norm_g, nb, ln)


def kernel(x, c, ctx, c_ctx, w_mod, b_mod, norm_mix_g, norm_ffn_g, four_w, ssd_w_in, ssd_conv_w, ssd_conv_b,
           ssd_dt_bias, ssd_a_log, ssd_d, ssd_norm_g, ssd_w_out, ffn_w_up, ffn_conv_w, ffn_conv_b, ffn_w_down,
           final_g):
    nb, ln, d = x.shape
    lc = ctx.shape[1]
    depth = w_mod.shape[0]
    d_ff = ffn_w_down.shape[1]
    d_inner = ssd_w_out.shape[1]
    heads = ssd_dt_bias.shape[2]
    gn = SSD_GROUPS * SSD_STATE
    t_lat, t_ctx = nb * ln, nb * lc

    mod_rows = -(-(nb + 1) // SUBLANES) * SUBLANES
    cond = jnp.zeros((mod_rows, d), F32).at[:nb].set(c).at[nb].set(c_ctx)
    mods = _modulation(cond, w_mod, b_mod)

    def lat_mod(i, j):
        return mods[i, :nb, j * d:(j + 1) * d].reshape(nb, 1, d)

    def ctx_mod(i, j):
        return mods[i, nb:nb + 1, j * d:(j + 1) * d].reshape(1, 1, d)

    four_bf = four_w.astype(BF16)
    w_down_bf = ffn_w_down.astype(BF16)
    w_out_bf = ssd_w_out.astype(BF16)
    xb = d_inner + gn
    state_cols = xb + 2 * heads
    w_c_bf = ssd_w_in[:, :, state_cols:state_cols + gn].astype(BF16)
    w_z_bf = ssd_w_in[:, :, state_cols + gn:].astype(BF16)
    w_dt_t_bf = lax.optimization_barrier(ssd_w_in[:, :, xb:state_cols]).transpose(0, 2, 1).astype(BF16)
    conv9 = ffn_conv_w.reshape(depth, 9, d_ff)

    dg = d // FOURIER_GROUPS
    wc = _dft_tables(dg, 1.0).astype(BF16)
    cs_lat = _dft_tables(ln, -1.0).astype(BF16)
    cs_ctx = _dft_tables(lc, -1.0).astype(BF16)

    xl = x.reshape(t_lat, d)
    xc = ctx.reshape(t_ctx, d)
    a_lat = _normmod(xl, norm_mix_g[0], lat_mod(0, 0), lat_mod(0, 1), ln)
    a_ctx = _normmod(xc, norm_mix_g[0], ctx_mod(0, 0), ctx_mod(0, 1), t_ctx)
    out = None
    for i in range(depth):
        last = i == depth - 1
        is_ssd = i % 2 == 1
        j = i // 2
        if is_ssd:
            d0e = jnp.repeat(ssd_d[j, 0], SSD_HEAD_DIM).reshape(1, d_inner)
            d1e = jnp.repeat(ssd_d[j, 1], SSD_HEAD_DIM).reshape(1, d_inner)
            weights = (ssd_w_in, w_c_bf, w_z_bf, w_dt_t_bf[j], ssd_conv_w[j], ssd_conv_b[j], ssd_dt_bias[j], ssd_a_log[j],
                       d0e, d1e, ssd_norm_g[j].reshape(1, d_inner))
            zeros = jnp.zeros((nb, SSD_GROUPS, 2, SSD_STATE, d_inner // SSD_GROUPS), F32)
            mix_ctx, h_ctx = _ssd_mixer(a_ctx, nb, lc, weights, j, zeros, emit=not last)
            mix_lat, _ = _ssd_mixer(a_lat, nb, ln, weights, j, h_ctx)
            w_mix = w_out_bf
        else:
            mix_lat = _fourier(a_lat, nb, ln, wc, cs_lat)
            mix_ctx = None if last else _fourier(a_ctx, nb, lc, wc, cs_ctx)
            w_mix = four_bf
        xl, b_lat = _matmul_residual(mix_lat, w_mix, j, xl, lat_mod(i, 2), norm_ffn_g[i], lat_mod(i, 3),
                                     lat_mod(i, 4), ln)
        act = _proj_conv(b_lat, ffn_w_up, i, conv9[i], ffn_conv_b[i], width=GRID_W, vertical=True,
                         n_out=d_ff, val_offset=d_ff, tn=256)
        if last:
            zero_mod = jnp.zeros((nb, 1, d), F32)
            _, out = _matmul_residual(act, w_down_bf, i, xl, lat_mod(i, 5), final_g, zero_mod, zero_mod, ln,
                                      modulate=False, a_dtype=F32)
        else:
            xl, a_lat = _matmul_residual(act, w_down_bf, i, xl, lat_mod(i, 5), norm_mix_g[i + 1],
                                         lat_mod(i + 1, 0), lat_mod(i + 1, 1), ln)
            xc, b_ctx = _matmul_residual(mix_ctx, w_mix, j, xc, ctx_mod(i, 2), norm_ffn_g[i], ctx_mod(i, 3),
                                         ctx_mod(i, 4), t_ctx)
            act_c = _proj_conv(b_ctx, ffn_w_up, i, ffn_conv_w[i, 1], ffn_conv_b[i], width=lc, vertical=False,
                               n_out=d_ff, val_offset=d_ff, tn=256)
            xc, a_ctx = _matmul_residual(act_c, w_down_bf, i, xc, ctx_mod(i, 5), norm_mix_g[i + 1],
                                         ctx_mod(i + 1, 0), ctx_mod(i + 1, 1), t_ctx)
    return out.reshape(nb, ln, d)
```

```python
import functools
import math

import jax
import jax.numpy as jnp
from jax import lax
from jax.experimental import pallas as pl
from jax.experimental.pallas import tpu as pltpu

F32 = jnp.float32
BF16 = jnp.bfloat16

EPS = 1e-6
GRID_W = 64
FOURIER_GROUPS = 8
SSD_HEAD_DIM = 64
SSD_GROUPS = 8
SSD_STATE = 128
SSD_CHUNK = 128
SCAN_UNROLL = 16

LANES = 128
SUBLANES = 8
VMEM_LIMIT_BYTES = 56 * 1024 * 1024
ROW_TILE = 2048
EPILOGUE_ROWS = 256
PROJ_BLOCK_ROWS = 512

_HIGHEST = lax.Precision.HIGHEST


def _cparams(*sem, flags=None):
    return pltpu.CompilerParams(dimension_semantics=sem, vmem_limit_bytes=VMEM_LIMIT_BYTES, flags=flags)


def _norm_rows(x, g, shift, scale, modulate):
    ms = jnp.mean(x * x, axis=-1, keepdims=True)
    y = x * lax.rsqrt(ms + EPS) * g
    if modulate:
        y = y * (1.0 + scale) + shift
    return y


def _silu(v):
    return v * jax.nn.sigmoid(v)


def _mod_kernel(s_ref, w_ref, b_ref, o_ref):
    s = _silu(s_ref[...]).astype(BF16)
    w = w_ref[0].astype(BF16)
    o_ref[0] = jnp.dot(s, w, preferred_element_type=F32) + b_ref[0]


def _modulation(cond, w_mod, b_mod):
    depth, d, n = w_mod.shape
    rows = cond.shape[0]
    tn = 1024
    return pl.pallas_call(
        _mod_kernel,
        grid=(depth, n // tn),
        in_specs=[
            pl.BlockSpec((rows, d), lambda i, j: (0, 0)),
            pl.BlockSpec((1, d, tn), lambda i, j: (i, 0, j)),
            pl.BlockSpec((1, 1, tn), lambda i, j: (i, 0, j)),
        ],
        out_specs=pl.BlockSpec((1, rows, tn), lambda i, j: (i, 0, j)),
        out_shape=jax.ShapeDtypeStruct((depth, rows, n), F32),
        compiler_params=_cparams("parallel", "parallel"),
        name="modulation",
    )(cond, w_mod, b_mod.reshape(depth, 1, n))


def _normmod_kernel(x_ref, g_ref, sh_ref, sc_ref, o_ref):
    o_ref[...] = _norm_rows(x_ref[...], g_ref[...], sh_ref[0], sc_ref[0], True).astype(o_ref.dtype)


def _normmod(x, g, shift, scale, rows_per_mod):
    m, d = x.shape
    tm = 512
    mod_spec = pl.BlockSpec((1, 1, d), lambda i: ((i * tm) // rows_per_mod, 0, 0))
    return pl.pallas_call(
        _normmod_kernel,
        grid=(m // tm,),
        in_specs=[
            pl.BlockSpec((tm, d), lambda i: (i, 0)),
            pl.BlockSpec((1, d), lambda i: (0, 0)),
            mod_spec,
            mod_spec,
        ],
        out_specs=pl.BlockSpec((tm, d), lambda i: (i, 0)),
        out_shape=jax.ShapeDtypeStruct((m, d), BF16),
        compiler_params=_cparams("parallel"),
        name="normmod",
    )(x, g.reshape(1, d), shift, scale)


def _mm_kernel(a_ref, w_ref, o_ref):
    o_ref[...] = jnp.dot(a_ref[...], w_ref[...], preferred_element_type=F32).astype(o_ref.dtype)


def _matmul(a, w, layer, out_dtype, tm=1024, tn=1024):
    m, k = a.shape
    n = w.shape[2]
    tm, tn = min(tm, m), min(tn, n)
    return pl.pallas_call(
        _mm_kernel,
        grid=(m // tm, n // tn),
        in_specs=[
            pl.BlockSpec((tm, k), lambda i, j: (i, 0)),
            pl.BlockSpec((None, k, tn), lambda i, j: (layer, 0, j)),
        ],
        out_specs=pl.BlockSpec((tm, tn), lambda i, j: (i, j)),
        out_shape=jax.ShapeDtypeStruct((m, n), out_dtype),
        compiler_params=_cparams("parallel", "parallel"),
        name="matmul",
    )(a, w)


def _mm_res_kernel(a_ref, w_ref, x_ref, gate_ref, g_ref, sh_ref, sc_ref, xo_ref, ao_ref, xn_scr, ssq_scr,
                   *, nn, modulate):
    j = pl.program_id(1)
    tn = x_ref.shape[1]
    w = w_ref[:, pl.ds(pl.multiple_of(j * tn, tn), tn)]
    xn = x_ref[...] + gate_ref[0] * jnp.dot(a_ref[...], w, preferred_element_type=F32)
    xo_ref[...] = xn
    xn_scr[j] = xn
    ssq = jnp.sum(xn * xn, axis=-1, keepdims=True)

    @pl.when(j == 0)
    def _():
        ssq_scr[...] = ssq

    @pl.when(j > 0)
    def _():
        ssq_scr[...] += ssq

    @pl.when(j == nn - 1)
    def _():
        inv = lax.rsqrt(ssq_scr[...] * (1.0 / (nn * tn)) + EPS)
        for c in range(nn):
            cols = slice(c * tn, (c + 1) * tn)
            y = xn_scr[c] * inv * g_ref[:, cols]
            if modulate:
                y = y * (1.0 + sc_ref[0][:, cols]) + sh_ref[0][:, cols]
            ao_ref[:, cols] = y.astype(ao_ref.dtype)


def _residual_tile_cols(k):
    return 1024 if k <= 4096 else 512


def _matmul_residual(a, w, layer, x, gate, g_next, shift, scale, rows_per_mod, modulate=True, a_dtype=BF16):
    m, k = a.shape
    d = w.shape[2]
    tn = _residual_tile_cols(k)
    nn = d // tn
    tm = min(512, m)

    def mod_row(i, j):
        return ((i * tm) // rows_per_mod, 0, 0)

    mod_spec = pl.BlockSpec((1, 1, d), mod_row)
    return pl.pallas_call(
        functools.partial(_mm_res_kernel, nn=nn, modulate=modulate),
        grid=(m // tm, nn),
        in_specs=[
            pl.BlockSpec((tm, k), lambda i, j: (i, 0)),
            pl.BlockSpec((None, k, d), lambda i, j: (layer, 0, 0), pipeline_mode=pl.Buffered(1)),
            pl.BlockSpec((tm, tn), lambda i, j: (i, j)),
            pl.BlockSpec((1, 1, tn), lambda i, j: ((i * tm) // rows_per_mod, 0, j)),
            pl.BlockSpec((1, d), lambda i, j: (0, 0)),
            mod_spec,
            mod_spec,
        ],
        out_specs=[
            pl.BlockSpec((tm, tn), lambda i, j: (i, j)),
            pl.BlockSpec((tm, d), lambda i, j: (i, 0)),
        ],
        out_shape=[
            jax.ShapeDtypeStruct((m, d), F32),
            jax.ShapeDtypeStruct((m, d), a_dtype),
        ],
        scratch_shapes=[pltpu.VMEM((nn, tm, tn), F32), pltpu.VMEM((tm, 1), F32)],
        compiler_params=_cparams("parallel", "arbitrary"),
        name="matmul_residual",
    )(a, w, x, gate, g_next.reshape(1, d), shift, scale)


def _proj_conv_kernel(*refs, rows, width, vertical, gated):
    refs = list(refs)
    a_ref, wg_ref = refs[:2]
    wv_ref = refs.pop(2) if gated else None
    cw_ref, cb_ref, o_ref, u_scr = refs[2:6]
    scratch = refs[6:]
    if vertical:
        (ul_scr, ur_scr), scratch = scratch[:2], scratch[2:]
    if gated:
        v_scr, scratch = scratch[0], scratch[1:]
    if wg_ref.dtype != BF16:
        scratch[0][...] = wg_ref[...].astype(BF16)
        wg_ref = scratch[0]
        if gated:
            scratch[1][...] = wv_ref[...].astype(BF16)
            wv_ref = scratch[1]
    tn = o_ref.shape[1]
    pad = (u_scr.shape[0] - rows) // 2
    rb = min(PROJ_BLOCK_ROWS, rows)
    rc = min(EPILOGUE_ROWS, rb)
    zeros = jnp.zeros((pad, tn), F32)
    for buf in [u_scr] + ([ul_scr, ur_scr] if vertical else []):
        buf[pl.ds(0, pad), :] = zeros
        buf[pl.ds(pad + rows, pad), :] = zeros
    cw = cw_ref[...]
    bias = cb_ref[...]
    row = lax.broadcasted_iota(jnp.int32, (rc, tn), 0)
    everything = slice(0, tn)

    def project(blk, cols=everything):
        a = a_ref[pl.ds(blk * rb, rb), :]
        u_scr[pl.ds(pad + blk * rb, rb), cols] = jnp.dot(a, wg_ref[:, cols], preferred_element_type=F32)
        if vertical:
            col = row & (width - 1)
            for r in range(blk * rb // rc, (blk + 1) * rb // rc):
                chunk = u_scr[pl.ds(pad + r * rc, rc), :]
                ul_scr[pl.ds(pad + r * rc, rc), :] = jnp.where(col != 0, pltpu.roll(chunk, 1, 0), 0.0)
                ur_scr[pl.ds(pad + r * rc, rc), :] = jnp.where(col != width - 1, pltpu.roll(chunk, rc - 1, 0), 0.0)

    def convolve(blk, cols=everything):
        for r in range(blk * rb // rc, (blk + 1) * rb // rc):
            base = pad + r * rc
            if vertical:
                acc = bias
                for di in range(3):
                    off = base + (di - 1) * width
                    acc = acc + ul_scr[pl.ds(off, rc), :] * cw[3 * di:3 * di + 1, :]
                    acc = acc + u_scr[pl.ds(off, rc), :] * cw[3 * di + 1:3 * di + 2, :]
                    acc = acc + ur_scr[pl.ds(off, rc), :] * cw[3 * di + 2:3 * di + 3, :]
            else:
                col = (lax.broadcasted_iota(jnp.int32, (rc, cols.stop - cols.start), 0) + r * rc) & (width - 1)
                left = jnp.where(col != 0, u_scr[pl.ds(base - 1, rc), cols], 0.0)
                right = jnp.where(col != width - 1, u_scr[pl.ds(base + 1, rc), cols], 0.0)
                acc = (left * cw_ref[0:1, cols] + u_scr[pl.ds(base, rc), cols] * cw_ref[1:2, cols]
                       + right * cw_ref[2:3, cols] + cb_ref[:, cols])
            y = _silu(acc)
            if gated:
                v_scr[pl.ds(r * rc, rc), :] = y
            else:
                o_ref[pl.ds(r * rc, rc), cols] = y.astype(o_ref.dtype)

    n_blocks = rows // rb
    for blk in range(n_blocks):
        project(blk)
        if blk >= 1:
            convolve(blk - 1)
    if gated:
        val = jnp.dot(a_ref[...], wv_ref[...], preferred_element_type=F32)
    convolve(n_blocks - 1)
    if gated:
        for r in range(rows // rc):
            chunk = pl.ds(r * rc, rc)
            o_ref[chunk, :] = (v_scr[chunk, :] * val[r * rc:(r + 1) * rc, :]).astype(o_ref.dtype)


def _proj_conv(a, w, layer, conv_w, conv_b, *, width, vertical, n_out, val_offset=None, tn):
    m, k = a.shape
    rows = min(ROW_TILE, m)
    gated = val_offset is not None
    assert width & (width - 1) == 0 and rows % width == 0 and m % rows == 0
    pad = (width + SUBLANES) if vertical else SUBLANES
    taps = conv_w.shape[0]
    in_specs = [
        pl.BlockSpec((rows, k), lambda i, j: (i, 0)),
        pl.BlockSpec((None, k, tn), lambda i, j: (layer, 0, j)),
    ]
    operands = [a, w]
    padded = pltpu.VMEM((rows + 2 * pad, tn), F32)
    scratch = [padded] + ([padded, padded] if vertical else []) + ([pltpu.VMEM((rows, tn), F32)] if gated else [])
    if w.dtype != BF16:
        scratch += [pltpu.VMEM((k, tn), BF16)] * (2 if gated else 1)
    if gated:
        voff = val_offset // tn
        in_specs.append(pl.BlockSpec((None, k, tn), lambda i, j: (layer, 0, j + voff)))
        operands.append(w)
    in_specs += [
        pl.BlockSpec((taps, tn), lambda i, j: (0, j)),
        pl.BlockSpec((1, tn), lambda i, j: (0, j)),
    ]
    operands += [conv_w, conv_b.reshape(1, n_out)]
    return pl.pallas_call(
        functools.partial(_proj_conv_kernel, rows=rows, width=width, vertical=vertical, gated=gated),
        grid=(m // rows, n_out // tn),
        in_specs=in_specs,
        out_specs=pl.BlockSpec((rows, tn), lambda i, j: (i, j)),
        out_shape=jax.ShapeDtypeStruct((m, n_out), BF16),
        scratch_shapes=scratch,
        compiler_params=_cparams("parallel", "parallel"),
        name="proj_conv",
    )(*operands)


def _dft_tables(n, sin_sign):
    s = 64
    k = jnp.arange(n, dtype=jnp.int32)[:, None]

    def narrow(cols):
        ang = ((k * cols[None, :]) % n).astype(F32) * F32(2.0 * math.pi / n)
        return jnp.cos(ang), jnp.sin(ang)

    ca, sa = narrow(jnp.arange(n // s, dtype=jnp.int32) * s)
    cb, sb = narrow(jnp.arange(s, dtype=jnp.int32))
    scale = F32(1.0 / math.sqrt(n))
    p = jnp.stack([ca, sin_sign * sa], axis=1)[:, :, :, None] * scale
    q = jnp.stack([-sa, sin_sign * ca], axis=1)[:, :, :, None] * scale
    return (p * cb[:, None, None, :] + q * sb[:, None, None, :]).reshape(n, 2 * n)


def _chan_dft_kernel(a_ref, wc_ref, o_ref, *, groups):
    dg = wc_ref.shape[0]
    wc = wc_ref[...]
    for g in range(groups):
        cols = pl.ds(g * dg, dg)
        res = jnp.dot(a_ref[:, cols], wc, preferred_element_type=F32)
        o_ref[0, :, cols] = res[:, :dg].astype(o_ref.dtype)
        o_ref[1, :, cols] = res[:, dg:].astype(o_ref.dtype)


def _chan_dft(a, wc, nb, ln):
    d = a.shape[1]
    dg = wc.shape[0]
    tm = min(512, ln)
    mt = ln // tm
    return pl.pallas_call(
        functools.partial(_chan_dft_kernel, groups=d // dg),
        grid=(nb, mt),
        in_specs=[
            pl.BlockSpec((tm, d), lambda b, i: (b * mt + i, 0)),
            pl.BlockSpec((dg, 2 * dg), lambda b, i: (0, 0)),
        ],
        out_specs=pl.BlockSpec((None, 2, tm, d), lambda b, i: (b, 0, i, 0)),
        out_shape=jax.ShapeDtypeStruct((nb, 2, ln, d), BF16),
        compiler_params=_cparams("parallel", "parallel"),
        name="chan_dft",
    )(a, wc)


def _pos_dft(cs, y):
    nb, k2, d = y.shape
    ln = cs.shape[0]
    tm, tn = min(1024, ln), 1024
    return pl.pallas_call(
        _mm_kernel,
        grid=(nb, d // tn, ln // tm),
        in_specs=[
            pl.BlockSpec((tm, k2), lambda b, j, i: (i, 0)),
            pl.BlockSpec((None, k2, tn), lambda b, j, i: (b, 0, j)),
        ],
        out_specs=pl.BlockSpec((None, tm, tn), lambda b, j, i: (b, i, j)),
        out_shape=jax.ShapeDtypeStruct((nb, ln, d), BF16),
        compiler_params=_cparams("parallel", "parallel", "parallel"),
        name="pos_dft",
    )(cs, y)


def _fourier(a, nb, ln, wc, cs):
    d = a.shape[1]
    y = _chan_dft(a, wc, nb, ln)
    f = _pos_dft(cs, y.reshape(nb, 2 * ln, d))
    return f.reshape(nb * ln, d)


def _ssd_dt_kernel(a_ref, wt_ref, bias_ref, alog_ref, dtt_ref, acst_ref, *, heads, groups):
    ln = a_ref.shape[0]
    q = SSD_CHUNK
    r = heads // groups
    dtt = jax.nn.softplus(
        lax.dot_general(wt_ref[...], a_ref[...], (((1,), (1,)), ((), ())), preferred_element_type=F32)
        + bias_ref[...])
    dtat = dtt * (-jnp.exp(alog_ref[...]))
    ri = lax.broadcasted_iota(jnp.int32, (q, q), 0)
    ci = lax.broadcasted_iota(jnp.int32, (q, q), 1)
    lower = (ri >= ci).astype(F32)
    upper = (ri <= ci).astype(F32)
    fwd_row = lax.broadcasted_iota(jnp.int32, (2 * heads, q), 0) < heads
    for c in range(ln // q):
        xt = dtat[:, c * q:(c + 1) * q]
        prefix = jnp.dot(xt, upper, precision=_HIGHEST, preferred_element_type=F32)
        suffix = jnp.dot(xt, lower, precision=_HIGHEST, preferred_element_type=F32)
        acs = jnp.where(fwd_row, prefix, suffix)
        dtc = dtt[:, c * q:(c + 1) * q]
        for g in range(groups):
            for half in range(2):
                src = slice(half * heads + g * r, half * heads + (g + 1) * r)
                dst = pl.ds(half * r, r)
                acst_ref[g, c, dst, :] = acs[src, :]
                dtt_ref[g, c, dst, :] = dtc[src, :]


def _ssd_dt(a, w_dt_t, dt_bias, a_log, nb, ln):
    d = a.shape[1]
    h2 = w_dt_t.shape[0]
    groups = SSD_GROUPS
    nc = ln // SSD_CHUNK
    vec = pl.BlockSpec((h2, 1), lambda b: (0, 0))
    out = pl.BlockSpec((None, groups, nc, h2 // groups, SSD_CHUNK), lambda b: (b, 0, 0, 0, 0))
    shape = jax.ShapeDtypeStruct((nb, groups, nc, h2 // groups, SSD_CHUNK), F32)
    return pl.pallas_call(
        functools.partial(_ssd_dt_kernel, heads=h2 // 2, groups=groups),
        grid=(nb,),
        in_specs=[
            pl.BlockSpec((ln, d), lambda b: (b, 0)),
            pl.BlockSpec((h2, d), lambda b: (0, 0)),
            vec, vec,
        ],
        out_specs=[out, out],
        out_shape=[shape, shape],
        compiler_params=_cparams("parallel"),
        name="ssd_dt",
    )(a, w_dt_t, dt_bias.reshape(h2, 1), a_log.reshape(h2, 1))


def _ssd_scan_kernel(*refs, nc, heads_per_group, emit):
    if emit:
        (xs_ref, b_ref, c_ref, z_ref, acst_ref, dtt_ref, h0_ref, d0_ref, d1_ref, ng_ref, o_ref, hfin_ref,
         yf_scr, h_scr, tok_scr) = refs
    else:
        xs_ref, b_ref, acst_ref, dtt_ref, h0_ref, hfin_ref, h_scr, tok_scr = refs
    q = SSD_CHUNK
    r_heads = heads_per_group
    p = SSD_HEAD_DIM
    gw = r_heads * p
    expand = (lax.broadcasted_iota(jnp.int32, (2 * r_heads, 2 * gw), 1) // p
              == lax.broadcasted_iota(jnp.int32, (2 * r_heads, 2 * gw), 0)).astype(BF16)
    li = lax.broadcasted_iota(jnp.int32, (q, q), 0)
    si = lax.broadcasted_iota(jnp.int32, (q, q), 1)
    lane = lax.broadcasted_iota(jnp.int32, (q, gw), 1)
    even_head = (lane % (2 * p)) < p
    pad_rows = jnp.zeros((q - 3 * r_heads, q), F32)
    if emit:
        dskip = d0_ref[...] + d1_ref[...]
        norm_g = ng_ref[...]

    def chunk(c, direction):
        rows = pl.ds(pl.multiple_of(c * q, q), q)
        xs_bf = xs_ref[rows, :]
        bc = b_ref[rows, :]
        tok = tok_scr[direction, c]
        fac = tok[:, :2 * r_heads]
        fac_hi = fac.astype(BF16)
        fac_lo = (fac - fac_hi.astype(F32)).astype(BF16)
        fac_e = (jnp.dot(fac_hi, expand, preferred_element_type=F32)
                 + jnp.dot(fac_lo, expand, preferred_element_type=F32))
        ea_e = fac_e[:, :gw]
        wf_e = fac_e[:, gw:]
        h_decay = ea_e[q - 1:q, :] if direction == 0 else ea_e[0:1, :]
        h = h_scr[...]

        def update_state():
            w = xs_bf * wf_e.astype(BF16)
            h_scr[...] = h * h_decay + lax.dot_general(
                bc, w, (((0,), (0,)), ((), ())), preferred_element_type=F32)

        if not emit:
            update_state()
            return
        cc = c_ref[rows, :]
        src_rows = acst_ref[c] - jnp.log(dtt_ref[c])
        causal = (li >= si) if direction == 0 else (li <= si)
        scores = lax.dot_general(cc, bc, (((1,), (1,)), ((), ())), preferred_element_type=F32)
        y = jnp.dot(cc, h.astype(BF16), preferred_element_type=F32) * ea_e
        xs_even = jnp.where(even_head, xs_bf, jnp.zeros_like(xs_bf))
        xs_odd = jnp.where(even_head, jnp.zeros_like(xs_bf), xs_bf)
        pieces = []
        for pair in range(r_heads // 2):
            ms = []
            for sub in range(2):
                r = 2 * pair + sub
                col = direction * r_heads + r
                seg = tok[:, 2 * r_heads + r:2 * r_heads + r + 1] - src_rows[col:col + 1, :]
                decay = jnp.exp(jnp.where(causal, seg, -jnp.inf))
                ms.append((scores * decay).astype(BF16))
            cols = slice(pair * 2 * p, (pair + 1) * 2 * p)
            pieces.append(jnp.dot(jnp.concatenate(ms, axis=1),
                                  jnp.concatenate([xs_even[:, cols], xs_odd[:, cols]], axis=0),
                                  preferred_element_type=F32))
        y = y + jnp.concatenate(pieces, axis=1)
        update_state()
        if direction == 0:
            yf_scr[rows, :] = y
        else:
            tot = yf_scr[rows, :] + y + dskip * xs_bf.astype(F32)
            gz = tot * _silu(z_ref[rows, :].astype(F32))
            gz = gz * lax.rsqrt(jnp.mean(gz * gz, axis=-1, keepdims=True) + EPS)
            o_ref[rows, :] = (gz * norm_g).astype(o_ref.dtype)

    def to_token_major(c, carry):
        acs_rows = acst_ref[c]
        dt_rows = dtt_ref[c]
        for direction in range(2):
            acs_d = acs_rows[direction * r_heads:(direction + 1) * r_heads, :]
            dt_d = dt_rows[direction * r_heads:(direction + 1) * r_heads, :]
            last = acs_d[:, q - 1:q] if direction == 0 else acs_d[:, 0:1]
            tok_scr[direction, c] = jnp.concatenate(
                [jnp.exp(acs_d), dt_d * jnp.exp(last - acs_d), acs_d, pad_rows], axis=0).T
        return carry

    lax.fori_loop(0, nc, to_token_major, 0, unroll=True)
    h_scr[...] = h0_ref[0]

    def fwd_body(c, carry):
        chunk(c, 0)
        return carry

    lax.fori_loop(0, nc, fwd_body, 0, unroll=min(SCAN_UNROLL, nc))
    hfin_ref[0] = h_scr[...]
    h_scr[...] = h0_ref[1]

    def bwd_body(i, carry):
        chunk(nc - 1 - i, 1)
        return carry

    lax.fori_loop(0, nc, bwd_body, 0, unroll=min(SCAN_UNROLL, nc))
    hfin_ref[1] = h_scr[...]


def _ssd_scan(xb, cm, z, acst4, dtt4, h0, d0e, d1e, norm_g, nb, ln):
    emit = z is not None
    groups = SSD_GROUPS
    n = SSD_STATE
    gw = h0.shape[-1]
    d_inner = gw * groups
    r2 = acst4.shape[-2]
    nc = ln // SSD_CHUNK
    b_blk = d_inner // n
    chunk_g = pl.BlockSpec((None, None, nc, r2, SSD_CHUNK), lambda b, g: (b, g, 0, 0, 0))
    state = pl.BlockSpec((None, None, 2, n, gw), lambda b, g: (b, g, 0, 0, 0))
    vec = pl.BlockSpec((1, gw), lambda b, g: (0, g))
    seq_x = pl.BlockSpec((ln, gw), lambda b, g: (b, g))
    seq_b = pl.BlockSpec((ln, n), lambda b, g: (b, b_blk + g))
    seq_c = pl.BlockSpec((ln, n), lambda b, g: (b, g))
    state_shape = jax.ShapeDtypeStruct(h0.shape, F32)
    scratch = [pltpu.VMEM((n, gw), F32), pltpu.VMEM((2, nc, SSD_CHUNK, SSD_CHUNK), F32)]
    if emit:
        in_specs = [seq_x, seq_b, seq_c, seq_x, chunk_g, chunk_g, state, vec, vec, vec]
        operands = (xb, xb, cm, z, acst4, dtt4, h0, d0e, d1e, norm_g)
        out_specs = [seq_x, state]
        out_shape = [jax.ShapeDtypeStruct((nb * ln, d_inner), BF16), state_shape]
        scratch = [pltpu.VMEM((ln, gw), F32)] + scratch
    else:
        in_specs = [seq_x, seq_b, chunk_g, chunk_g, state]
        operands = (xb, xb, acst4, dtt4, h0)
        out_specs = [state]
        out_shape = [state_shape]
    res = pl.pallas_call(
        functools.partial(_ssd_scan_kernel, nc=nc, heads_per_group=r2 // 2, emit=emit),
        grid=(nb, groups),
        in_specs=in_specs,
        out_specs=out_specs,
        out_shape=out_shape,
        scratch_shapes=scratch,
        compiler_params=_cparams("parallel", "parallel"),
        name="ssd_scan",
    )(*operands)
    return res if emit else (None, res[0])


def _ssd_mixer(a, nb, ln, weights, layer, h0, emit=True):
    w_in, w_c, w_z, w_dt_t, conv_w, conv_b, dt_bias, a_log, d0e, d1e, norm_g = weights
    d_inner = w_z.shape[2]
    n_xb = d_inner + SSD_GROUPS * SSD_STATE
    xb = _proj_conv(a, w_in, layer, conv_w[:, :n_xb], conv_b[:n_xb], width=ln, vertical=False, n_out=n_xb, tn=512)
    cm = z = None
    if emit:
        cm = _proj_conv(a, w_c, layer, conv_w[:, n_xb:], conv_b[n_xb:], width=ln, vertical=False,
                        n_out=w_c.shape[2], tn=512)
        z = _matmul(a, w_z, layer, BF16)
    dtt4, acst4 = _ssd_dt(a, w_dt_t, dt_bias, a_log, nb, ln)
    return _ssd_scan(xb, cm, z, acst4, dtt4, h0, d0e, d1e, norm_g, nb, ln)


def kernel(x, c, ctx, c_ctx, w_mod, b_mod, norm_mix_g, norm_ffn_g, four_w, ssd_w_in, ssd_conv_w, ssd_conv_b,
           ssd_dt_bias, ssd_a_log, ssd_d, ssd_norm_g, ssd_w_out, ffn_w_up, ffn_conv_w, ffn_conv_b, ffn_w_down,
           final_g):
    nb, ln, d = x.shape
    lc = ctx.shape[1]
    depth = w_mod.shape[0]
    d_ff = ffn_w_down.shape[1]
    d_inner = ssd_w_out.shape[1]
    heads = ssd_dt_bias.shape[2]
    gn = SSD_GROUPS * SSD_STATE
    t_lat, t_ctx = nb * ln, nb * lc

    mod_rows = -(-(nb + 1) // SUBLANES) * SUBLANES
    cond = jnp.zeros((mod_rows, d), F32).at[:nb].set(c).at[nb].set(c_ctx)
    mods = _modulation(cond, w_mod, b_mod)

    def lat_mod(i, j):
        return mods[i, :nb, j * d:(j + 1) * d].reshape(nb, 1, d)

    def ctx_mod(i, j):
        return mods[i, nb:nb + 1, j * d:(j + 1) * d].reshape(1, 1, d)

    four_bf = four_w.astype(BF16)
    w_down_bf = ffn_w_down.astype(BF16)
    w_out_bf = ssd_w_out.astype(BF16)
    xb = d_inner + gn
    state_cols = xb + 2 * heads
    w_c_bf = ssd_w_in[:, :, state_cols:state_cols + gn].astype(BF16)
    w_z_bf = ssd_w_in[:, :, state_cols + gn:].astype(BF16)
    w_dt_t_bf = lax.optimization_barrier(ssd_w_in[:, :, xb:state_cols]).transpose(0, 2, 1).astype(BF16)
    conv9 = ffn_conv_w.reshape(depth, 9, d_ff)

    dg = d // FOURIER_GROUPS
    wc = _dft_tables(dg, 1.0).astype(BF16)
    cs_lat = _dft_tables(ln, -1.0).astype(BF16)
    cs_ctx = _dft_tables(lc, -1.0).astype(BF16)

    xl = x.reshape(t_lat, d)
    xc = ctx.reshape(t_ctx, d)
    a_lat = _normmod(xl, norm_mix_g[0], lat_mod(0, 0), lat_mod(0, 1), ln)
    a_ctx = _normmod(xc, norm_mix_g[0], ctx_mod(0, 0), ctx_mod(0, 1), t_ctx)
    out = None
    for i in range(depth):
        last = i == depth - 1
        is_ssd = i % 2 == 1
        j = i // 2
        if is_ssd:
            d0e = jnp.repeat(ssd_d[j, 0], SSD_HEAD_DIM).reshape(1, d_inner)
            d1e = jnp.repeat(ssd_d[j, 1], SSD_HEAD_DIM).reshape(1, d_inner)
            weights = (ssd_w_in, w_c_bf, w_z_bf, w_dt_t_bf[j], ssd_conv_w[j], ssd_conv_b[j], ssd_dt_bias[j], ssd_a_log[j],
                       d0e, d1e, ssd_norm_g[j].reshape(1, d_inner))
            zeros = jnp.zeros((nb, SSD_GROUPS, 2, SSD_STATE, d_inner // SSD_GROUPS), F32)
            mix_ctx, h_ctx = _ssd_mixer(a_ctx, nb, lc, weights, j, zeros, emit=not last)
            mix_lat, _ = _ssd_mixer(a_lat, nb, ln, weights, j, h_ctx)
            w_mix = w_out_bf
        else:
            mix_lat = _fourier(a_lat, nb, ln, wc, cs_lat)
            mix_ctx = None if last else _fourier(a_ctx, nb, lc, wc, cs_ctx)
            w_mix = four_bf
        xl, b_lat = _matmul_residual(mix_lat, w_mix, j, xl, lat_mod(i, 2), norm_ffn_g[i], lat_mod(i, 3),
                                     lat_mod(i, 4), ln)
        act = _proj_conv(b_lat, ffn_w_up, i, conv9[i], ffn_conv_b[i], width=GRID_W, vertical=True,
                         n_out=d_ff, val_offset=d_ff, tn=256)
        if last:
            zero_mod = jnp.zeros((nb, 1, d), F32)
            _, out = _matmul_residual(act, w_down_bf, i, xl, lat_mod(i, 5), final_g, zero_mod, zero_mod, ln,
                                      modulate=False, a_dtype=F32)
        else:
            xl, a_lat = _matmul_residual(act, w_down_bf, i, xl, lat_mod(i, 5), norm_mix_g[i + 1],
                                         lat_mod(i + 1, 0), lat_mod(i + 1, 1), ln)
            xc, b_ctx = _matmul_residual(mix_ctx, w_mix, j, xc, ctx_mod(i, 2), norm_ffn_g[i], ctx_mod(i, 3),
                                         ctx_mod(i, 4), t_ctx)
            act_c = _proj_conv(b_ctx, ffn_w_up, i, ffn_conv_w[i, 1], ffn_conv_b[i], width=lc, vertical=False,
                               n_out=d_ff, val_offset=d_ff, tn=256)
            xc, a_ctx = _matmul_residual(act_c, w_down_bf, i, xc, ctx_mod(i, 5), norm_mix_g[i + 1],
                                         ctx_mod(i + 1, 0), ctx_mod(i + 1, 1), t_ctx)
    return out.reshape(nb, ln, d)
```

```python
import functools
import math

import jax
import jax.numpy as jnp
from jax import lax
from jax.experimental import pallas as pl
from jax.experimental.pallas import tpu as pltpu

F32 = jnp.float32
BF16 = jnp.bfloat16

EPS = 1e-6
GRID_W = 64
FOURIER_GROUPS = 8
SSD_HEAD_DIM = 64
SSD_GROUPS = 8
SSD_STATE = 128
SSD_CHUNK = 128
SCAN_UNROLL = 16

LANES = 128
SUBLANES = 8
VMEM_LIMIT_BYTES = 56 * 1024 * 1024
ROW_TILE = 2048
EPILOGUE_ROWS = 256
PROJ_BLOCK_ROWS = 512
GATED_BLOCK_ROWS = 256

_HIGHEST = lax.Precision.HIGHEST


def _cparams(*sem, flags=None):
    return pltpu.CompilerParams(dimension_semantics=sem, vmem_limit_bytes=VMEM_LIMIT_BYTES, flags=flags)


def _norm_rows(x, g, shift, scale, modulate):
    ms = jnp.mean(x * x, axis=-1, keepdims=True)
    y = x * lax.rsqrt(ms + EPS) * g
    if modulate:
        y = y * (1.0 + scale) + shift
    return y


def _silu(v):
    return v * jax.nn.sigmoid(v)


def _mod_kernel(s_ref, w_ref, b_ref, o_ref):
    s = _silu(s_ref[...]).astype(BF16)
    w = w_ref[0].astype(BF16)
    o_ref[0] = jnp.dot(s, w, preferred_element_type=F32) + b_ref[0]


def _modulation(cond, w_mod, b_mod):
    depth, d, n = w_mod.shape
    rows = cond.shape[0]
    tn = 1024
    return pl.pallas_call(
        _mod_kernel,
        grid=(depth, n // tn),
        in_specs=[
            pl.BlockSpec((rows, d), lambda i, j: (0, 0)),
            pl.BlockSpec((1, d, tn), lambda i, j: (i, 0, j)),
            pl.BlockSpec((1, 1, tn), lambda i, j: (i, 0, j)),
        ],
        out_specs=pl.BlockSpec((1, rows, tn), lambda i, j: (i, 0, j)),
        out_shape=jax.ShapeDtypeStruct((depth, rows, n), F32),
        compiler_params=_cparams("parallel", "parallel"),
        name="modulation",
    )(cond, w_mod, b_mod.reshape(depth, 1, n))


def _normmod_kernel(x_ref, g_ref, sh_ref, sc_ref, o_ref):
    o_ref[...] = _norm_rows(x_ref[...], g_ref[...], sh_ref[0], sc_ref[0], True).astype(o_ref.dtype)


def _normmod(x, g, shift, scale, rows_per_mod):
    m, d = x.shape
    tm = 512
    mod_spec = pl.BlockSpec((1, 1, d), lambda i: ((i * tm) // rows_per_mod, 0, 0))
    return pl.pallas_call(
        _normmod_kernel,
        grid=(m // tm,),
        in_specs=[
            pl.BlockSpec((tm, d), lambda i: (i, 0)),
            pl.BlockSpec((1, d), lambda i: (0, 0)),
            mod_spec,
            mod_spec,
        ],
        out_specs=pl.BlockSpec((tm, d), lambda i: (i, 0)),
        out_shape=jax.ShapeDtypeStruct((m, d), BF16),
        compiler_params=_cparams("parallel"),
        name="normmod",
    )(x, g.reshape(1, d), shift, scale)


def _mm_kernel(a_ref, w_ref, o_ref):
    o_ref[...] = jnp.dot(a_ref[...], w_ref[...], preferred_element_type=F32).astype(o_ref.dtype)


def _matmul(a, w, layer, out_dtype, tm=1024, tn=1024):
    m, k = a.shape
    n = w.shape[2]
    tm, tn = min(tm, m), min(tn, n)
    return pl.pallas_call(
        _mm_kernel,
        grid=(m // tm, n // tn),
        in_specs=[
            pl.BlockSpec((tm, k), lambda i, j: (i, 0)),
            pl.BlockSpec((None, k, tn), lambda i, j: (layer, 0, j)),
        ],
        out_specs=pl.BlockSpec((tm, tn), lambda i, j: (i, j)),
        out_shape=jax.ShapeDtypeStruct((m, n), out_dtype),
        compiler_params=_cparams("parallel", "parallel"),
        name="matmul",
    )(a, w)


def _mm_res_kernel(a_ref, w_ref, x_ref, gate_ref, g_ref, sh_ref, sc_ref, xo_ref, ao_ref, xn_scr, ssq_scr,
                   *, nn, modulate):
    j = pl.program_id(1)
    tn = x_ref.shape[1]
    w = w_ref[:, pl.ds(pl.multiple_of(j * tn, tn), tn)]
    xn = x_ref[...] + gate_ref[0] * jnp.dot(a_ref[...], w, preferred_element_type=F32)
    xo_ref[...] = xn
    xn_scr[j] = xn
    ssq = jnp.sum(xn * xn, axis=-1, keepdims=True)

    @pl.when(j == 0)
    def _():
        ssq_scr[...] = ssq

    @pl.when(j > 0)
    def _():
        ssq_scr[...] += ssq

    @pl.when(j == nn - 1)
    def _():
        inv = lax.rsqrt(ssq_scr[...] * (1.0 / (nn * tn)) + EPS)
        for c in range(nn):
            cols = slice(c * tn, (c + 1) * tn)
            y = xn_scr[c] * inv * g_ref[:, cols]
            if modulate:
                y = y * (1.0 + sc_ref[0][:, cols]) + sh_ref[0][:, cols]
            ao_ref[:, cols] = y.astype(ao_ref.dtype)


def _residual_tile_cols(k):
    return 1024 if k <= 4096 else 512


def _matmul_residual(a, w, layer, x, gate, g_next, shift, scale, rows_per_mod, modulate=True, a_dtype=BF16):
    m, k = a.shape
    d = w.shape[2]
    tn = _residual_tile_cols(k)
    nn = d // tn
    tm = min(512, m)

    def mod_row(i, j):
        return ((i * tm) // rows_per_mod, 0, 0)

    mod_spec = pl.BlockSpec((1, 1, d), mod_row)
    return pl.pallas_call(
        functools.partial(_mm_res_kernel, nn=nn, modulate=modulate),
        grid=(m // tm, nn),
        in_specs=[
            pl.BlockSpec((tm, k), lambda i, j: (i, 0)),
            pl.BlockSpec((None, k, d), lambda i, j: (layer, 0, 0), pipeline_mode=pl.Buffered(1)),
            pl.BlockSpec((tm, tn), lambda i, j: (i, j)),
            pl.BlockSpec((1, 1, tn), lambda i, j: ((i * tm) // rows_per_mod, 0, j)),
            pl.BlockSpec((1, d), lambda i, j: (0, 0)),
            mod_spec,
            mod_spec,
        ],
        out_specs=[
            pl.BlockSpec((tm, tn), lambda i, j: (i, j)),
            pl.BlockSpec((tm, d), lambda i, j: (i, 0)),
        ],
        out_shape=[
            jax.ShapeDtypeStruct((m, d), F32),
            jax.ShapeDtypeStruct((m, d), a_dtype),
        ],
        scratch_shapes=[pltpu.VMEM((nn, tm, tn), F32), pltpu.VMEM((tm, 1), F32)],
        compiler_params=_cparams("parallel", "arbitrary"),
        name="matmul_residual",
    )(a, w, x, gate, g_next.reshape(1, d), shift, scale)


def _proj_conv_kernel(*refs, rows, width, vertical, gated):
    refs = list(refs)
    a_ref, wg_ref = refs[:2]
    wv_ref = refs.pop(2) if gated else None
    cw_ref, cb_ref, o_ref, u_scr = refs[2:6]
    scratch = refs[6:]
    if vertical:
        (ul_scr, ur_scr), scratch = scratch[:2], scratch[2:]
    if gated:
        v_scr, scratch = scratch[0], scratch[1:]
    if wg_ref.dtype != BF16:
        scratch[0][...] = wg_ref[...].astype(BF16)
        wg_ref = scratch[0]
        if gated:
            scratch[1][...] = wv_ref[...].astype(BF16)
            wv_ref = scratch[1]
    tn = o_ref.shape[1]
    pad = (u_scr.shape[0] - rows) // 2
    rb = min(GATED_BLOCK_ROWS if gated else PROJ_BLOCK_ROWS, rows)
    rc = min(EPILOGUE_ROWS, rb)
    zeros = jnp.zeros((pad, tn), F32)
    for buf in [u_scr] + ([ul_scr, ur_scr] if vertical else []):
        buf[pl.ds(0, pad), :] = zeros
        buf[pl.ds(pad + rows, pad), :] = zeros
    cw = cw_ref[...]
    bias = cb_ref[...]
    row = lax.broadcasted_iota(jnp.int32, (rc, tn), 0)
    everything = slice(0, tn)

    def project(blk, cols=everything):
        a = a_ref[pl.ds(blk * rb, rb), :]
        u_scr[pl.ds(pad + blk * rb, rb), cols] = jnp.dot(a, wg_ref[:, cols], preferred_element_type=F32)
        if vertical:
            col = row & (width - 1)
            for r in range(blk * rb // rc, (blk + 1) * rb // rc):
                chunk = u_scr[pl.ds(pad + r * rc, rc), :]
                ul_scr[pl.ds(pad + r * rc, rc), :] = jnp.where(col != 0, pltpu.roll(chunk, 1, 0), 0.0)
                ur_scr[pl.ds(pad + r * rc, rc), :] = jnp.where(col != width - 1, pltpu.roll(chunk, rc - 1, 0), 0.0)

    def convolve(blk, cols=everything):
        for r in range(blk * rb // rc, (blk + 1) * rb // rc):
            base = pad + r * rc
            if vertical:
                acc = bias
                for di in range(3):
                    off = base + (di - 1) * width
                    acc = acc + ul_scr[pl.ds(off, rc), :] * cw[3 * di:3 * di + 1, :]
                    acc = acc + u_scr[pl.ds(off, rc), :] * cw[3 * di + 1:3 * di + 2, :]
                    acc = acc + ur_scr[pl.ds(off, rc), :] * cw[3 * di + 2:3 * di + 3, :]
            else:
                col = (lax.broadcasted_iota(jnp.int32, (rc, cols.stop - cols.start), 0) + r * rc) & (width - 1)
                left = jnp.where(col != 0, u_scr[pl.ds(base - 1, rc), cols], 0.0)
                right = jnp.where(col != width - 1, u_scr[pl.ds(base + 1, rc), cols], 0.0)
                acc = (left * cw_ref[0:1, cols] + u_scr[pl.ds(base, rc), cols] * cw_ref[1:2, cols]
                       + right * cw_ref[2:3, cols] + cb_ref[:, cols])
            y = _silu(acc)
            if gated:
                v_scr[pl.ds(r * rc, rc), :] = y
            else:
                o_ref[pl.ds(r * rc, rc), cols] = y.astype(o_ref.dtype)

    n_blocks = rows // rb
    for blk in range(n_blocks):
        project(blk)
        if blk >= 1:
            convolve(blk - 1)
    if gated:
        val = jnp.dot(a_ref[...], wv_ref[...], preferred_element_type=F32)
    convolve(n_blocks - 1)
    if gated:
        for r in range(rows // rc):
            chunk = pl.ds(r * rc, rc)
            o_ref[chunk, :] = (v_scr[chunk, :] * val[r * rc:(r + 1) * rc, :]).astype(o_ref.dtype)


def _proj_conv(a, w, layer, conv_w, conv_b, *, width, vertical, n_out, val_offset=None, tn):
    m, k = a.shape
    rows = min(ROW_TILE, m)
    gated = val_offset is not None
    assert width & (width - 1) == 0 and rows % width == 0 and m % rows == 0
    pad = (width + SUBLANES) if vertical else SUBLANES
    taps = conv_w.shape[0]
    in_specs = [
        pl.BlockSpec((rows, k), lambda i, j: (i, 0)),
        pl.BlockSpec((None, k, tn), lambda i, j: (layer, 0, j)),
    ]
    operands = [a, w]
    padded = pltpu.VMEM((rows + 2 * pad, tn), F32)
    scratch = [padded] + ([padded, padded] if vertical else []) + ([pltpu.VMEM((rows, tn), F32)] if gated else [])
    if w.dtype != BF16:
        scratch += [pltpu.VMEM((k, tn), BF16)] * (2 if gated else 1)
    if gated:
        voff = val_offset // tn
        in_specs.append(pl.BlockSpec((None, k, tn), lambda i, j: (layer, 0, j + voff)))
        operands.append(w)
    in_specs += [
        pl.BlockSpec((taps, tn), lambda i, j: (0, j)),
        pl.BlockSpec((1, tn), lambda i, j: (0, j)),
    ]
    operands += [conv_w, conv_b.reshape(1, n_out)]
    return pl.pallas_call(
        functools.partial(_proj_conv_kernel, rows=rows, width=width, vertical=vertical, gated=gated),
        grid=(m // rows, n_out // tn),
        in_specs=in_specs,
        out_specs=pl.BlockSpec((rows, tn), lambda i, j: (i, j)),
        out_shape=jax.ShapeDtypeStruct((m, n_out), BF16),
        scratch_shapes=scratch,
        compiler_params=_cparams("parallel", "parallel"),
        name="proj_conv",
    )(*operands)


def _dft_tables(n, sin_sign):
    s = 64
    k = jnp.arange(n, dtype=jnp.int32)[:, None]

    def narrow(cols):
        ang = ((k * cols[None, :]) % n).astype(F32) * F32(2.0 * math.pi / n)
        return jnp.cos(ang), jnp.sin(ang)

    ca, sa = narrow(jnp.arange(n // s, dtype=jnp.int32) * s)
    cb, sb = narrow(jnp.arange(s, dtype=jnp.int32))
    scale = F32(1.0 / math.sqrt(n))
    p = jnp.stack([ca, sin_sign * sa], axis=1)[:, :, :, None] * scale
    q = jnp.stack([-sa, sin_sign * ca], axis=1)[:, :, :, None] * scale
    return (p * cb[:, None, None, :] + q * sb[:, None, None, :]).reshape(n, 2 * n)


def _chan_dft_kernel(a_ref, wc_ref, o_ref, *, groups):
    dg = wc_ref.shape[0]
    wc = wc_ref[...]
    for g in range(groups):
        cols = pl.ds(g * dg, dg)
        res = jnp.dot(a_ref[:, cols], wc, preferred_element_type=F32)
        o_ref[0, :, cols] = res[:, :dg].astype(o_ref.dtype)
        o_ref[1, :, cols] = res[:, dg:].astype(o_ref.dtype)


def _chan_dft(a, wc, nb, ln):
    d = a.shape[1]
    dg = wc.shape[0]
    tm = min(512, ln)
    mt = ln // tm
    return pl.pallas_call(
        functools.partial(_chan_dft_kernel, groups=d // dg),
        grid=(nb, mt),
        in_specs=[
            pl.BlockSpec((tm, d), lambda b, i: (b * mt + i, 0)),
            pl.BlockSpec((dg, 2 * dg), lambda b, i: (0, 0)),
        ],
        out_specs=pl.BlockSpec((None, 2, tm, d), lambda b, i: (b, 0, i, 0)),
        out_shape=jax.ShapeDtypeStruct((nb, 2, ln, d), BF16),
        compiler_params=_cparams("parallel", "parallel"),
        name="chan_dft",
    )(a, wc)


def _pos_dft(cs, y):
    nb, k2, d = y.shape
    ln = cs.shape[0]
    tm, tn = min(1024, ln), 1024
    return pl.pallas_call(
        _mm_kernel,
        grid=(nb, d // tn, ln // tm),
        in_specs=[
            pl.BlockSpec((tm, k2), lambda b, j, i: (i, 0)),
            pl.BlockSpec((None, k2, tn), lambda b, j, i: (b, 0, j)),
        ],
        out_specs=pl.BlockSpec((None, tm, tn), lambda b, j, i: (b, i, j)),
        out_shape=jax.ShapeDtypeStruct((nb, ln, d), BF16),
        compiler_params=_cparams("parallel", "parallel", "parallel"),
        name="pos_dft",
    )(cs, y)


def _fourier(a, nb, ln, wc, cs):
    d = a.shape[1]
    y = _chan_dft(a, wc, nb, ln)
    f = _pos_dft(cs, y.reshape(nb, 2 * ln, d))
    return f.reshape(nb * ln, d)


def _ssd_dt_kernel(a_ref, wt_ref, bias_ref, alog_ref, dtt_ref, acst_ref, *, heads, groups):
    ln = a_ref.shape[0]
    q = SSD_CHUNK
    r = heads // groups
    dtt = jax.nn.softplus(
        lax.dot_general(wt_ref[...], a_ref[...], (((1,), (1,)), ((), ())), preferred_element_type=F32)
        + bias_ref[...])
    dtat = dtt * (-jnp.exp(alog_ref[...]))
    ri = lax.broadcasted_iota(jnp.int32, (q, q), 0)
    ci = lax.broadcasted_iota(jnp.int32, (q, q), 1)
    lower = (ri >= ci).astype(F32)
    upper = (ri <= ci).astype(F32)
    fwd_row = lax.broadcasted_iota(jnp.int32, (2 * heads, q), 0) < heads
    for c in range(ln // q):
        xt = dtat[:, c * q:(c + 1) * q]
        prefix = jnp.dot(xt, upper, precision=_HIGHEST, preferred_element_type=F32)
        suffix = jnp.dot(xt, lower, precision=_HIGHEST, preferred_element_type=F32)
        acs = jnp.where(fwd_row, prefix, suffix)
        dtc = dtt[:, c * q:(c + 1) * q]
        for g in range(groups):
            for half in range(2):
                src = slice(half * heads + g * r, half * heads + (g + 1) * r)
                dst = pl.ds(half * r, r)
                acst_ref[g, c, dst, :] = acs[src, :]
                dtt_ref[g, c, dst, :] = dtc[src, :]


def _ssd_dt(a, w_dt_t, dt_bias, a_log, nb, ln):
    d = a.shape[1]
    h2 = w_dt_t.shape[0]
    groups = SSD_GROUPS
    nc = ln // SSD_CHUNK
    vec = pl.BlockSpec((h2, 1), lambda b: (0, 0))
    out = pl.BlockSpec((None, groups, nc, h2 // groups, SSD_CHUNK), lambda b: (b, 0, 0, 0, 0))
    shape = jax.ShapeDtypeStruct((nb, groups, nc, h2 // groups, SSD_CHUNK), F32)
    return pl.pallas_call(
        functools.partial(_ssd_dt_kernel, heads=h2 // 2, groups=groups),
        grid=(nb,),
        in_specs=[
            pl.BlockSpec((ln, d), lambda b: (b, 0)),
            pl.BlockSpec((h2, d), lambda b: (0, 0)),
            vec, vec,
        ],
        out_specs=[out, out],
        out_shape=[shape, shape],
        compiler_params=_cparams("parallel"),
        name="ssd_dt",
    )(a, w_dt_t, dt_bias.reshape(h2, 1), a_log.reshape(h2, 1))


def _ssd_scan_kernel(*refs, nc, heads_per_group, emit):
    if emit:
        (xs_ref, b_ref, c_ref, z_ref, acst_ref, dtt_ref, h0_ref, d0_ref, d1_ref, ng_ref, o_ref, hfin_ref,
         yf_scr, h_scr, tok_scr) = refs
    else:
        xs_ref, b_ref, acst_ref, dtt_ref, h0_ref, hfin_ref, h_scr, tok_scr = refs
    q = SSD_CHUNK
    r_heads = heads_per_group
    p = SSD_HEAD_DIM
    gw = r_heads * p
    expand = (lax.broadcasted_iota(jnp.int32, (2 * r_heads, 2 * gw), 1) // p
              == lax.broadcasted_iota(jnp.int32, (2 * r_heads, 2 * gw), 0)).astype(BF16)
    li = lax.broadcasted_iota(jnp.int32, (q, q), 0)
    si = lax.broadcasted_iota(jnp.int32, (q, q), 1)
    lane = lax.broadcasted_iota(jnp.int32, (q, gw), 1)
    even_head = (lane % (2 * p)) < p
    pad_rows = jnp.zeros((q - 3 * r_heads, q), F32)
    if emit:
        dskip = d0_ref[...] + d1_ref[...]
        norm_g = ng_ref[...]

    def chunk(c, direction):
        rows = pl.ds(pl.multiple_of(c * q, q), q)
        xs_bf = xs_ref[rows, :]
        bc = b_ref[rows, :]
        tok = tok_scr[direction, c]
        fac = tok[:, :2 * r_heads]
        fac_hi = fac.astype(BF16)
        fac_lo = (fac - fac_hi.astype(F32)).astype(BF16)
        fac_e = (jnp.dot(fac_hi, expand, preferred_element_type=F32)
                 + jnp.dot(fac_lo, expand, preferred_element_type=F32))
        ea_e = fac_e[:, :gw]
        wf_e = fac_e[:, gw:]
        h_decay = ea_e[q - 1:q, :] if direction == 0 else ea_e[0:1, :]
        h = h_scr[...]

        def update_state():
            w = xs_bf * wf_e.astype(BF16)
            h_scr[...] = h * h_decay + lax.dot_general(
                bc, w, (((0,), (0,)), ((), ())), preferred_element_type=F32)

        if not emit:
            update_state()
            return
        cc = c_ref[rows, :]
        src_rows = acst_ref[c] - jnp.log(dtt_ref[c])
        causal = (li >= si) if direction == 0 else (li <= si)
        scores = lax.dot_general(cc, bc, (((1,), (1,)), ((), ())), preferred_element_type=F32)
        y = jnp.dot(cc, h.astype(BF16), preferred_element_type=F32) * ea_e
        xs_even = jnp.where(even_head, xs_bf, jnp.zeros_like(xs_bf))
        xs_odd = jnp.where(even_head, jnp.zeros_like(xs_bf), xs_bf)
        pieces = []
        for pair in range(r_heads // 2):
            ms = []
            for sub in range(2):
                r = 2 * pair + sub
                col = direction * r_heads + r
                seg = tok[:, 2 * r_heads + r:2 * r_heads + r + 1] - src_rows[col:col + 1, :]
                decay = jnp.exp(jnp.where(causal, seg, -jnp.inf))
                ms.append((scores * decay).astype(BF16))
            cols = slice(pair * 2 * p, (pair + 1) * 2 * p)
            pieces.append(jnp.dot(jnp.concatenate(ms, axis=1),
                                  jnp.concatenate([xs_even[:, cols], xs_odd[:, cols]], axis=0),
                                  preferred_element_type=F32))
        y = y + jnp.concatenate(pieces, axis=1)
        update_state()
        if direction == 0:
            yf_scr[rows, :] = y
        else:
            tot = yf_scr[rows, :] + y + dskip * xs_bf.astype(F32)
            gz = tot * _silu(z_ref[rows, :].astype(F32))
            gz = gz * lax.rsqrt(jnp.mean(gz * gz, axis=-1, keepdims=True) + EPS)
            o_ref[rows, :] = (gz * norm_g).astype(o_ref.dtype)

    def to_token_major(c, carry):
        acs_rows = acst_ref[c]
        dt_rows = dtt_ref[c]
        for direction in range(2):
            acs_d = acs_rows[direction * r_heads:(direction + 1) * r_heads, :]
            dt_d = dt_rows[direction * r_heads:(direction + 1) * r_heads, :]
            last = acs_d[:, q - 1:q] if direction == 0 else acs_d[:, 0:1]
            tok_scr[direction, c] = jnp.concatenate(
                [jnp.exp(acs_d), dt_d * jnp.exp(last - acs_d), acs_d, pad_rows], axis=0).T
        return carry

    lax.fori_loop(0, nc, to_token_major, 0, unroll=True)
    h_scr[...] = h0_ref[0]

    def fwd_body(c, carry):
        chunk(c, 0)
        return carry

    lax.fori_loop(0, nc, fwd_body, 0, unroll=min(SCAN_UNROLL, nc))
    hfin_ref[0] = h_scr[...]
    h_scr[...] = h0_ref[1]

    def bwd_body(i, carry):
        chunk(nc - 1 - i, 1)
        return carry

    lax.fori_loop(0, nc, bwd_body, 0, unroll=min(SCAN_UNROLL, nc))
    hfin_ref[1] = h_scr[...]


def _ssd_scan(xb, cm, z, acst4, dtt4, h0, d0e, d1e, norm_g, nb, ln):
    emit = z is not None
    groups = SSD_GROUPS
    n = SSD_STATE
    gw = h0.shape[-1]
    d_inner = gw * groups
    r2 = acst4.shape[-2]
    nc = ln // SSD_CHUNK
    b_blk = d_inner // n
    chunk_g = pl.BlockSpec((None, None, nc, r2, SSD_CHUNK), lambda b, g: (b, g, 0, 0, 0))
    state = pl.BlockSpec((None, None, 2, n, gw), lambda b, g: (b, g, 0, 0, 0))
    vec = pl.BlockSpec((1, gw), lambda b, g: (0, g))
    seq_x = pl.BlockSpec((ln, gw), lambda b, g: (b, g))
    seq_b = pl.BlockSpec((ln, n), lambda b, g: (b, b_blk + g))
    seq_c = pl.BlockSpec((ln, n), lambda b, g: (b, g))
    state_shape = jax.ShapeDtypeStruct(h0.shape, F32)
    scratch = [pltpu.VMEM((n, gw), F32), pltpu.VMEM((2, nc, SSD_CHUNK, SSD_CHUNK), F32)]
    if emit:
        in_specs = [seq_x, seq_b, seq_c, seq_x, chunk_g, chunk_g, state, vec, vec, vec]
        operands = (xb, xb, cm, z, acst4, dtt4, h0, d0e, d1e, norm_g)
        out_specs = [seq_x, state]
        out_shape = [jax.ShapeDtypeStruct((nb * ln, d_inner), BF16), state_shape]
        scratch = [pltpu.VMEM((ln, gw), F32)] + scratch
    else:
        in_specs = [seq_x, seq_b, chunk_g, chunk_g, state]
        operands = (xb, xb, acst4, dtt4, h0)
        out_specs = [state]
        out_shape = [state_shape]
    res = pl.pallas_call(
        functools.partial(_ssd_scan_kernel, nc=nc, heads_per_group=r2 // 2, emit=emit),
        grid=(nb, groups),
        in_specs=in_specs,
        out_specs=out_specs,
        out_shape=out_shape,
        scratch_shapes=scratch,
        compiler_params=_cparams("parallel", "parallel"),
        name="ssd_scan",
    )(*operands)
    return res if emit else (None, res[0])


def _ssd_mixer(a, nb, ln, weights, layer, h0, emit=True):
    w_in, w_c, w_z, w_dt_t, conv_w, conv_b, dt_bias, a_log, d0e, d1e, norm_g = weights
    d_inner = w_z.shape[2]
    n_xb = d_inner + SSD_GROUPS * SSD_STATE
    xb = _proj_conv(a, w_in, layer, conv_w[:, :n_xb], conv_b[:n_xb], width=ln, vertical=False, n_out=n_xb, tn=512)
    cm = z = None
    if emit:
        cm = _proj_conv(a, w_c, layer, conv_w[:, n_xb:], conv_b[n_xb:], width=ln, vertical=False,
                        n_out=w_c.shape[2], tn=512)
        z = _matmul(a, w_z, layer, BF16)
    dtt4, acst4 = _ssd_dt(a, w_dt_t, dt_bias, a_log, nb, ln)
    return _ssd_scan(xb, cm, z, acst4, dtt4, h0, d0e, d1e, norm_g, nb, ln)


def kernel(x, c, ctx, c_ctx, w_mod, b_mod, norm_mix_g, norm_ffn_g, four_w, ssd_w_in, ssd_conv_w, ssd_conv_b,
           ssd_dt_bias, ssd_a_log, ssd_d, ssd_norm_g, ssd_w_out, ffn_w_up, ffn_conv_w, ffn_conv_b, ffn_w_down,
           final_g):
    nb, ln, d = x.shape
    lc = ctx.shape[1]
    depth = w_mod.shape[0]
    d_ff = ffn_w_down.shape[1]
    d_inner = ssd_w_out.shape[1]
    heads = ssd_dt_bias.shape[2]
    gn = SSD_GROUPS * SSD_STATE
    t_lat, t_ctx = nb * ln, nb * lc

    mod_rows = -(-(nb + 1) // SUBLANES) * SUBLANES
    cond = jnp.zeros((mod_rows, d), F32).at[:nb].set(c).at[nb].set(c_ctx)
    mods = _modulation(cond, w_mod, b_mod)

    def lat_mod(i, j):
        return mods[i, :nb, j * d:(j + 1) * d].reshape(nb, 1, d)

    def ctx_mod(i, j):
        return mods[i, nb:nb + 1, j * d:(j + 1) * d].reshape(1, 1, d)

    four_bf = four_w.astype(BF16)
    w_down_bf = ffn_w_down.astype(BF16)
    w_out_bf = ssd_w_out.astype(BF16)
    xb = d_inner + gn
    state_cols = xb + 2 * heads
    w_c_bf = ssd_w_in[:, :, state_cols:state_cols + gn].astype(BF16)
    w_z_bf = ssd_w_in[:, :, state_cols + gn:].astype(BF16)
    w_dt_t_bf = lax.optimization_barrier(ssd_w_in[:, :, xb:state_cols]).transpose(0, 2, 1).astype(BF16)
    conv9 = ffn_conv_w.reshape(depth, 9, d_ff)

    dg = d // FOURIER_GROUPS
    wc = _dft_tables(dg, 1.0).astype(BF16)
    cs_lat = _dft_tables(ln, -1.0).astype(BF16)
    cs_ctx = _dft_tables(lc, -1.0).astype(BF16)

    xl = x.reshape(t_lat, d)
    xc = ctx.reshape(t_ctx, d)
    a_lat = _normmod(xl, norm_mix_g[0], lat_mod(0, 0), lat_mod(0, 1), ln)
    a_ctx = _normmod(xc, norm_mix_g[0], ctx_mod(0, 0), ctx_mod(0, 1), t_ctx)
    out = None
    for i in range(depth):
        last = i == depth - 1
        is_ssd = i % 2 == 1
        j = i // 2
        if is_ssd:
            d0e = jnp.repeat(ssd_d[j, 0], SSD_HEAD_DIM).reshape(1, d_inner)
            d1e = jnp.repeat(ssd_d[j, 1], SSD_HEAD_DIM).reshape(1, d_inner)
            weights = (ssd_w_in, w_c_bf, w_z_bf, w_dt_t_bf[j], ssd_conv_w[j], ssd_conv_b[j], ssd_dt_bias[j], ssd_a_log[j],
                       d0e, d1e, ssd_norm_g[j].reshape(1, d_inner))
            zeros = jnp.zeros((nb, SSD_GROUPS, 2, SSD_STATE, d_inner // SSD_GROUPS), F32)
            mix_ctx, h_ctx = _ssd_mixer(a_ctx, nb, lc, weights, j, zeros, emit=not last)
            mix_lat, _ = _ssd_mixer(a_lat, nb, ln, weights, j, h_ctx)
            w_mix = w_out_bf
        else:
            mix_lat = _fourier(a_lat, nb, ln, wc, cs_lat)
            mix_ctx = None if last else _fourier(a_ctx, nb, lc, wc, cs_ctx)
            w_mix = four_bf
        xl, b_lat = _matmul_residual(mix_lat, w_mix, j, xl, lat_mod(i, 2), norm_ffn_g[i], lat_mod(i, 3),
                                     lat_mod(i, 4), ln)
        act = _proj_conv(b_lat, ffn_w_up, i, conv9[i], ffn_conv_b[i], width=GRID_W, vertical=True,
                         n_out=d_ff, val_offset=d_ff, tn=256)
        if last:
            zero_mod = jnp.zeros((nb, 1, d), F32)
            _, out = _matmul_residual(act, w_down_bf, i, xl, lat_mod(i, 5), final_g, zero_mod, zero_mod, ln,
                                      modulate=False, a_dtype=F32)
        else:
            xl, a_lat = _matmul_residual(act, w_down_bf, i, xl, lat_mod(i, 5), norm_mix_g[i + 1],
                                         lat_mod(i + 1, 0), lat_mod(i + 1, 1), ln)
            xc, b_ctx = _matmul_residual(mix_ctx, w_mix, j, xc, ctx_mod(i, 2), norm_ffn_g[i], ctx_mod(i, 3),
                                         ctx_mod(i, 4), t_ctx)
            act_c = _proj_conv(b_ctx, ffn_w_up, i, ffn_conv_w[i, 1], ffn_conv_b[i], width=lc, vertical=False,
                               n_out=d_ff, val_offset=d_ff, tn=256)
            xc, a_ctx = _matmul_residual(act_c, w_down_bf, i, xc, ctx_mod(i, 5), norm_mix_g[i + 1],
                                         ctx_mod(i + 1, 0), ctx_mod(i + 1, 1), t_ctx)
    return out.reshape(nb, ln, d)
```

```python
import functools
import math

import jax
import jax.numpy as jnp
from jax import lax
from jax.experimental import pallas as pl
from jax.experimental.pallas import tpu as pltpu

F32 = jnp.float32
BF16 = jnp.bfloat16

EPS = 1e-6
GRID_W = 64
FOURIER_GROUPS = 8
SSD_HEAD_DIM = 64
SSD_GROUPS = 8
SSD_STATE = 128
SSD_CHUNK = 128
SCAN_UNROLL = 16

LANES = 128
SUBLANES = 8
VMEM_LIMIT_BYTES = 56 * 1024 * 1024
ROW_TILE = 2048
EPILOGUE_ROWS = 256
PROJ_BLOCK_ROWS = 512
GATED_BLOCK_ROWS = 256

_HIGHEST = lax.Precision.HIGHEST


def _cparams(*sem, flags=None):
    return pltpu.CompilerParams(dimension_semantics=sem, vmem_limit_bytes=VMEM_LIMIT_BYTES, flags=flags)


def _norm_rows(x, g, shift, scale, modulate):
    ms = jnp.mean(x * x, axis=-1, keepdims=True)
    y = x * lax.rsqrt(ms + EPS) * g
    if modulate:
        y = y * (1.0 + scale) + shift
    return y


def _silu(v):
    return v * jax.nn.sigmoid(v)


def _mod_kernel(s_ref, w_ref, b_ref, o_ref):
    s = _silu(s_ref[...]).astype(BF16)
    w = w_ref[0].astype(BF16)
    o_ref[0] = jnp.dot(s, w, preferred_element_type=F32) + b_ref[0]


def _modulation(cond, w_mod, b_mod):
    depth, d, n = w_mod.shape
    rows = cond.shape[0]
    tn = 1024
    return pl.pallas_call(
        _mod_kernel,
        grid=(depth, n // tn),
        in_specs=[
            pl.BlockSpec((rows, d), lambda i, j: (0, 0)),
            pl.BlockSpec((1, d, tn), lambda i, j: (i, 0, j)),
            pl.BlockSpec((1, 1, tn), lambda i, j: (i, 0, j)),
        ],
        out_specs=pl.BlockSpec((1, rows, tn), lambda i, j: (i, 0, j)),
        out_shape=jax.ShapeDtypeStruct((depth, rows, n), F32),
        compiler_params=_cparams("parallel", "parallel"),
        name="modulation",
    )(cond, w_mod, b_mod.reshape(depth, 1, n))


def _normmod_kernel(x_ref, g_ref, sh_ref, sc_ref, o_ref):
    o_ref[...] = _norm_rows(x_ref[...], g_ref[...], sh_ref[0], sc_ref[0], True).astype(o_ref.dtype)


def _normmod(x, g, shift, scale, rows_per_mod):
    m, d = x.shape
    tm = 512
    mod_spec = pl.BlockSpec((1, 1, d), lambda i: ((i * tm) // rows_per_mod, 0, 0))
    return pl.pallas_call(
        _normmod_kernel,
        grid=(m // tm,),
        in_specs=[
            pl.BlockSpec((tm, d), lambda i: (i, 0)),
            pl.BlockSpec((1, d), lambda i: (0, 0)),
            mod_spec,
            mod_spec,
        ],
        out_specs=pl.BlockSpec((tm, d), lambda i: (i, 0)),
        out_shape=jax.ShapeDtypeStruct((m, d), BF16),
        compiler_params=_cparams("parallel"),
        name="normmod",
    )(x, g.reshape(1, d), shift, scale)


def _mm_kernel(a_ref, w_ref, o_ref):
    o_ref[...] = jnp.dot(a_ref[...], w_ref[...], preferred_element_type=F32).astype(o_ref.dtype)


def _matmul(a, w, layer, out_dtype, tm=1024, tn=1024):
    m, k = a.shape
    n = w.shape[2]
    tm, tn = min(tm, m), min(tn, n)
    return pl.pallas_call(
        _mm_kernel,
        grid=(m // tm, n // tn),
        in_specs=[
            pl.BlockSpec((tm, k), lambda i, j: (i, 0)),
            pl.BlockSpec((None, k, tn), lambda i, j: (layer, 0, j)),
        ],
        out_specs=pl.BlockSpec((tm, tn), lambda i, j: (i, j)),
        out_shape=jax.ShapeDtypeStruct((m, n), out_dtype),
        compiler_params=_cparams("parallel", "parallel"),
        name="matmul",
    )(a, w)


def _mm_res_kernel(a_ref, w_ref, x_ref, gate_ref, g_ref, sh_ref, sc_ref, xo_ref, ao_ref, xn_scr, ssq_scr,
                   *, nn, modulate):
    j = pl.program_id(1)
    tn = x_ref.shape[1]
    w = w_ref[:, pl.ds(pl.multiple_of(j * tn, tn), tn)]
    xn = x_ref[...] + gate_ref[0] * jnp.dot(a_ref[...], w, preferred_element_type=F32)
    xo_ref[...] = xn
    xn_scr[j] = xn
    ssq = jnp.sum(xn * xn, axis=-1, keepdims=True)

    @pl.when(j == 0)
    def _():
        ssq_scr[...] = ssq

    @pl.when(j > 0)
    def _():
        ssq_scr[...] += ssq

    @pl.when(j == nn - 1)
    def _():
        inv = lax.rsqrt(ssq_scr[...] * (1.0 / (nn * tn)) + EPS)
        for c in range(nn):
            cols = slice(c * tn, (c + 1) * tn)
            y = xn_scr[c] * inv * g_ref[:, cols]
            if modulate:
                y = y * (1.0 + sc_ref[0][:, cols]) + sh_ref[0][:, cols]
            ao_ref[:, cols] = y.astype(ao_ref.dtype)


def _residual_tile_cols(k):
    return 1024 if k <= 4096 else 512


def _matmul_residual(a, w, layer, x, gate, g_next, shift, scale, rows_per_mod, modulate=True, a_dtype=BF16):
    m, k = a.shape
    d = w.shape[2]
    tn = _residual_tile_cols(k)
    nn = d // tn
    tm = min(512, m)

    def mod_row(i, j):
        return ((i * tm) // rows_per_mod, 0, 0)

    mod_spec = pl.BlockSpec((1, 1, d), mod_row)
    return pl.pallas_call(
        functools.partial(_mm_res_kernel, nn=nn, modulate=modulate),
        grid=(m // tm, nn),
        in_specs=[
            pl.BlockSpec((tm, k), lambda i, j: (i, 0)),
            pl.BlockSpec((None, k, d), lambda i, j: (layer, 0, 0), pipeline_mode=pl.Buffered(1)),
            pl.BlockSpec((tm, tn), lambda i, j: (i, j)),
            pl.BlockSpec((1, 1, tn), lambda i, j: ((i * tm) // rows_per_mod, 0, j)),
            pl.BlockSpec((1, d), lambda i, j: (0, 0)),
            mod_spec,
            mod_spec,
        ],
        out_specs=[
            pl.BlockSpec((tm, tn), lambda i, j: (i, j)),
            pl.BlockSpec((tm, d), lambda i, j: (i, 0)),
        ],
        out_shape=[
            jax.ShapeDtypeStruct((m, d), F32),
            jax.ShapeDtypeStruct((m, d), a_dtype),
        ],
        scratch_shapes=[pltpu.VMEM((nn, tm, tn), F32), pltpu.VMEM((tm, 1), F32)],
        compiler_params=_cparams("parallel", "arbitrary"),
        name="matmul_residual",
    )(a, w, x, gate, g_next.reshape(1, d), shift, scale)


def _proj_conv_kernel(*refs, rows, width, vertical, gated):
    refs = list(refs)
    a_ref, wg_ref = refs[:2]
    wv_ref = refs.pop(2) if gated else None
    cw_ref, cb_ref, o_ref, u_scr = refs[2:6]
    scratch = refs[6:]
    if vertical:
        (ul_scr, ur_scr), scratch = scratch[:2], scratch[2:]
    if gated:
        v_scr, scratch = scratch[0], scratch[1:]
    if wg_ref.dtype != BF16:
        scratch[0][...] = wg_ref[...].astype(BF16)
        wg_ref = scratch[0]
        if gated:
            scratch[1][...] = wv_ref[...].astype(BF16)
            wv_ref = scratch[1]
    tn = o_ref.shape[1]
    pad = (u_scr.shape[0] - rows) // 2
    rb = min(GATED_BLOCK_ROWS if gated else PROJ_BLOCK_ROWS, rows)
    rc = min(EPILOGUE_ROWS, rb)
    zeros = jnp.zeros((pad, tn), F32)
    for buf in [u_scr] + ([ul_scr, ur_scr] if vertical else []):
        buf[pl.ds(0, pad), :] = zeros
        buf[pl.ds(pad + rows, pad), :] = zeros
    cw = cw_ref[...]
    bias = cb_ref[...]
    row = lax.broadcasted_iota(jnp.int32, (rc, tn), 0)
    everything = slice(0, tn)

    def project(blk, cols=everything):
        a = a_ref[pl.ds(blk * rb, rb), :]
        u_scr[pl.ds(pad + blk * rb, rb), cols] = jnp.dot(a, wg_ref[:, cols], preferred_element_type=F32)
        if vertical:
            col = row & (width - 1)
            for r in range(blk * rb // rc, (blk + 1) * rb // rc):
                chunk = u_scr[pl.ds(pad + r * rc, rc), :]
                ul_scr[pl.ds(pad + r * rc, rc), :] = jnp.where(col != 0, pltpu.roll(chunk, 1, 0), 0.0)
                ur_scr[pl.ds(pad + r * rc, rc), :] = jnp.where(col != width - 1, pltpu.roll(chunk, rc - 1, 0), 0.0)

    def convolve(blk, cols=everything):
        for r in range(blk * rb // rc, (blk + 1) * rb // rc):
            base = pad + r * rc
            if vertical:
                acc = bias
                for di in range(3):
                    off = base + (di - 1) * width
                    acc = acc + ul_scr[pl.ds(off, rc), :] * cw[3 * di:3 * di + 1, :]
                    acc = acc + u_scr[pl.ds(off, rc), :] * cw[3 * di + 1:3 * di + 2, :]
                    acc = acc + ur_scr[pl.ds(off, rc), :] * cw[3 * di + 2:3 * di + 3, :]
            else:
                col = (lax.broadcasted_iota(jnp.int32, (rc, cols.stop - cols.start), 0) + r * rc) & (width - 1)
                left = jnp.where(col != 0, u_scr[pl.ds(base - 1, rc), cols], 0.0)
                right = jnp.where(col != width - 1, u_scr[pl.ds(base + 1, rc), cols], 0.0)
                acc = (left * cw_ref[0:1, cols] + u_scr[pl.ds(base, rc), cols] * cw_ref[1:2, cols]
                       + right * cw_ref[2:3, cols] + cb_ref[:, cols])
            y = _silu(acc)
            if gated:
                v_scr[pl.ds(r * rc, rc), :] = y
            else:
                o_ref[pl.ds(r * rc, rc), cols] = y.astype(o_ref.dtype)

    n_blocks = rows // rb
    for blk in range(n_blocks):
        project(blk)
        if blk >= 1:
            convolve(blk - 1)
    if gated:
        val = jnp.dot(a_ref[...], wv_ref[...], preferred_element_type=F32)
    convolve(n_blocks - 1)
    if gated:
        for r in range(rows // rc):
            chunk = pl.ds(r * rc, rc)
            o_ref[chunk, :] = (v_scr[chunk, :] * val[r * rc:(r + 1) * rc, :]).astype(o_ref.dtype)


def _proj_conv(a, w, layer, conv_w, conv_b, *, width, vertical, n_out, val_offset=None, tn):
    m, k = a.shape
    rows = min(ROW_TILE, m)
    gated = val_offset is not None
    assert width & (width - 1) == 0 and rows % width == 0 and m % rows == 0
    pad = (width + SUBLANES) if vertical else SUBLANES
    taps = conv_w.shape[0]
    in_specs = [
        pl.BlockSpec((rows, k), lambda i, j: (i, 0)),
        pl.BlockSpec((None, k, tn), lambda i, j: (layer, 0, j)),
    ]
    operands = [a, w]
    padded = pltpu.VMEM((rows + 2 * pad, tn), F32)
    scratch = [padded] + ([padded, padded] if vertical else []) + ([pltpu.VMEM((rows, tn), F32)] if gated else [])
    if w.dtype != BF16:
        scratch += [pltpu.VMEM((k, tn), BF16)] * (2 if gated else 1)
    if gated:
        voff = val_offset // tn
        in_specs.append(pl.BlockSpec((None, k, tn), lambda i, j: (layer, 0, j + voff)))
        operands.append(w)
    in_specs += [
        pl.BlockSpec((taps, tn), lambda i, j: (0, j)),
        pl.BlockSpec((1, tn), lambda i, j: (0, j)),
    ]
    operands += [conv_w, conv_b.reshape(1, n_out)]
    return pl.pallas_call(
        functools.partial(_proj_conv_kernel, rows=rows, width=width, vertical=vertical, gated=gated),
        grid=(m // rows, n_out // tn),
        in_specs=in_specs,
        out_specs=pl.BlockSpec((rows, tn), lambda i, j: (i, j)),
        out_shape=jax.ShapeDtypeStruct((m, n_out), BF16),
        scratch_shapes=scratch,
        compiler_params=_cparams("parallel", "parallel"),
        name="proj_conv",
    )(*operands)


def _dft_tables(n, sin_sign):
    s = 64
    k = jnp.arange(n, dtype=jnp.int32)[:, None]

    def narrow(cols):
        ang = ((k * cols[None, :]) % n).astype(F32) * F32(2.0 * math.pi / n)
        return jnp.cos(ang), jnp.sin(ang)

    ca, sa = narrow(jnp.arange(n // s, dtype=jnp.int32) * s)
    cb, sb = narrow(jnp.arange(s, dtype=jnp.int32))
    scale = F32(1.0 / math.sqrt(n))
    p = jnp.stack([ca, sin_sign * sa], axis=1)[:, :, :, None] * scale
    q = jnp.stack([-sa, sin_sign * ca], axis=1)[:, :, :, None] * scale
    return (p * cb[:, None, None, :] + q * sb[:, None, None, :]).reshape(n, 2 * n)


def _chan_dft_kernel(a_ref, wc_ref, o_ref, *, groups):
    dg = wc_ref.shape[0]
    wc = wc_ref[...]
    for g in range(groups):
        cols = pl.ds(g * dg, dg)
        res = jnp.dot(a_ref[:, cols], wc, preferred_element_type=F32)
        o_ref[0, :, cols] = res[:, :dg].astype(o_ref.dtype)
        o_ref[1, :, cols] = res[:, dg:].astype(o_ref.dtype)


def _chan_dft(a, wc, nb, ln):
    d = a.shape[1]
    dg = wc.shape[0]
    tm = min(512, ln)
    mt = ln // tm
    return pl.pallas_call(
        functools.partial(_chan_dft_kernel, groups=d // dg),
        grid=(nb, mt),
        in_specs=[
            pl.BlockSpec((tm, d), lambda b, i: (b * mt + i, 0)),
            pl.BlockSpec((dg, 2 * dg), lambda b, i: (0, 0)),
        ],
        out_specs=pl.BlockSpec((None, 2, tm, d), lambda b, i: (b, 0, i, 0)),
        out_shape=jax.ShapeDtypeStruct((nb, 2, ln, d), BF16),
        compiler_params=_cparams("parallel", "parallel"),
        name="chan_dft",
    )(a, wc)


def _pos_dft(cs, y):
    nb, k2, d = y.shape
    ln = cs.shape[0]
    tm, tn = min(1024, ln), 1024
    return pl.pallas_call(
        _mm_kernel,
        grid=(nb, d // tn, ln // tm),
        in_specs=[
            pl.BlockSpec((tm, k2), lambda b, j, i: (i, 0)),
            pl.BlockSpec((None, k2, tn), lambda b, j, i: (b, 0, j)),
        ],
        out_specs=pl.BlockSpec((None, tm, tn), lambda b, j, i: (b, i, j)),
        out_shape=jax.ShapeDtypeStruct((nb, ln, d), BF16),
        compiler_params=_cparams("parallel", "parallel", "parallel"),
        name="pos_dft",
    )(cs, y)


def _fourier(a, nb, ln, wc, cs):
    d = a.shape[1]
    y = _chan_dft(a, wc, nb, ln)
    f = _pos_dft(cs, y.reshape(nb, 2 * ln, d))
    return f.reshape(nb * ln, d)


def _ssd_dt_kernel(a_ref, wt_ref, bias_ref, alog_ref, dtt_ref, acst_ref, *, heads, groups):
    ln = a_ref.shape[0]
    q = SSD_CHUNK
    r = heads // groups
    dtt = jax.nn.softplus(
        lax.dot_general(wt_ref[...], a_ref[...], (((1,), (1,)), ((), ())), preferred_element_type=F32)
        + bias_ref[...])
    dtat = dtt * (-jnp.exp(alog_ref[...]))
    ri = lax.broadcasted_iota(jnp.int32, (q, q), 0)
    ci = lax.broadcasted_iota(jnp.int32, (q, q), 1)
    lower = (ri >= ci).astype(F32)
    upper = (ri <= ci).astype(F32)
    fwd_row = lax.broadcasted_iota(jnp.int32, (2 * heads, q), 0) < heads
    for c in range(ln // q):
        xt = dtat[:, c * q:(c + 1) * q]
        prefix = jnp.dot(xt, upper, precision=_HIGHEST, preferred_element_type=F32)
        suffix = jnp.dot(xt, lower, precision=_HIGHEST, preferred_element_type=F32)
        acs = jnp.where(fwd_row, prefix, suffix)
        dtc = dtt[:, c * q:(c + 1) * q]
        for g in range(groups):
            for half in range(2):
                src = slice(half * heads + g * r, half * heads + (g + 1) * r)
                dst = pl.ds(half * r, r)
                acst_ref[g, c, dst, :] = acs[src, :]
                dtt_ref[g, c, dst, :] = dtc[src, :]


def _ssd_dt(a, w_dt_t, dt_bias, a_log, nb, ln):
    d = a.shape[1]
    h2 = w_dt_t.shape[0]
    groups = SSD_GROUPS
    nc = ln // SSD_CHUNK
    vec = pl.BlockSpec((h2, 1), lambda b: (0, 0))
    out = pl.BlockSpec((None, groups, nc, h2 // groups, SSD_CHUNK), lambda b: (b, 0, 0, 0, 0))
    shape = jax.ShapeDtypeStruct((nb, groups, nc, h2 // groups, SSD_CHUNK), F32)
    return pl.pallas_call(
        functools.partial(_ssd_dt_kernel, heads=h2 // 2, groups=groups),
        grid=(nb,),
        in_specs=[
            pl.BlockSpec((ln, d), lambda b: (b, 0)),
            pl.BlockSpec((h2, d), lambda b: (0, 0)),
            vec, vec,
        ],
        out_specs=[out, out],
        out_shape=[shape, shape],
        compiler_params=_cparams("parallel"),
        name="ssd_dt",
    )(a, w_dt_t, dt_bias.reshape(h2, 1), a_log.reshape(h2, 1))


def _ssd_scan_kernel(*refs, nc, heads_per_group, emit):
    if emit:
        (xs_ref, b_ref, c_ref, z_ref, acst_ref, dtt_ref, h0_ref, d0_ref, d1_ref, ng_ref, o_ref, hfin_ref,
         yf_scr, h_scr, tok_scr) = refs
    else:
        xs_ref, b_ref, acst_ref, dtt_ref, h0_ref, hfin_ref, h_scr, tok_scr = refs
    q = SSD_CHUNK
    r_heads = heads_per_group
    p = SSD_HEAD_DIM
    gw = r_heads * p
    expand = (lax.broadcasted_iota(jnp.int32, (2 * r_heads, 2 * gw), 1) // p
              == lax.broadcasted_iota(jnp.int32, (2 * r_heads, 2 * gw), 0)).astype(BF16)
    li = lax.broadcasted_iota(jnp.int32, (q, q), 0)
    si = lax.broadcasted_iota(jnp.int32, (q, q), 1)
    lane = lax.broadcasted_iota(jnp.int32, (q, gw), 1)
    even_head = (lane % (2 * p)) < p
    pad_rows = jnp.zeros((q - 3 * r_heads, q), F32)
    if emit:
        dskip = d0_ref[...] + d1_ref[...]
        norm_g = ng_ref[...]

    def chunk(c, direction):
        rows = pl.ds(pl.multiple_of(c * q, q), q)
        xs_bf = xs_ref[rows, :]
        bc = b_ref[rows, :]
        tok = tok_scr[direction, c]
        fac = tok[:, :2 * r_heads]
        fac_e = jnp.dot(fac.astype(BF16), expand, preferred_element_type=F32)
        ea_e = fac_e[:, :gw]
        wf_e = fac_e[:, gw:]
        h_decay = ea_e[q - 1:q, :] if direction == 0 else ea_e[0:1, :]
        h = h_scr[...]

        def update_state():
            w = xs_bf * wf_e.astype(BF16)
            h_scr[...] = h * h_decay + lax.dot_general(
                bc, w, (((0,), (0,)), ((), ())), preferred_element_type=F32)

        if not emit:
            update_state()
            return
        cc = c_ref[rows, :]
        src_rows = acst_ref[c] - jnp.log(dtt_ref[c])
        causal = (li >= si) if direction == 0 else (li <= si)
        scores = lax.dot_general(cc, bc, (((1,), (1,)), ((), ())), preferred_element_type=F32)
        y = jnp.dot(cc, h.astype(BF16), preferred_element_type=F32) * ea_e
        xs_even = jnp.where(even_head, xs_bf, jnp.zeros_like(xs_bf))
        xs_odd = jnp.where(even_head, jnp.zeros_like(xs_bf), xs_bf)
        pieces = []
        for pair in range(r_heads // 2):
            ms = []
            for sub in range(2):
                r = 2 * pair + sub
                col = direction * r_heads + r
                seg = tok[:, 2 * r_heads + r:2 * r_heads + r + 1] - src_rows[col:col + 1, :]
                decay = jnp.exp(jnp.where(causal, seg, -jnp.inf))
                ms.append((scores * decay).astype(BF16))
            cols = slice(pair * 2 * p, (pair + 1) * 2 * p)
            pieces.append(jnp.dot(jnp.concatenate(ms, axis=1),
                                  jnp.concatenate([xs_even[:, cols], xs_odd[:, cols]], axis=0),
                                  preferred_element_type=F32))
        y = y + jnp.concatenate(pieces, axis=1)
        update_state()
        if direction == 0:
            yf_scr[rows, :] = y
        else:
            tot = yf_scr[rows, :] + y + dskip * xs_bf.astype(F32)
            gz = tot * _silu(z_ref[rows, :].astype(F32))
            gz = gz * lax.rsqrt(jnp.mean(gz * gz, axis=-1, keepdims=True) + EPS)
            o_ref[rows, :] = (gz * norm_g).astype(o_ref.dtype)

    def to_token_major(c, carry):
        acs_rows = acst_ref[c]
        dt_rows = dtt_ref[c]
        for direction in range(2):
            acs_d = acs_rows[direction * r_heads:(direction + 1) * r_heads, :]
            dt_d = dt_rows[direction * r_heads:(direction + 1) * r_heads, :]
            last = acs_d[:, q - 1:q] if direction == 0 else acs_d[:, 0:1]
            tok_scr[direction, c] = jnp.concatenate(
                [jnp.exp(acs_d), dt_d * jnp.exp(last - acs_d), acs_d, pad_rows], axis=0).T
        return carry

    lax.fori_loop(0, nc, to_token_major, 0, unroll=True)
    h_scr[...] = h0_ref[0]

    def fwd_body(c, carry):
        chunk(c, 0)
        return carry

    lax.fori_loop(0, nc, fwd_body, 0, unroll=min(SCAN_UNROLL, nc))
    hfin_ref[0] = h_scr[...]
    h_scr[...] = h0_ref[1]

    def bwd_body(i, carry):
        chunk(nc - 1 - i, 1)
        return carry

    lax.fori_loop(0, nc, bwd_body, 0, unroll=min(SCAN_UNROLL, nc))
    hfin_ref[1] = h_scr[...]


def _ssd_scan(xb, cm, z, acst4, dtt4, h0, d0e, d1e, norm_g, nb, ln):
    emit = z is not None
    groups = SSD_GROUPS
    n = SSD_STATE
    gw = h0.shape[-1]
    d_inner = gw * groups
    r2 = acst4.shape[-2]
    nc = ln // SSD_CHUNK
    b_blk = d_inner // n
    chunk_g = pl.BlockSpec((None, None, nc, r2, SSD_CHUNK), lambda b, g: (b, g, 0, 0, 0))
    state = pl.BlockSpec((None, None, 2, n, gw), lambda b, g: (b, g, 0, 0, 0))
    vec = pl.BlockSpec((1, gw), lambda b, g: (0, g))
    seq_x = pl.BlockSpec((ln, gw), lambda b, g: (b, g))
    seq_b = pl.BlockSpec((ln, n), lambda b, g: (b, b_blk + g))
    seq_c = pl.BlockSpec((ln, n), lambda b, g: (b, g))
    state_shape = jax.ShapeDtypeStruct(h0.shape, F32)
    scratch = [pltpu.VMEM((n, gw), F32), pltpu.VMEM((2, nc, SSD_CHUNK, SSD_CHUNK), F32)]
    if emit:
        in_specs = [seq_x, seq_b, seq_c, seq_x, chunk_g, chunk_g, state, vec, vec, vec]
        operands = (xb, xb, cm, z, acst4, dtt4, h0, d0e, d1e, norm_g)
        out_specs = [seq_x, state]
        out_shape = [jax.ShapeDtypeStruct((nb * ln, d_inner), BF16), state_shape]
        scratch = [pltpu.VMEM((ln, gw), F32)] + scratch
    else:
        in_specs = [seq_x, seq_b, chunk_g, chunk_g, state]
        operands = (xb, xb, acst4, dtt4, h0)
        out_specs = [state]
        out_shape = [state_shape]
    res = pl.pallas_call(
        functools.partial(_ssd_scan_kernel, nc=nc, heads_per_group=r2 // 2, emit=emit),
        grid=(nb, groups),
        in_specs=in_specs,
        out_specs=out_specs,
        out_shape=out_shape,
        scratch_shapes=scratch,
        compiler_params=_cparams("parallel", "parallel"),
        name="ssd_scan",
    )(*operands)
    return res if emit else (None, res[0])


def _ssd_mixer(a, nb, ln, weights, layer, h0, emit=True):
    w_in, w_c, w_z, w_dt_t, conv_w, conv_b, dt_bias, a_log, d0e, d1e, norm_g = weights
    d_inner = w_z.shape[2]
    n_xb = d_inner + SSD_GROUPS * SSD_STATE
    xb = _proj_conv(a, w_in, layer, conv_w[:, :n_xb], conv_b[:n_xb], width=ln, vertical=False, n_out=n_xb, tn=512)
    cm = z = None
    if emit:
        cm = _proj_conv(a, w_c, layer, conv_w[:, n_xb:], conv_b[n_xb:], width=ln, vertical=False,
                        n_out=w_c.shape[2], tn=512)
        z = _matmul(a, w_z, layer, BF16)
    dtt4, acst4 = _ssd_dt(a, w_dt_t, dt_bias, a_log, nb, ln)
    return _ssd_scan(xb, cm, z, acst4, dtt4, h0, d0e, d1e, norm_g, nb, ln)


def kernel(x, c, ctx, c_ctx, w_mod, b_mod, norm_mix_g, norm_ffn_g, four_w, ssd_w_in, ssd_conv_w, ssd_conv_b,
           ssd_dt_bias, ssd_a_log, ssd_d, ssd_norm_g, ssd_w_out, ffn_w_up, ffn_conv_w, ffn_conv_b, ffn_w_down,
           final_g):
    nb, ln, d = x.shape
    lc = ctx.shape[1]
    depth = w_mod.shape[0]
    d_ff = ffn_w_down.shape[1]
    d_inner = ssd_w_out.shape[1]
    heads = ssd_dt_bias.shape[2]
    gn = SSD_GROUPS * SSD_STATE
    t_lat, t_ctx = nb * ln, nb * lc

    mod_rows = -(-(nb + 1) // SUBLANES) * SUBLANES
    cond = jnp.zeros((mod_rows, d), F32).at[:nb].set(c).at[nb].set(c_ctx)
    mods = _modulation(cond, w_mod, b_mod)

    def lat_mod(i, j):
        return mods[i, :nb, j * d:(j + 1) * d].reshape(nb, 1, d)

    def ctx_mod(i, j):
        return mods[i, nb:nb + 1, j * d:(j + 1) * d].reshape(1, 1, d)

    four_bf = four_w.astype(BF16)
    w_down_bf = ffn_w_down.astype(BF16)
    w_out_bf = ssd_w_out.astype(BF16)
    xb = d_inner + gn
    state_cols = xb + 2 * heads
    w_c_bf = ssd_w_in[:, :, state_cols:state_cols + gn].astype(BF16)
    w_z_bf = ssd_w_in[:, :, state_cols + gn:].astype(BF16)
    w_dt_t_bf = lax.optimization_barrier(ssd_w_in[:, :, xb:state_cols]).transpose(0, 2, 1).astype(BF16)
    conv9 = ffn_conv_w.reshape(depth, 9, d_ff)

    dg = d // FOURIER_GROUPS
    wc = _dft_tables(dg, 1.0).astype(BF16)
    cs_lat = _dft_tables(ln, -1.0).astype(BF16)
    cs_ctx = _dft_tables(lc, -1.0).astype(BF16)

    xl = x.reshape(t_lat, d)
    xc = ctx.reshape(t_ctx, d)
    a_lat = _normmod(xl, norm_mix_g[0], lat_mod(0, 0), lat_mod(0, 1), ln)
    a_ctx = _normmod(xc, norm_mix_g[0], ctx_mod(0, 0), ctx_mod(0, 1), t_ctx)
    out = None
    for i in range(depth):
        last = i == depth - 1
        is_ssd = i % 2 == 1
        j = i // 2
        if is_ssd:
            d0e = jnp.repeat(ssd_d[j, 0], SSD_HEAD_DIM).reshape(1, d_inner)
            d1e = jnp.repeat(ssd_d[j, 1], SSD_HEAD_DIM).reshape(1, d_inner)
            weights = (ssd_w_in, w_c_bf, w_z_bf, w_dt_t_bf[j], ssd_conv_w[j], ssd_conv_b[j], ssd_dt_bias[j], ssd_a_log[j],
                       d0e, d1e, ssd_norm_g[j].reshape(1, d_inner))
            zeros = jnp.zeros((nb, SSD_GROUPS, 2, SSD_STATE, d_inner // SSD_GROUPS), F32)
            mix_ctx, h_ctx = _ssd_mixer(a_ctx, nb, lc, weights, j, zeros, emit=not last)
            mix_lat, _ = _ssd_mixer(a_lat, nb, ln, weights, j, h_ctx)
            w_mix = w_out_bf
        else:
            mix_lat = _fourier(a_lat, nb, ln, wc, cs_lat)
            mix_ctx = None if last else _fourier(a_ctx, nb, lc, wc, cs_ctx)
            w_mix = four_bf
        xl, b_lat = _matmul_residual(mix_lat, w_mix, j, xl, lat_mod(i, 2), norm_ffn_g[i], lat_mod(i, 3),
                                     lat_mod(i, 4), ln)
        act = _proj_conv(b_lat, ffn_w_up, i, conv9[i], ffn_conv_b[i], width=GRID_W, vertical=True,
                         n_out=d_ff, val_offset=d_ff, tn=256)
        if last:
            zero_mod = jnp.zeros((nb, 1, d), F32)
            _, out = _matmul_residual(act, w_down_bf, i, xl, lat_mod(i, 5), final_g, zero_mod, zero_mod, ln,
                                      modulate=False, a_dtype=F32)
        else:
            xl, a_lat = _matmul_residual(act, w_down_bf, i, xl, lat_mod(i, 5), norm_mix_g[i + 1],
                                         lat_mod(i + 1, 0), lat_mod(i + 1, 1), ln)
            xc, b_ctx = _matmul_residual(mix_ctx, w_mix, j, xc, ctx_mod(i, 2), norm_ffn_g[i], ctx_mod(i, 3),
                                         ctx_mod(i, 4), t_ctx)
            act_c = _proj_conv(b_ctx, ffn_w_up, i, ffn_conv_w[i, 1], ffn_conv_b[i], width=lc, vertical=False,
                               n_out=d_ff, val_offset=d_ff, tn=256)
            xc, a_ctx = _matmul_residual(act_c, w_down_bf, i, xc, ctx_mod(i, 5), norm_mix_g[i + 1],
                                         ctx_mod(i + 1, 0), ctx_mod(i + 1, 1), t_ctx)
    return out.reshape(nb, ln, d)
```

```python
import functools
import math

import jax
import jax.numpy as jnp
from jax import lax
from jax.experimental import pallas as pl
from jax.experimental.pallas import tpu as pltpu

F32 = jnp.float32
BF16 = jnp.bfloat16

EPS = 1e-6
GRID_W = 64
FOURIER_GROUPS = 8
SSD_HEAD_DIM = 64
SSD_GROUPS = 8
SSD_STATE = 128
SSD_CHUNK = 128
SCAN_UNROLL = 16

LANES = 128
SUBLANES = 8
VMEM_LIMIT_BYTES = 56 * 1024 * 1024
ROW_TILE = 2048
EPILOGUE_ROWS = 256
PROJ_BLOCK_ROWS = 512
GATED_BLOCK_ROWS = 256

_HIGHEST = lax.Precision.HIGHEST


def _cparams(*sem, flags=None):
    return pltpu.CompilerParams(dimension_semantics=sem, vmem_limit_bytes=VMEM_LIMIT_BYTES, flags=flags)


def _norm_rows(x, g, shift, scale, modulate):
    ms = jnp.mean(x * x, axis=-1, keepdims=True)
    y = x * lax.rsqrt(ms + EPS) * g
    if modulate:
        y = y * (1.0 + scale) + shift
    return y


def _silu(v):
    return v * jax.nn.sigmoid(v)


def _mod_kernel(s_ref, w_ref, b_ref, o_ref):
    s = _silu(s_ref[...]).astype(BF16)
    w = w_ref[0].astype(BF16)
    o_ref[0] = jnp.dot(s, w, preferred_element_type=F32) + b_ref[0]


def _modulation(cond, w_mod, b_mod):
    depth, d, n = w_mod.shape
    rows = cond.shape[0]
    tn = 1024
    return pl.pallas_call(
        _mod_kernel,
        grid=(depth, n // tn),
        in_specs=[
            pl.BlockSpec((rows, d), lambda i, j: (0, 0)),
            pl.BlockSpec((1, d, tn), lambda i, j: (i, 0, j)),
            pl.BlockSpec((1, 1, tn), lambda i, j: (i, 0, j)),
        ],
        out_specs=pl.BlockSpec((1, rows, tn), lambda i, j: (i, 0, j)),
        out_shape=jax.ShapeDtypeStruct((depth, rows, n), F32),
        compiler_params=_cparams("parallel", "parallel"),
        name="modulation",
    )(cond, w_mod, b_mod.reshape(depth, 1, n))


def _normmod_kernel(x_ref, g_ref, sh_ref, sc_ref, o_ref):
    o_ref[...] = _norm_rows(x_ref[...], g_ref[...], sh_ref[0], sc_ref[0], True).astype(o_ref.dtype)


def _normmod(x, g, shift, scale, rows_per_mod):
    m, d = x.shape
    tm = 512
    mod_spec = pl.BlockSpec((1, 1, d), lambda i: ((i * tm) // rows_per_mod, 0, 0))
    return pl.pallas_call(
        _normmod_kernel,
        grid=(m // tm,),
        in_specs=[
            pl.BlockSpec((tm, d), lambda i: (i, 0)),
            pl.BlockSpec((1, d), lambda i: (0, 0)),
            mod_spec,
            mod_spec,
        ],
        out_specs=pl.BlockSpec((tm, d), lambda i: (i, 0)),
        out_shape=jax.ShapeDtypeStruct((m, d), BF16),
        compiler_params=_cparams("parallel"),
        name="normmod",
    )(x, g.reshape(1, d), shift, scale)


def _mm_kernel(a_ref, w_ref, o_ref):
    o_ref[...] = jnp.dot(a_ref[...], w_ref[...], preferred_element_type=F32).astype(o_ref.dtype)


def _matmul(a, w, layer, out_dtype, tm=1024, tn=1024):
    m, k = a.shape
    n = w.shape[2]
    tm, tn = min(tm, m), min(tn, n)
    return pl.pallas_call(
        _mm_kernel,
        grid=(m // tm, n // tn),
        in_specs=[
            pl.BlockSpec((tm, k), lambda i, j: (i, 0)),
            pl.BlockSpec((None, k, tn), lambda i, j: (layer, 0, j)),
        ],
        out_specs=pl.BlockSpec((tm, tn), lambda i, j: (i, j)),
        out_shape=jax.ShapeDtypeStruct((m, n), out_dtype),
        compiler_params=_cparams("parallel", "parallel"),
        name="matmul",
    )(a, w)


def _mm_res_kernel(a_ref, w_ref, x_ref, gate_ref, g_ref, sh_ref, sc_ref, xo_ref, ao_ref, xn_scr, ssq_scr,
                   *, nn, modulate):
    j = pl.program_id(1)
    tn = x_ref.shape[1]
    w = w_ref[:, pl.ds(pl.multiple_of(j * tn, tn), tn)]
    xn = x_ref[...] + gate_ref[0] * jnp.dot(a_ref[...], w, preferred_element_type=F32)
    xo_ref[...] = xn
    xn_scr[j] = xn
    ssq = jnp.sum(xn * xn, axis=-1, keepdims=True)

    @pl.when(j == 0)
    def _():
        ssq_scr[...] = ssq

    @pl.when(j > 0)
    def _():
        ssq_scr[...] += ssq

    @pl.when(j == nn - 1)
    def _():
        inv = lax.rsqrt(ssq_scr[...] * (1.0 / (nn * tn)) + EPS)
        for c in range(nn):
            cols = slice(c * tn, (c + 1) * tn)
            y = xn_scr[c] * inv * g_ref[:, cols]
            if modulate:
                y = y * (1.0 + sc_ref[0][:, cols]) + sh_ref[0][:, cols]
            ao_ref[:, cols] = y.astype(ao_ref.dtype)


def _residual_tile_cols(k):
    return 1024


def _matmul_residual(a, w, layer, x, gate, g_next, shift, scale, rows_per_mod, modulate=True, a_dtype=BF16):
    m, k = a.shape
    d = w.shape[2]
    tn = _residual_tile_cols(k)
    nn = d // tn
    tm = min(512, m)

    def mod_row(i, j):
        return ((i * tm) // rows_per_mod, 0, 0)

    mod_spec = pl.BlockSpec((1, 1, d), mod_row)
    return pl.pallas_call(
        functools.partial(_mm_res_kernel, nn=nn, modulate=modulate),
        grid=(m // tm, nn),
        in_specs=[
            pl.BlockSpec((tm, k), lambda i, j: (i, 0)),
            pl.BlockSpec((None, k, d), lambda i, j: (layer, 0, 0), pipeline_mode=pl.Buffered(1)),
            pl.BlockSpec((tm, tn), lambda i, j: (i, j)),
            pl.BlockSpec((1, 1, tn), lambda i, j: ((i * tm) // rows_per_mod, 0, j)),
            pl.BlockSpec((1, d), lambda i, j: (0, 0)),
            mod_spec,
            mod_spec,
        ],
        out_specs=[
            pl.BlockSpec((tm, tn), lambda i, j: (i, j)),
            pl.BlockSpec((tm, d), lambda i, j: (i, 0)),
        ],
        out_shape=[
            jax.ShapeDtypeStruct((m, d), F32),
            jax.ShapeDtypeStruct((m, d), a_dtype),
        ],
        scratch_shapes=[pltpu.VMEM((nn, tm, tn), F32), pltpu.VMEM((tm, 1), F32)],
        compiler_params=_cparams("parallel", "arbitrary"),
        name="matmul_residual",
    )(a, w, x, gate, g_next.reshape(1, d), shift, scale)


def _proj_conv_kernel(*refs, rows, width, vertical, gated):
    refs = list(refs)
    a_ref, wg_ref = refs[:2]
    wv_ref = refs.pop(2) if gated else None
    cw_ref, cb_ref, o_ref, u_scr = refs[2:6]
    scratch = refs[6:]
    if vertical:
        (ul_scr, ur_scr), scratch = scratch[:2], scratch[2:]
    if gated:
        v_scr, scratch = scratch[0], scratch[1:]
    if wg_ref.dtype != BF16:
        scratch[0][...] = wg_ref[...].astype(BF16)
        wg_ref = scratch[0]
        if gated:
            scratch[1][...] = wv_ref[...].astype(BF16)
            wv_ref = scratch[1]
    tn = o_ref.shape[1]
    pad = (u_scr.shape[0] - rows) // 2
    rb = min(GATED_BLOCK_ROWS if gated else PROJ_BLOCK_ROWS, rows)
    rc = min(EPILOGUE_ROWS, rb)
    zeros = jnp.zeros((pad, tn), F32)
    for buf in [u_scr] + ([ul_scr, ur_scr] if vertical else []):
        buf[pl.ds(0, pad), :] = zeros
        buf[pl.ds(pad + rows, pad), :] = zeros
    cw = cw_ref[...]
    bias = cb_ref[...]
    row = lax.broadcasted_iota(jnp.int32, (rc, tn), 0)
    everything = slice(0, tn)

    def project(blk, cols=everything):
        a = a_ref[pl.ds(blk * rb, rb), :]
        u_scr[pl.ds(pad + blk * rb, rb), cols] = jnp.dot(a, wg_ref[:, cols], preferred_element_type=F32)
        if vertical:
            col = row & (width - 1)
            for r in range(blk * rb // rc, (blk + 1) * rb // rc):
                chunk = u_scr[pl.ds(pad + r * rc, rc), :]
                ul_scr[pl.ds(pad + r * rc, rc), :] = jnp.where(col != 0, pltpu.roll(chunk, 1, 0), 0.0)
                ur_scr[pl.ds(pad + r * rc, rc), :] = jnp.where(col != width - 1, pltpu.roll(chunk, rc - 1, 0), 0.0)

    def convolve(blk, cols=everything):
        for r in range(blk * rb // rc, (blk + 1) * rb // rc):
            base = pad + r * rc
            if vertical:
                acc = bias
                for di in range(3):
                    off = base + (di - 1) * width
                    acc = acc + ul_scr[pl.ds(off, rc), :] * cw[3 * di:3 * di + 1, :]
                    acc = acc + u_scr[pl.ds(off, rc), :] * cw[3 * di + 1:3 * di + 2, :]
                    acc = acc + ur_scr[pl.ds(off, rc), :] * cw[3 * di + 2:3 * di + 3, :]
            else:
                col = (lax.broadcasted_iota(jnp.int32, (rc, cols.stop - cols.start), 0) + r * rc) & (width - 1)
                left = jnp.where(col != 0, u_scr[pl.ds(base - 1, rc), cols], 0.0)
                right = jnp.where(col != width - 1, u_scr[pl.ds(base + 1, rc), cols], 0.0)
                acc = (left * cw_ref[0:1, cols] + u_scr[pl.ds(base, rc), cols] * cw_ref[1:2, cols]
                       + right * cw_ref[2:3, cols] + cb_ref[:, cols])
            y = _silu(acc)
            if gated:
                v_scr[pl.ds(r * rc, rc), :] = y
            else:
                o_ref[pl.ds(r * rc, rc), cols] = y.astype(o_ref.dtype)

    n_blocks = rows // rb
    for blk in range(n_blocks):
        project(blk)
        if blk >= 1:
            convolve(blk - 1)
    if gated:
        val = jnp.dot(a_ref[...], wv_ref[...], preferred_element_type=F32)
    convolve(n_blocks - 1)
    if gated:
        for r in range(rows // rc):
            chunk = pl.ds(r * rc, rc)
            o_ref[chunk, :] = (v_scr[chunk, :] * val[r * rc:(r + 1) * rc, :]).astype(o_ref.dtype)


def _proj_conv(a, w, layer, conv_w, conv_b, *, width, vertical, n_out, val_offset=None, tn):
    m, k = a.shape
    rows = min(ROW_TILE, m)
    gated = val_offset is not None
    assert width & (width - 1) == 0 and rows % width == 0 and m % rows == 0
    pad = (width + SUBLANES) if vertical else SUBLANES
    taps = conv_w.shape[0]
    in_specs = [
        pl.BlockSpec((rows, k), lambda i, j: (i, 0)),
        pl.BlockSpec((None, k, tn), lambda i, j: (layer, 0, j)),
    ]
    operands = [a, w]
    padded = pltpu.VMEM((rows + 2 * pad, tn), F32)
    scratch = [padded] + ([padded, padded] if vertical else []) + ([pltpu.VMEM((rows, tn), F32)] if gated else [])
    if w.dtype != BF16:
        scratch += [pltpu.VMEM((k, tn), BF16)] * (2 if gated else 1)
    if gated:
        voff = val_offset // tn
        in_specs.append(pl.BlockSpec((None, k, tn), lambda i, j: (layer, 0, j + voff)))
        operands.append(w)
    in_specs += [
        pl.BlockSpec((taps, tn), lambda i, j: (0, j)),
        pl.BlockSpec((1, tn), lambda i, j: (0, j)),
    ]
    operands += [conv_w, conv_b.reshape(1, n_out)]
    return pl.pallas_call(
        functools.partial(_proj_conv_kernel, rows=rows, width=width, vertical=vertical, gated=gated),
        grid=(m // rows, n_out // tn),
        in_specs=in_specs,
        out_specs=pl.BlockSpec((rows, tn), lambda i, j: (i, j)),
        out_shape=jax.ShapeDtypeStruct((m, n_out), BF16),
        scratch_shapes=scratch,
        compiler_params=_cparams("parallel", "parallel"),
        name="proj_conv",
    )(*operands)


def _dft_tables(n, sin_sign):
    s = 64
    k = jnp.arange(n, dtype=jnp.int32)[:, None]

    def narrow(cols):
        ang = ((k * cols[None, :]) % n).astype(F32) * F32(2.0 * math.pi / n)
        return jnp.cos(ang), jnp.sin(ang)

    ca, sa = narrow(jnp.arange(n // s, dtype=jnp.int32) * s)
    cb, sb = narrow(jnp.arange(s, dtype=jnp.int32))
    scale = F32(1.0 / math.sqrt(n))
    p = jnp.stack([ca, sin_sign * sa], axis=1)[:, :, :, None] * scale
    q = jnp.stack([-sa, sin_sign * ca], axis=1)[:, :, :, None] * scale
    return (p * cb[:, None, None, :] + q * sb[:, None, None, :]).reshape(n, 2 * n)


def _chan_dft_kernel(a_ref, wc_ref, o_ref, *, groups):
    dg = wc_ref.shape[0]
    wc = wc_ref[...]
    for g in range(groups):
        cols = pl.ds(g * dg, dg)
        res = jnp.dot(a_ref[:, cols], wc, preferred_element_type=F32)
        o_ref[0, :, cols] = res[:, :dg].astype(o_ref.dtype)
        o_ref[1, :, cols] = res[:, dg:].astype(o_ref.dtype)


def _chan_dft(a, wc, nb, ln):
    d = a.shape[1]
    dg = wc.shape[0]
    tm = min(512, ln)
    mt = ln // tm
    return pl.pallas_call(
        functools.partial(_chan_dft_kernel, groups=d // dg),
        grid=(nb, mt),
        in_specs=[
            pl.BlockSpec((tm, d), lambda b, i: (b * mt + i, 0)),
            pl.BlockSpec((dg, 2 * dg), lambda b, i: (0, 0)),
        ],
        out_specs=pl.BlockSpec((None, 2, tm, d), lambda b, i: (b, 0, i, 0)),
        out_shape=jax.ShapeDtypeStruct((nb, 2, ln, d), BF16),
        compiler_params=_cparams("parallel", "parallel"),
        name="chan_dft",
    )(a, wc)


def _pos_dft(cs, y):
    nb, k2, d = y.shape
    ln = cs.shape[0]
    tm, tn = min(1024, ln), 1024
    return pl.pallas_call(
        _mm_kernel,
        grid=(nb, d // tn, ln // tm),
        in_specs=[
            pl.BlockSpec((tm, k2), lambda b, j, i: (i, 0)),
            pl.BlockSpec((None, k2, tn), lambda b, j, i: (b, 0, j)),
        ],
        out_specs=pl.BlockSpec((None, tm, tn), lambda b, j, i: (b, i, j)),
        out_shape=jax.ShapeDtypeStruct((nb, ln, d), BF16),
        compiler_params=_cparams("parallel", "parallel", "parallel"),
        name="pos_dft",
    )(cs, y)


def _fourier(a, nb, ln, wc, cs):
    d = a.shape[1]
    y = _chan_dft(a, wc, nb, ln)
    f = _pos_dft(cs, y.reshape(nb, 2 * ln, d))
    return f.reshape(nb * ln, d)


def _ssd_dt_kernel(a_ref, wt_ref, bias_ref, alog_ref, dtt_ref, acst_ref, *, heads, groups):
    ln = a_ref.shape[0]
    q = SSD_CHUNK
    r = heads // groups
    dtt = jax.nn.softplus(
        lax.dot_general(wt_ref[...], a_ref[...], (((1,), (1,)), ((), ())), preferred_element_type=F32)
        + bias_ref[...])
    dtat = dtt * (-jnp.exp(alog_ref[...]))
    ri = lax.broadcasted_iota(jnp.int32, (q, q), 0)
    ci = lax.broadcasted_iota(jnp.int32, (q, q), 1)
    lower = (ri >= ci).astype(F32)
    upper = (ri <= ci).astype(F32)
    fwd_row = lax.broadcasted_iota(jnp.int32, (2 * heads, q), 0) < heads
    for c in range(ln // q):
        xt = dtat[:, c * q:(c + 1) * q]
        prefix = jnp.dot(xt, upper, precision=_HIGHEST, preferred_element_type=F32)
        suffix = jnp.dot(xt, lower, precision=_HIGHEST, preferred_element_type=F32)
        acs = jnp.where(fwd_row, prefix, suffix)
        dtc = dtt[:, c * q:(c + 1) * q]
        for g in range(groups):
            for half in range(2):
                src = slice(half * heads + g * r, half * heads + (g + 1) * r)
                dst = pl.ds(half * r, r)
                acst_ref[g, c, dst, :] = acs[src, :]
                dtt_ref[g, c, dst, :] = dtc[src, :]


def _ssd_dt(a, w_dt_t, dt_bias, a_log, nb, ln):
    d = a.shape[1]
    h2 = w_dt_t.shape[0]
    groups = SSD_GROUPS
    nc = ln // SSD_CHUNK
    vec = pl.BlockSpec((h2, 1), lambda b: (0, 0))
    out = pl.BlockSpec((None, groups, nc, h2 // groups, SSD_CHUNK), lambda b: (b, 0, 0, 0, 0))
    shape = jax.ShapeDtypeStruct((nb, groups, nc, h2 // groups, SSD_CHUNK), F32)
    return pl.pallas_call(
        functools.partial(_ssd_dt_kernel, heads=h2 // 2, groups=groups),
        grid=(nb,),
        in_specs=[
            pl.BlockSpec((ln, d), lambda b: (b, 0)),
            pl.BlockSpec((h2, d), lambda b: (0, 0)),
            vec, vec,
        ],
        out_specs=[out, out],
        out_shape=[shape, shape],
        compiler_params=_cparams("parallel"),
        name="ssd_dt",
    )(a, w_dt_t, dt_bias.reshape(h2, 1), a_log.reshape(h2, 1))


def _ssd_scan_kernel(*refs, nc, heads_per_group, emit):
    if emit:
        (xs_ref, b_ref, c_ref, z_ref, acst_ref, dtt_ref, h0_ref, d0_ref, d1_ref, ng_ref, o_ref, hfin_ref,
         yf_scr, h_scr, tok_scr) = refs
    else:
        xs_ref, b_ref, acst_ref, dtt_ref, h0_ref, hfin_ref, h_scr, tok_scr = refs
    q = SSD_CHUNK
    r_heads = heads_per_group
    p = SSD_HEAD_DIM
    gw = r_heads * p
    expand = (lax.broadcasted_iota(jnp.int32, (2 * r_heads, 2 * gw), 1) // p
              == lax.broadcasted_iota(jnp.int32, (2 * r_heads, 2 * gw), 0)).astype(BF16)
    li = lax.broadcasted_iota(jnp.int32, (q, q), 0)
    si = lax.broadcasted_iota(jnp.int32, (q, q), 1)
    lane = lax.broadcasted_iota(jnp.int32, (q, gw), 1)
    even_head = (lane % (2 * p)) < p
    pad_rows = jnp.zeros((q - 3 * r_heads, q), F32)
    if emit:
        dskip = d0_ref[...] + d1_ref[...]
        norm_g = ng_ref[...]

    def chunk(c, direction):
        rows = pl.ds(pl.multiple_of(c * q, q), q)
        xs_bf = xs_ref[rows, :]
        bc = b_ref[rows, :]
        tok = tok_scr[direction, c]
        fac = tok[:, :2 * r_heads]
        fac_e = jnp.dot(fac.astype(BF16), expand, preferred_element_type=F32)
        ea_e = fac_e[:, :gw]
        wf_e = fac_e[:, gw:]
        h_decay = ea_e[q - 1:q, :] if direction == 0 else ea_e[0:1, :]
        h = h_scr[...]

        def update_state():
            w = xs_bf * wf_e.astype(BF16)
            h_scr[...] = h * h_decay + lax.dot_general(
                bc, w, (((0,), (0,)), ((), ())), preferred_element_type=F32)

        if not emit:
            update_state()
            return
        cc = c_ref[rows, :]
        src_rows = acst_ref[c] - jnp.log(dtt_ref[c])
        causal = (li >= si) if direction == 0 else (li <= si)
        scores = lax.dot_general(cc, bc, (((1,), (1,)), ((), ())), preferred_element_type=F32)
        y = jnp.dot(cc, h.astype(BF16), preferred_element_type=F32) * ea_e
        xs_even = jnp.where(even_head, xs_bf, jnp.zeros_like(xs_bf))
        xs_odd = jnp.where(even_head, jnp.zeros_like(xs_bf), xs_bf)
        pieces = []
        for pair in range(r_heads // 2):
            ms = []
            for sub in range(2):
                r = 2 * pair + sub
                col = direction * r_heads + r
                seg = tok[:, 2 * r_heads + r:2 * r_heads + r + 1] - src_rows[col:col + 1, :]
                decay = jnp.exp(jnp.where(causal, seg, -jnp.inf))
                ms.append((scores * decay).astype(BF16))
            cols = slice(pair * 2 * p, (pair + 1) * 2 * p)
            pieces.append(jnp.dot(jnp.concatenate(ms, axis=1),
                                  jnp.concatenate([xs_even[:, cols], xs_odd[:, cols]], axis=0),
                                  preferred_element_type=F32))
        y = y + jnp.concatenate(pieces, axis=1)
        update_state()
        if direction == 0:
            yf_scr[rows, :] = y
        else:
            tot = yf_scr[rows, :] + y + dskip * xs_bf.astype(F32)
            gz = tot * _silu(z_ref[rows, :].astype(F32))
            gz = gz * lax.rsqrt(jnp.mean(gz * gz, axis=-1, keepdims=True) + EPS)
            o_ref[rows, :] = (gz * norm_g).astype(o_ref.dtype)

    def to_token_major(c, carry):
        acs_rows = acst_ref[c]
        dt_rows = dtt_ref[c]
        for direction in range(2):
            acs_d = acs_rows[direction * r_heads:(direction + 1) * r_heads, :]
            dt_d = dt_rows[direction * r_heads:(direction + 1) * r_heads, :]
            last = acs_d[:, q - 1:q] if direction == 0 else acs_d[:, 0:1]
            tok_scr[direction, c] = jnp.concatenate(
                [jnp.exp(acs_d), dt_d * jnp.exp(last - acs_d), acs_d, pad_rows], axis=0).T
        return carry

    lax.fori_loop(0, nc, to_token_major, 0, unroll=True)
    h_scr[...] = h0_ref[0]

    def fwd_body(c, carry):
        chunk(c, 0)
        return carry

    lax.fori_loop(0, nc, fwd_body, 0, unroll=min(SCAN_UNROLL, nc))
    hfin_ref[0] = h_scr[...]
    h_scr[...] = h0_ref[1]

    def bwd_body(i, carry):
        chunk(nc - 1 - i, 1)
        return carry

    lax.fori_loop(0, nc, bwd_body, 0, unroll=min(SCAN_UNROLL, nc))
    hfin_ref[1] = h_scr[...]


def _ssd_scan(xb, cm, z, acst4, dtt4, h0, d0e, d1e, norm_g, nb, ln):
    emit = z is not None
    groups = SSD_GROUPS
    n = SSD_STATE
    gw = h0.shape[-1]
    d_inner = gw * groups
    r2 = acst4.shape[-2]
    nc = ln // SSD_CHUNK
    b_blk = d_inner // n
    chunk_g = pl.BlockSpec((None, None, nc, r2, SSD_CHUNK), lambda b, g: (b, g, 0, 0, 0))
    state = pl.BlockSpec((None, None, 2, n, gw), lambda b, g: (b, g, 0, 0, 0))
    vec = pl.BlockSpec((1, gw), lambda b, g: (0, g))
    seq_x = pl.BlockSpec((ln, gw), lambda b, g: (b, g))
    seq_b = pl.BlockSpec((ln, n), lambda b, g: (b, b_blk + g))
    seq_c = pl.BlockSpec((ln, n), lambda b, g: (b, g))
    state_shape = jax.ShapeDtypeStruct(h0.shape, F32)
    scratch = [pltpu.VMEM((n, gw), F32), pltpu.VMEM((2, nc, SSD_CHUNK, SSD_CHUNK), F32)]
    if emit:
        in_specs = [seq_x, seq_b, seq_c, seq_x, chunk_g, chunk_g, state, vec, vec, vec]
        operands = (xb, xb, cm, z, acst4, dtt4, h0, d0e, d1e, norm_g)
        out_specs = [seq_x, state]
        out_shape = [jax.ShapeDtypeStruct((nb * ln, d_inner), BF16), state_shape]
        scratch = [pltpu.VMEM((ln, gw), F32)] + scratch
    else:
        in_specs = [seq_x, seq_b, chunk_g, chunk_g, state]
        operands = (xb, xb, acst4, dtt4, h0)
        out_specs = [state]
        out_shape = [state_shape]
    res = pl.pallas_call(
        functools.partial(_ssd_scan_kernel, nc=nc, heads_per_group=r2 // 2, emit=emit),
        grid=(nb, groups),
        in_specs=in_specs,
        out_specs=out_specs,
        out_shape=out_shape,
        scratch_shapes=scratch,
        compiler_params=_cparams("parallel", "parallel"),
        name="ssd_scan",
    )(*operands)
    return res if emit else (None, res[0])


def _ssd_mixer(a, nb, ln, weights, layer, h0, emit=True):
    w_in, w_c, w_z, w_dt_t, conv_w, conv_b, dt_bias, a_log, d0e, d1e, norm_g = weights
    d_inner = w_z.shape[2]
    n_xb = d_inner + SSD_GROUPS * SSD_STATE
    xb = _proj_conv(a, w_in, layer, conv_w[:, :n_xb], conv_b[:n_xb], width=ln, vertical=False, n_out=n_xb, tn=512)
    cm = z = None
    if emit:
        cm = _proj_conv(a, w_c, layer, conv_w[:, n_xb:], conv_b[n_xb:], width=ln, vertical=False,
                        n_out=w_c.shape[2], tn=512)
        z = _matmul(a, w_z, layer, BF16)
    dtt4, acst4 = _ssd_dt(a, w_dt_t, dt_bias, a_log, nb, ln)
    return _ssd_scan(xb, cm, z, acst4, dtt4, h0, d0e, d1e, norm_g, nb, ln)


def kernel(x, c, ctx, c_ctx, w_mod, b_mod, norm_mix_g, norm_ffn_g, four_w, ssd_w_in, ssd_conv_w, ssd_conv_b,
           ssd_dt_bias, ssd_a_log, ssd_d, ssd_norm_g, ssd_w_out, ffn_w_up, ffn_conv_w, ffn_conv_b, ffn_w_down,
           final_g):
    nb, ln, d = x.shape
    lc = ctx.shape[1]
    depth = w_mod.shape[0]
    d_ff = ffn_w_down.shape[1]
    d_inner = ssd_w_out.shape[1]
    heads = ssd_dt_bias.shape[2]
    gn = SSD_GROUPS * SSD_STATE
    t_lat, t_ctx = nb * ln, nb * lc

    mod_rows = -(-(nb + 1) // SUBLANES) * SUBLANES
    cond = jnp.zeros((mod_rows, d), F32).at[:nb].set(c).at[nb].set(c_ctx)
    mods = _modulation(cond, w_mod, b_mod)

    def lat_mod(i, j):
        return mods[i, :nb, j * d:(j + 1) * d].reshape(nb, 1, d)

    def ctx_mod(i, j):
        return mods[i, nb:nb + 1, j * d:(j + 1) * d].reshape(1, 1, d)

    four_bf = four_w.astype(BF16)
    w_down_bf = ffn_w_down.astype(BF16)
    w_out_bf = ssd_w_out.astype(BF16)
    xb = d_inner + gn
    state_cols = xb + 2 * heads
    w_c_bf = ssd_w_in[:, :, state_cols:state_cols + gn].astype(BF16)
    w_z_bf = ssd_w_in[:, :, state_cols + gn:].astype(BF16)
    w_dt_t_bf = lax.optimization_barrier(ssd_w_in[:, :, xb:state_cols]).transpose(0, 2, 1).astype(BF16)
    conv9 = ffn_conv_w.reshape(depth, 9, d_ff)

    dg = d // FOURIER_GROUPS
    wc = _dft_tables(dg, 1.0).astype(BF16)
    cs_lat = _dft_tables(ln, -1.0).astype(BF16)
    cs_ctx = _dft_tables(lc, -1.0).astype(BF16)

    xl = x.reshape(t_lat, d)
    xc = ctx.reshape(t_ctx, d)
    a_lat = _normmod(xl, norm_mix_g[0], lat_mod(0, 0), lat_mod(0, 1), ln)
    a_ctx = _normmod(xc, norm_mix_g[0], ctx_mod(0, 0), ctx_mod(0, 1), t_ctx)
    out = None
    for i in range(depth):
        last = i == depth - 1
        is_ssd = i % 2 == 1
        j = i // 2
        if is_ssd:
            d0e = jnp.repeat(ssd_d[j, 0], SSD_HEAD_DIM).reshape(1, d_inner)
            d1e = jnp.repeat(ssd_d[j, 1], SSD_HEAD_DIM).reshape(1, d_inner)
            weights = (ssd_w_in, w_c_bf, w_z_bf, w_dt_t_bf[j], ssd_conv_w[j], ssd_conv_b[j], ssd_dt_bias[j], ssd_a_log[j],
                       d0e, d1e, ssd_norm_g[j].reshape(1, d_inner))
            zeros = jnp.zeros((nb, SSD_GROUPS, 2, SSD_STATE, d_inner // SSD_GROUPS), F32)
            mix_ctx, h_ctx = _ssd_mixer(a_ctx, nb, lc, weights, j, zeros, emit=not last)
            mix_lat, _ = _ssd_mixer(a_lat, nb, ln, weights, j, h_ctx)
            w_mix = w_out_bf
        else:
            mix_lat = _fourier(a_lat, nb, ln, wc, cs_lat)
            mix_ctx = None if last else _fourier(a_ctx, nb, lc, wc, cs_ctx)
            w_mix = four_bf
        xl, b_lat = _matmul_residual(mix_lat, w_mix, j, xl, lat_mod(i, 2), norm_ffn_g[i], lat_mod(i, 3),
                                     lat_mod(i, 4), ln)
        act = _proj_conv(b_lat, ffn_w_up, i, conv9[i], ffn_conv_b[i], width=GRID_W, vertical=True,
                         n_out=d_ff, val_offset=d_ff, tn=256)
        if last:
            zero_mod = jnp.zeros((nb, 1, d), F32)
            _, out = _matmul_residual(act, w_down_bf, i, xl, lat_mod(i, 5), final_g, zero_mod, zero_mod, ln,
                                      modulate=False, a_dtype=F32)
        else:
            xl, a_lat = _matmul_residual(act, w_down_bf, i, xl, lat_mod(i, 5), norm_mix_g[i + 1],
                                         lat_mod(i + 1, 0), lat_mod(i + 1, 1), ln)
            xc, b_ctx = _matmul_residual(mix_ctx, w_mix, j, xc, ctx_mod(i, 2), norm_ffn_g[i], ctx_mod(i, 3),
                                         ctx_mod(i, 4), t_ctx)
            act_c = _proj_conv(b_ctx, ffn_w_up, i, ffn_conv_w[i, 1], ffn_conv_b[i], width=lc, vertical=False,
                               n_out=d_ff, val_offset=d_ff, tn=256)
            xc, a_ctx = _matmul_residual(act_c, w_down_bf, i, xc, ctx_mod(i, 5), norm_mix_g[i + 1],
                                         ctx_mod(i + 1, 0), ctx_mod(i + 1, 1), t_ctx)
    return out.reshape(nb, ln, d)
```

```python
import functools
import math

import jax
import jax.numpy as jnp
from jax import lax
from jax.experimental import pallas as pl
from jax.experimental.pallas import tpu as pltpu

F32 = jnp.float32
BF16 = jnp.bfloat16

EPS = 1e-6
GRID_W = 64
FOURIER_GROUPS = 8
SSD_HEAD_DIM = 64
SSD_GROUPS = 8
SSD_STATE = 128
SSD_CHUNK = 128
SCAN_UNROLL = 16

LANES = 128
SUBLANES = 8
VMEM_LIMIT_BYTES = 56 * 1024 * 1024
ROW_TILE = 2048
EPILOGUE_ROWS = 256
PROJ_BLOCK_ROWS = 512
GATED_BLOCK_ROWS = 256

_HIGHEST = lax.Precision.HIGHEST


def _cparams(*sem, flags=None):
    return pltpu.CompilerParams(dimension_semantics=sem, vmem_limit_bytes=VMEM_LIMIT_BYTES, flags=flags)


def _norm_rows(x, g, shift, scale, modulate):
    ms = jnp.mean(x * x, axis=-1, keepdims=True)
    y = x * lax.rsqrt(ms + EPS) * g
    if modulate:
        y = y * (1.0 + scale) + shift
    return y


def _silu(v):
    return v * jax.nn.sigmoid(v)


def _mod_kernel(s_ref, w_ref, b_ref, o_ref):
    s = _silu(s_ref[...]).astype(BF16)
    w = w_ref[0].astype(BF16)
    o_ref[0] = jnp.dot(s, w, preferred_element_type=F32) + b_ref[0]


def _modulation(cond, w_mod, b_mod):
    depth, d, n = w_mod.shape
    rows = cond.shape[0]
    tn = 1024
    return pl.pallas_call(
        _mod_kernel,
        grid=(depth, n // tn),
        in_specs=[
            pl.BlockSpec((rows, d), lambda i, j: (0, 0)),
            pl.BlockSpec((1, d, tn), lambda i, j: (i, 0, j)),
            pl.BlockSpec((1, 1, tn), lambda i, j: (i, 0, j)),
        ],
        out_specs=pl.BlockSpec((1, rows, tn), lambda i, j: (i, 0, j)),
        out_shape=jax.ShapeDtypeStruct((depth, rows, n), F32),
        compiler_params=_cparams("parallel", "parallel"),
        name="modulation",
    )(cond, w_mod, b_mod.reshape(depth, 1, n))


def _normmod_kernel(x_ref, g_ref, sh_ref, sc_ref, o_ref):
    o_ref[...] = _norm_rows(x_ref[...], g_ref[...], sh_ref[0], sc_ref[0], True).astype(o_ref.dtype)


def _normmod(x, g, shift, scale, rows_per_mod):
    m, d = x.shape
    tm = 512
    mod_spec = pl.BlockSpec((1, 1, d), lambda i: ((i * tm) // rows_per_mod, 0, 0))
    return pl.pallas_call(
        _normmod_kernel,
        grid=(m // tm,),
        in_specs=[
            pl.BlockSpec((tm, d), lambda i: (i, 0)),
            pl.BlockSpec((1, d), lambda i: (0, 0)),
            mod_spec,
            mod_spec,
        ],
        out_specs=pl.BlockSpec((tm, d), lambda i: (i, 0)),
        out_shape=jax.ShapeDtypeStruct((m, d), BF16),
        compiler_params=_cparams("parallel"),
        name="normmod",
    )(x, g.reshape(1, d), shift, scale)


def _mm_kernel(a_ref, w_ref, o_ref):
    o_ref[...] = jnp.dot(a_ref[...], w_ref[...], preferred_element_type=F32).astype(o_ref.dtype)


def _matmul(a, w, layer, out_dtype, tm=1024, tn=1024, tile_major=False):
    m, k = a.shape
    n = w.shape[2]
    tm, tn = min(tm, m), min(tn, n)
    return pl.pallas_call(
        _mm_kernel,
        grid=(m // tm, n // tn),
        in_specs=[
            pl.BlockSpec((tm, k), lambda i, j: (i, 0)),
            pl.BlockSpec((None, k, tn), lambda i, j: (layer, 0, j)),
        ],
        out_specs=(pl.BlockSpec((None, tm, tn), lambda i, j: (j, i, 0)) if tile_major
                   else pl.BlockSpec((tm, tn), lambda i, j: (i, j))),
        out_shape=jax.ShapeDtypeStruct((n // tn, m, tn) if tile_major else (m, n), out_dtype),
        compiler_params=_cparams("parallel", "parallel"),
        name="matmul",
    )(a, w)


def _mm_res_kernel(a_ref, w_ref, x_ref, gate_ref, g_ref, sh_ref, sc_ref, xo_ref, ao_ref, xn_scr, ssq_scr,
                   *, nn, modulate):
    j = pl.program_id(1)
    tn = x_ref.shape[1]
    w = w_ref[:, pl.ds(pl.multiple_of(j * tn, tn), tn)]
    xn = x_ref[...] + gate_ref[0] * jnp.dot(a_ref[...], w, preferred_element_type=F32)
    xo_ref[...] = xn
    xn_scr[j] = xn
    ssq = jnp.sum(xn * xn, axis=-1, keepdims=True)

    @pl.when(j == 0)
    def _():
        ssq_scr[...] = ssq

    @pl.when(j > 0)
    def _():
        ssq_scr[...] += ssq

    @pl.when(j == nn - 1)
    def _():
        inv = lax.rsqrt(ssq_scr[...] * (1.0 / (nn * tn)) + EPS)
        for c in range(nn):
            cols = slice(c * tn, (c + 1) * tn)
            y = xn_scr[c] * inv * g_ref[:, cols]
            if modulate:
                y = y * (1.0 + sc_ref[0][:, cols]) + sh_ref[0][:, cols]
            ao_ref[:, cols] = y.astype(ao_ref.dtype)


def _residual_tile_cols(k):
    return 1024


def _matmul_residual(a, w, layer, x, gate, g_next, shift, scale, rows_per_mod, modulate=True, a_dtype=BF16):
    m, k = a.shape
    d = w.shape[2]
    tn = _residual_tile_cols(k)
    nn = d // tn
    tm = min(512, m)

    def mod_row(i, j):
        return ((i * tm) // rows_per_mod, 0, 0)

    mod_spec = pl.BlockSpec((1, 1, d), mod_row)
    return pl.pallas_call(
        functools.partial(_mm_res_kernel, nn=nn, modulate=modulate),
        grid=(m // tm, nn),
        in_specs=[
            pl.BlockSpec((tm, k), lambda i, j: (i, 0)),
            pl.BlockSpec((None, k, d), lambda i, j: (layer, 0, 0), pipeline_mode=pl.Buffered(1)),
            pl.BlockSpec((tm, tn), lambda i, j: (i, j)),
            pl.BlockSpec((1, 1, tn), lambda i, j: ((i * tm) // rows_per_mod, 0, j)),
            pl.BlockSpec((1, d), lambda i, j: (0, 0)),
            mod_spec,
            mod_spec,
        ],
        out_specs=[
            pl.BlockSpec((tm, tn), lambda i, j: (i, j)),
            pl.BlockSpec((tm, d), lambda i, j: (i, 0)),
        ],
        out_shape=[
            jax.ShapeDtypeStruct((m, d), F32),
            jax.ShapeDtypeStruct((m, d), a_dtype),
        ],
        scratch_shapes=[pltpu.VMEM((nn, tm, tn), F32), pltpu.VMEM((tm, 1), F32)],
        compiler_params=_cparams("parallel", "arbitrary"),
        name="matmul_residual",
    )(a, w, x, gate, g_next.reshape(1, d), shift, scale)


def _proj_conv_kernel(*refs, rows, width, vertical, gated):
    refs = list(refs)
    a_ref, wg_ref = refs[:2]
    wv_ref = refs.pop(2) if gated else None
    cw_ref, cb_ref, o_ref, u_scr = refs[2:6]
    scratch = refs[6:]
    if vertical:
        (ul_scr, ur_scr), scratch = scratch[:2], scratch[2:]
    if gated:
        v_scr, scratch = scratch[0], scratch[1:]
    if wg_ref.dtype != BF16:
        scratch[0][...] = wg_ref[...].astype(BF16)
        wg_ref = scratch[0]
        if gated:
            scratch[1][...] = wv_ref[...].astype(BF16)
            wv_ref = scratch[1]
    tn = o_ref.shape[1]
    pad = (u_scr.shape[0] - rows) // 2
    rb = min(GATED_BLOCK_ROWS if gated else PROJ_BLOCK_ROWS, rows)
    rc = min(EPILOGUE_ROWS, rb)
    zeros = jnp.zeros((pad, tn), F32)
    for buf in [u_scr] + ([ul_scr, ur_scr] if vertical else []):
        buf[pl.ds(0, pad), :] = zeros
        buf[pl.ds(pad + rows, pad), :] = zeros
    cw = cw_ref[...]
    bias = cb_ref[...]
    row = lax.broadcasted_iota(jnp.int32, (rc, tn), 0)
    everything = slice(0, tn)

    def project(blk, cols=everything):
        a = a_ref[pl.ds(blk * rb, rb), :]
        u_scr[pl.ds(pad + blk * rb, rb), cols] = jnp.dot(a, wg_ref[:, cols], preferred_element_type=F32)
        if vertical:
            col = row & (width - 1)
            for r in range(blk * rb // rc, (blk + 1) * rb // rc):
                chunk = u_scr[pl.ds(pad + r * rc, rc), :]
                ul_scr[pl.ds(pad + r * rc, rc), :] = jnp.where(col != 0, pltpu.roll(chunk, 1, 0), 0.0)
                ur_scr[pl.ds(pad + r * rc, rc), :] = jnp.where(col != width - 1, pltpu.roll(chunk, rc - 1, 0), 0.0)

    def convolve(blk, cols=everything):
        for r in range(blk * rb // rc, (blk + 1) * rb // rc):
            base = pad + r * rc
            if vertical:
                acc = bias
                for di in range(3):
                    off = base + (di - 1) * width
                    acc = acc + ul_scr[pl.ds(off, rc), :] * cw[3 * di:3 * di + 1, :]
                    acc = acc + u_scr[pl.ds(off, rc), :] * cw[3 * di + 1:3 * di + 2, :]
                    acc = acc + ur_scr[pl.ds(off, rc), :] * cw[3 * di + 2:3 * di + 3, :]
            else:
                col = (lax.broadcasted_iota(jnp.int32, (rc, cols.stop - cols.start), 0) + r * rc) & (width - 1)
                left = jnp.where(col != 0, u_scr[pl.ds(base - 1, rc), cols], 0.0)
                right = jnp.where(col != width - 1, u_scr[pl.ds(base + 1, rc), cols], 0.0)
                acc = (left * cw_ref[0:1, cols] + u_scr[pl.ds(base, rc), cols] * cw_ref[1:2, cols]
                       + right * cw_ref[2:3, cols] + cb_ref[:, cols])
            y = _silu(acc)
            if gated:
                v_scr[pl.ds(r * rc, rc), :] = y
            else:
                o_ref[pl.ds(r * rc, rc), cols] = y.astype(o_ref.dtype)

    n_blocks = rows // rb
    for blk in range(n_blocks):
        project(blk)
        if blk >= 1:
            convolve(blk - 1)
    if gated:
        val = jnp.dot(a_ref[...], wv_ref[...], preferred_element_type=F32)
    convolve(n_blocks - 1)
    if gated:
        for r in range(rows // rc):
            chunk = pl.ds(r * rc, rc)
            o_ref[chunk, :] = (v_scr[chunk, :] * val[r * rc:(r + 1) * rc, :]).astype(o_ref.dtype)


def _proj_conv(a, w, layer, conv_w, conv_b, *, width, vertical, n_out, val_offset=None, tn, tile_major=False):
    m, k = a.shape
    rows = min(ROW_TILE, m)
    gated = val_offset is not None
    assert width & (width - 1) == 0 and rows % width == 0 and m % rows == 0
    pad = (width + SUBLANES) if vertical else SUBLANES
    taps = conv_w.shape[0]
    in_specs = [
        pl.BlockSpec((rows, k), lambda i, j: (i, 0)),
        pl.BlockSpec((None, k, tn), lambda i, j: (layer, 0, j)),
    ]
    operands = [a, w]
    padded = pltpu.VMEM((rows + 2 * pad, tn), F32)
    scratch = [padded] + ([padded, padded] if vertical else []) + ([pltpu.VMEM((rows, tn), F32)] if gated else [])
    if w.dtype != BF16:
        scratch += [pltpu.VMEM((k, tn), BF16)] * (2 if gated else 1)
    if gated:
        voff = val_offset // tn
        in_specs.append(pl.BlockSpec((None, k, tn), lambda i, j: (layer, 0, j + voff)))
        operands.append(w)
    in_specs += [
        pl.BlockSpec((taps, tn), lambda i, j: (0, j)),
        pl.BlockSpec((1, tn), lambda i, j: (0, j)),
    ]
    operands += [conv_w, conv_b.reshape(1, n_out)]
    return pl.pallas_call(
        functools.partial(_proj_conv_kernel, rows=rows, width=width, vertical=vertical, gated=gated),
        grid=(m // rows, n_out // tn),
        in_specs=in_specs,
        out_specs=(pl.BlockSpec((None, rows, tn), lambda i, j: (j, i, 0)) if tile_major
                   else pl.BlockSpec((rows, tn), lambda i, j: (i, j))),
        out_shape=jax.ShapeDtypeStruct((n_out // tn, m, tn) if tile_major else (m, n_out), BF16),
        scratch_shapes=scratch,
        compiler_params=_cparams("parallel", "parallel"),
        name="proj_conv",
    )(*operands)


def _dft_tables(n, sin_sign):
    s = 64
    k = jnp.arange(n, dtype=jnp.int32)[:, None]

    def narrow(cols):
        ang = ((k * cols[None, :]) % n).astype(F32) * F32(2.0 * math.pi / n)
        return jnp.cos(ang), jnp.sin(ang)

    ca, sa = narrow(jnp.arange(n // s, dtype=jnp.int32) * s)
    cb, sb = narrow(jnp.arange(s, dtype=jnp.int32))
    scale = F32(1.0 / math.sqrt(n))
    p = jnp.stack([ca, sin_sign * sa], axis=1)[:, :, :, None] * scale
    q = jnp.stack([-sa, sin_sign * ca], axis=1)[:, :, :, None] * scale
    return (p * cb[:, None, None, :] + q * sb[:, None, None, :]).reshape(n, 2 * n)


def _chan_dft_kernel(a_ref, wc_ref, o_ref, *, groups):
    dg = wc_ref.shape[0]
    wc = wc_ref[...]
    for g in range(groups):
        cols = pl.ds(g * dg, dg)
        res = jnp.dot(a_ref[:, cols], wc, preferred_element_type=F32)
        o_ref[0, :, cols] = res[:, :dg].astype(o_ref.dtype)
        o_ref[1, :, cols] = res[:, dg:].astype(o_ref.dtype)


def _chan_dft(a, wc, nb, ln):
    d = a.shape[1]
    dg = wc.shape[0]
    tm = min(512, ln)
    mt = ln // tm
    return pl.pallas_call(
        functools.partial(_chan_dft_kernel, groups=d // dg),
        grid=(nb, mt),
        in_specs=[
            pl.BlockSpec((tm, d), lambda b, i: (b * mt + i, 0)),
            pl.BlockSpec((dg, 2 * dg), lambda b, i: (0, 0)),
        ],
        out_specs=pl.BlockSpec((None, 2, tm, d), lambda b, i: (b, 0, i, 0)),
        out_shape=jax.ShapeDtypeStruct((nb, 2, ln, d), BF16),
        compiler_params=_cparams("parallel", "parallel"),
        name="chan_dft",
    )(a, wc)


def _pos_dft(cs, y):
    nb, k2, d = y.shape
    ln = cs.shape[0]
    tm, tn = min(1024, ln), 1024
    return pl.pallas_call(
        _mm_kernel,
        grid=(nb, d // tn, ln // tm),
        in_specs=[
            pl.BlockSpec((tm, k2), lambda b, j, i: (i, 0)),
            pl.BlockSpec((None, k2, tn), lambda b, j, i: (b, 0, j)),
        ],
        out_specs=pl.BlockSpec((None, tm, tn), lambda b, j, i: (b, i, j)),
        out_shape=jax.ShapeDtypeStruct((nb, ln, d), BF16),
        compiler_params=_cparams("parallel", "parallel", "parallel"),
        name="pos_dft",
    )(cs, y)


def _fourier(a, nb, ln, wc, cs):
    d = a.shape[1]
    y = _chan_dft(a, wc, nb, ln)
    f = _pos_dft(cs, y.reshape(nb, 2 * ln, d))
    return f.reshape(nb * ln, d)


def _ssd_dt_kernel(a_ref, wt_ref, bias_ref, alog_ref, dtt_ref, acst_ref, *, heads, groups):
    ln = a_ref.shape[0]
    q = SSD_CHUNK
    r = heads // groups
    dtt = jax.nn.softplus(
        lax.dot_general(wt_ref[...], a_ref[...], (((1,), (1,)), ((), ())), preferred_element_type=F32)
        + bias_ref[...])
    dtat = dtt * (-jnp.exp(alog_ref[...]))
    ri = lax.broadcasted_iota(jnp.int32, (q, q), 0)
    ci = lax.broadcasted_iota(jnp.int32, (q, q), 1)
    lower = (ri >= ci).astype(F32)
    upper = (ri <= ci).astype(F32)
    fwd_row = lax.broadcasted_iota(jnp.int32, (2 * heads, q), 0) < heads
    for c in range(ln // q):
        xt = dtat[:, c * q:(c + 1) * q]
        prefix = jnp.dot(xt, upper, precision=_HIGHEST, preferred_element_type=F32)
        suffix = jnp.dot(xt, lower, precision=_HIGHEST, preferred_element_type=F32)
        acs = jnp.where(fwd_row, prefix, suffix)
        dtc = dtt[:, c * q:(c + 1) * q]
        for g in range(groups):
            for half in range(2):
                src = slice(half * heads + g * r, half * heads + (g + 1) * r)
                dst = pl.ds(half * r, r)
                acst_ref[g, c, dst, :] = acs[src, :]
                dtt_ref[g, c, dst, :] = dtc[src, :]


def _ssd_dt(a, w_dt_t, dt_bias, a_log, nb, ln):
    d = a.shape[1]
    h2 = w_dt_t.shape[0]
    groups = SSD_GROUPS
    nc = ln // SSD_CHUNK
    vec = pl.BlockSpec((h2, 1), lambda b: (0, 0))
    out = pl.BlockSpec((None, groups, nc, h2 // groups, SSD_CHUNK), lambda b: (b, 0, 0, 0, 0))
    shape = jax.ShapeDtypeStruct((nb, groups, nc, h2 // groups, SSD_CHUNK), F32)
    return pl.pallas_call(
        functools.partial(_ssd_dt_kernel, heads=h2 // 2, groups=groups),
        grid=(nb,),
        in_specs=[
            pl.BlockSpec((ln, d), lambda b: (b, 0)),
            pl.BlockSpec((h2, d), lambda b: (0, 0)),
            vec, vec,
        ],
        out_specs=[out, out],
        out_shape=[shape, shape],
        compiler_params=_cparams("parallel"),
        name="ssd_dt",
    )(a, w_dt_t, dt_bias.reshape(h2, 1), a_log.reshape(h2, 1))


def _ssd_scan_kernel(*refs, nc, heads_per_group, emit):
    if emit:
        (xs_ref, b_ref, c_ref, z_ref, acst_ref, dtt_ref, h0_ref, d0_ref, d1_ref, ng_ref, o_ref, hfin_ref,
         yf_scr, h_scr, tok_scr) = refs
    else:
        xs_ref, b_ref, acst_ref, dtt_ref, h0_ref, hfin_ref, h_scr, tok_scr = refs
    q = SSD_CHUNK
    r_heads = heads_per_group
    p = SSD_HEAD_DIM
    gw = r_heads * p
    expand = (lax.broadcasted_iota(jnp.int32, (2 * r_heads, 2 * gw), 1) // p
              == lax.broadcasted_iota(jnp.int32, (2 * r_heads, 2 * gw), 0)).astype(BF16)
    li = lax.broadcasted_iota(jnp.int32, (q, q), 0)
    si = lax.broadcasted_iota(jnp.int32, (q, q), 1)
    lane = lax.broadcasted_iota(jnp.int32, (q, gw), 1)
    even_head = (lane % (2 * p)) < p
    pad_rows = jnp.zeros((q - 3 * r_heads, q), F32)
    if emit:
        dskip = d0_ref[...] + d1_ref[...]
        norm_g = ng_ref[...]

    def chunk(c, direction):
        rows = pl.ds(pl.multiple_of(c * q, q), q)
        xs_bf = xs_ref[rows, :]
        bc = b_ref[rows, :]
        tok = tok_scr[direction, c]
        fac = tok[:, :2 * r_heads]
        fac_e = jnp.dot(fac.astype(BF16), expand, preferred_element_type=F32)
        ea_e = fac_e[:, :gw]
        wf_e = fac_e[:, gw:]
        h_decay = ea_e[q - 1:q, :] if direction == 0 else ea_e[0:1, :]
        h = h_scr[...]

        def update_state():
            w = xs_bf * wf_e.astype(BF16)
            h_scr[...] = h * h_decay + lax.dot_general(
                bc, w, (((0,), (0,)), ((), ())), preferred_element_type=F32)

        if not emit:
            update_state()
            return
        cc = c_ref[rows, :]
        src_rows = acst_ref[c] - jnp.log(dtt_ref[c])
        causal = (li >= si) if direction == 0 else (li <= si)
        scores = lax.dot_general(cc, bc, (((1,), (1,)), ((), ())), preferred_element_type=F32)
        y = jnp.dot(cc, h.astype(BF16), preferred_element_type=F32) * ea_e
        xs_even = jnp.where(even_head, xs_bf, jnp.zeros_like(xs_bf))
        xs_odd = jnp.where(even_head, jnp.zeros_like(xs_bf), xs_bf)
        pieces = []
        for pair in range(r_heads // 2):
            ms = []
            for sub in range(2):
                r = 2 * pair + sub
                col = direction * r_heads + r
                seg = tok[:, 2 * r_heads + r:2 * r_heads + r + 1] - src_rows[col:col + 1, :]
                decay = jnp.exp(jnp.where(causal, seg, -jnp.inf))
                ms.append((scores * decay).astype(BF16))
            cols = slice(pair * 2 * p, (pair + 1) * 2 * p)
            pieces.append(jnp.dot(jnp.concatenate(ms, axis=1),
                                  jnp.concatenate([xs_even[:, cols], xs_odd[:, cols]], axis=0),
                                  preferred_element_type=F32))
        y = y + jnp.concatenate(pieces, axis=1)
        update_state()
        if direction == 0:
            yf_scr[rows, :] = y
        else:
            tot = yf_scr[rows, :] + y + dskip * xs_bf.astype(F32)
            gz = tot * _silu(z_ref[rows, :].astype(F32))
            gz = gz * lax.rsqrt(jnp.mean(gz * gz, axis=-1, keepdims=True) + EPS)
            o_ref[rows, :] = (gz * norm_g).astype(o_ref.dtype)

    def to_token_major(c, carry):
        acs_rows = acst_ref[c]
        dt_rows = dtt_ref[c]
        for direction in range(2):
            acs_d = acs_rows[direction * r_heads:(direction + 1) * r_heads, :]
            dt_d = dt_rows[direction * r_heads:(direction + 1) * r_heads, :]
            last = acs_d[:, q - 1:q] if direction == 0 else acs_d[:, 0:1]
            tok_scr[direction, c] = jnp.concatenate(
                [jnp.exp(acs_d), dt_d * jnp.exp(last - acs_d), acs_d, pad_rows], axis=0).T
        return carry

    lax.fori_loop(0, nc, to_token_major, 0, unroll=True)
    h_scr[...] = h0_ref[0]

    def fwd_body(c, carry):
        chunk(c, 0)
        return carry

    lax.fori_loop(0, nc, fwd_body, 0, unroll=min(SCAN_UNROLL, nc))
    hfin_ref[0] = h_scr[...]
    h_scr[...] = h0_ref[1]

    def bwd_body(i, carry):
        chunk(nc - 1 - i, 1)
        return carry

    lax.fori_loop(0, nc, bwd_body, 0, unroll=min(SCAN_UNROLL, nc))
    hfin_ref[1] = h_scr[...]


def _ssd_scan(xb, cm, z, acst4, dtt4, h0, d0e, d1e, norm_g, nb, ln):
    emit = z is not None
    groups = SSD_GROUPS
    n = SSD_STATE
    gw = h0.shape[-1]
    d_inner = gw * groups
    r2 = acst4.shape[-2]
    nc = ln // SSD_CHUNK
    b_blk = d_inner // n
    chunk_g = pl.BlockSpec((None, None, nc, r2, SSD_CHUNK), lambda b, g: (b, g, 0, 0, 0))
    state = pl.BlockSpec((None, None, 2, n, gw), lambda b, g: (b, g, 0, 0, 0))
    vec = pl.BlockSpec((1, gw), lambda b, g: (0, g))
    seq_x = pl.BlockSpec((ln, gw), lambda b, g: (b, g))
    per_tile = gw // n
    seq_t = pl.BlockSpec((None, ln, gw), lambda b, g: (g, b, 0))
    seq_b = pl.BlockSpec((None, ln, n), lambda b, g: (groups + g // per_tile, b, g % per_tile))
    seq_c = pl.BlockSpec((ln, n), lambda b, g: (b, g))
    state_shape = jax.ShapeDtypeStruct(h0.shape, F32)
    scratch = [pltpu.VMEM((n, gw), F32), pltpu.VMEM((2, nc, SSD_CHUNK, SSD_CHUNK), F32)]
    if emit:
        in_specs = [seq_t, seq_b, seq_c, seq_t, chunk_g, chunk_g, state, vec, vec, vec]
        operands = (xb, xb, cm, z, acst4, dtt4, h0, d0e, d1e, norm_g)
        out_specs = [seq_x, state]
        out_shape = [jax.ShapeDtypeStruct((nb * ln, d_inner), BF16), state_shape]
        scratch = [pltpu.VMEM((ln, gw), F32)] + scratch
    else:
        in_specs = [seq_t, seq_b, chunk_g, chunk_g, state]
        operands = (xb, xb, acst4, dtt4, h0)
        out_specs = [state]
        out_shape = [state_shape]
    res = pl.pallas_call(
        functools.partial(_ssd_scan_kernel, nc=nc, heads_per_group=r2 // 2, emit=emit),
        grid=(nb, groups),
        in_specs=in_specs,
        out_specs=out_specs,
        out_shape=out_shape,
        scratch_shapes=scratch,
        compiler_params=_cparams("parallel", "parallel"),
        name="ssd_scan",
    )(*operands)
    return res if emit else (None, res[0])


def _ssd_mixer(a, nb, ln, weights, layer, h0, emit=True):
    w_in, w_c, w_z, w_dt_t, conv_w, conv_b, dt_bias, a_log, d0e, d1e, norm_g = weights
    d_inner = w_z.shape[2]
    n_xb = d_inner + SSD_GROUPS * SSD_STATE
    gw = d_inner // SSD_GROUPS
    xb = _proj_conv(a, w_in, layer, conv_w[:, :n_xb], conv_b[:n_xb], width=ln, vertical=False, n_out=n_xb, tn=gw,
                    tile_major=True)
    cm = z = None
    if emit:
        cm = _proj_conv(a, w_c, layer, conv_w[:, n_xb:], conv_b[n_xb:], width=ln, vertical=False,
                        n_out=w_c.shape[2], tn=512)
        z = _matmul(a, w_z, layer, BF16, tn=gw, tile_major=True)
    dtt4, acst4 = _ssd_dt(a, w_dt_t, dt_bias, a_log, nb, ln)
    return _ssd_scan(xb, cm, z, acst4, dtt4, h0, d0e, d1e, norm_g, nb, ln)


def kernel(x, c, ctx, c_ctx, w_mod, b_mod, norm_mix_g, norm_ffn_g, four_w, ssd_w_in, ssd_conv_w, ssd_conv_b,
           ssd_dt_bias, ssd_a_log, ssd_d, ssd_norm_g, ssd_w_out, ffn_w_up, ffn_conv_w, ffn_conv_b, ffn_w_down,
           final_g):
    nb, ln, d = x.shape
    lc = ctx.shape[1]
    depth = w_mod.shape[0]
    d_ff = ffn_w_down.shape[1]
    d_inner = ssd_w_out.shape[1]
    heads = ssd_dt_bias.shape[2]
    gn = SSD_GROUPS * SSD_STATE
    t_lat, t_ctx = nb * ln, nb * lc

    mod_rows = -(-(nb + 1) // SUBLANES) * SUBLANES
    cond = jnp.zeros((mod_rows, d), F32).at[:nb].set(c).at[nb].set(c_ctx)
    mods = _modulation(cond, w_mod, b_mod)

    def lat_mod(i, j):
        return mods[i, :nb, j * d:(j + 1) * d].reshape(nb, 1, d)

    def ctx_mod(i, j):
        return mods[i, nb:nb + 1, j * d:(j + 1) * d].reshape(1, 1, d)

    four_bf = four_w.astype(BF16)
    w_down_bf = ffn_w_down.astype(BF16)
    w_out_bf = ssd_w_out.astype(BF16)
    xb = d_inner + gn
    state_cols = xb + 2 * heads
    w_c_bf = ssd_w_in[:, :, state_cols:state_cols + gn].astype(BF16)
    w_z_bf = ssd_w_in[:, :, state_cols + gn:].astype(BF16)
    w_dt_t_bf = lax.optimization_barrier(ssd_w_in[:, :, xb:state_cols]).transpose(0, 2, 1).astype(BF16)
    conv9 = ffn_conv_w.reshape(depth, 9, d_ff)

    dg = d // FOURIER_GROUPS
    wc = _dft_tables(dg, 1.0).astype(BF16)
    cs_lat = _dft_tables(ln, -1.0).astype(BF16)
    cs_ctx = _dft_tables(lc, -1.0).astype(BF16)

    xl = x.reshape(t_lat, d)
    xc = ctx.reshape(t_ctx, d)
    a_lat = _normmod(xl, norm_mix_g[0], lat_mod(0, 0), lat_mod(0, 1), ln)
    a_ctx = _normmod(xc, norm_mix_g[0], ctx_mod(0, 0), ctx_mod(0, 1), t_ctx)
    out = None
    for i in range(depth):
        last = i == depth - 1
        is_ssd = i % 2 == 1
        j = i // 2
        if is_ssd:
            d0e = jnp.repeat(ssd_d[j, 0], SSD_HEAD_DIM).reshape(1, d_inner)
            d1e = jnp.repeat(ssd_d[j, 1], SSD_HEAD_DIM).reshape(1, d_inner)
            weights = (ssd_w_in, w_c_bf, w_z_bf, w_dt_t_bf[j], ssd_conv_w[j], ssd_conv_b[j], ssd_dt_bias[j], ssd_a_log[j],
                       d0e, d1e, ssd_norm_g[j].reshape(1, d_inner))
            zeros = jnp.zeros((nb, SSD_GROUPS, 2, SSD_STATE, d_inner // SSD_GROUPS), F32)
            mix_ctx, h_ctx = _ssd_mixer(a_ctx, nb, lc, weights, j, zeros, emit=not last)
            mix_lat, _ = _ssd_mixer(a_lat, nb, ln, weights, j, h_ctx)
            w_mix = w_out_bf
        else:
            mix_lat = _fourier(a_lat, nb, ln, wc, cs_lat)
            mix_ctx = None if last else _fourier(a_ctx, nb, lc, wc, cs_ctx)
            w_mix = four_bf
        xl, b_lat = _matmul_residual(mix_lat, w_mix, j, xl, lat_mod(i, 2), norm_ffn_g[i], lat_mod(i, 3),
                                     lat_mod(i, 4), ln)
        act = _proj_conv(b_lat, ffn_w_up, i, conv9[i], ffn_conv_b[i], width=GRID_W, vertical=True,
                         n_out=d_ff, val_offset=d_ff, tn=256)
        if last:
            zero_mod = jnp.zeros((nb, 1, d), F32)
            _, out = _matmul_residual(act, w_down_bf, i, xl, lat_mod(i, 5), final_g, zero_mod, zero_mod, ln,
                                      modulate=False, a_dtype=F32)
        else:
            xl, a_lat = _matmul_residual(act, w_down_bf, i, xl, lat_mod(i, 5), norm_mix_g[i + 1],
                                         lat_mod(i + 1, 0), lat_mod(i + 1, 1), ln)
            xc, b_ctx = _matmul_residual(mix_ctx, w_mix, j, xc, ctx_mod(i, 2), norm_ffn_g[i], ctx_mod(i, 3),
                                         ctx_mod(i, 4), t_ctx)
            act_c = _proj_conv(b_ctx, ffn_w_up, i, ffn_conv_w[i, 1], ffn_conv_b[i], width=lc, vertical=False,
                               n_out=d_ff, val_offset=d_ff, tn=256)
            xc, a_ctx = _matmul_residual(act_c, w_down_bf, i, xc, ctx_mod(i, 5), norm_mix_g[i + 1],
                                         ctx_mod(i + 1, 0), ctx_mod(i + 1, 1), t_ctx)
    return out.reshape(nb, ln, d)
```
